```python
import jax, jax.numpy as jnp
from jax import lax
import numpy as np

D_MODEL = 1024
BATCH = 8
SEQ = 4096
DEPTH = 4

N_MIXERS = 2
N_POOL_LAYERS = (DEPTH + 1) // 2
N_DELTA_LAYERS = DEPTH // 2
POOL_WINDOWS = (2, 4, 8, 16)
N_POOL_GROUPS = 4
POOL_GROUP_WIDTH = D_MODEL // N_POOL_GROUPS
DN_HEADS = 8
DN_HEAD_DIM = 128
DN_WIDTH = DN_HEADS * DN_HEAD_DIM
DN_CONV = 4
DN_CHUNK = 64
DN_IN_WIDTH = 4 * DN_WIDTH + 2 * DN_HEADS
N_EXPERTS = 32
TOP_K = 4
D_FF = D_MODEL
SWIGLU_LIMIT = 7.0
SWIGLU_ALPHA = 1.702
MOE_BLOCK = 128
N_MOD = 6
DEEPNORM_ALPHA = (2 * DEPTH) ** 0.25
DEEPNORM_BETA = (8 * DEPTH) ** -0.25
LN_EPS = 1e-5
RMS_EPS = 1e-6

kernel_name = 'hybrid_pool_deltanet_moe_adaln_deepnorm'


def layer_norm(x, g, b):
    xf = x.astype(jnp.float32)
    mu = jnp.mean(xf, axis=-1, keepdims=True)
    var = jnp.mean(jnp.square(xf - mu), axis=-1, keepdims=True)
    return ((xf - mu) * lax.rsqrt(var + LN_EPS) * g + b).astype(x.dtype)


def l2_normalize(x):
    return x * lax.rsqrt(jnp.sum(jnp.square(x), axis=-1, keepdims=True) + RMS_EPS)


def pool_mixer(h, w_grp, scale):
    B, S, D = h.shape
    hg = h.astype(jnp.float32).reshape(B, S, N_POOL_GROUPS, POOL_GROUP_WIDTH)
    cs = jnp.cumsum(hg, axis=1)
    pos = jnp.arange(S)
    outs = []
    for gi, win in enumerate(POOL_WINDOWS):
        csg = cs[:, :, gi]
        lag = jnp.pad(csg, ((0, 0), (win, 0), (0, 0)))[:, :S]
        cnt = jnp.minimum(pos + 1, win).astype(jnp.float32)[None, :, None]
        outs.append((csg - lag) / cnt - hg[:, :, gi])
    pooled = jnp.stack(outs, axis=2)
    y = jnp.einsum('bsgc,gcd->bsgd', pooled, w_grp.astype(jnp.float32)).reshape(B, S, D)
    return (y * scale).astype(h.dtype)


def causal_depthwise_conv(x, w):
    return lax.conv_general_dilated(x, w[:, None, :], window_strides=(1,), padding=[(DN_CONV - 1, 0)],
                                    dimension_numbers=('NWC', 'WIO', 'NWC'),
                                    feature_group_count=x.shape[-1])


def chunk_gated_delta_rule(q, k, v, g, beta):
    B, H, S, dk = q.shape
    dv = v.shape[-1]
    n_chunks = S // DN_CHUNK
    q = q * (dk ** -0.5)
    q = q.reshape(B, H, n_chunks, DN_CHUNK, dk)
    k = k.reshape(B, H, n_chunks, DN_CHUNK, dk)
    v = v.reshape(B, H, n_chunks, DN_CHUNK, dv)
    beta = beta.reshape(B, H, n_chunks, DN_CHUNK)
    g = jnp.cumsum(g.reshape(B, H, n_chunks, DN_CHUNK), axis=-1)
    tril = jnp.tril(jnp.ones((DN_CHUNK, DN_CHUNK), bool))
    strict = jnp.tril(jnp.ones((DN_CHUNK, DN_CHUNK), bool), -1)
    diff = g[..., :, None] - g[..., None, :]
    decay = jnp.where(tril, jnp.exp(jnp.where(tril, diff, 0.0)), 0.0)
    k_beta = k * beta[..., None]
    v_beta = v * beta[..., None]
    L = jnp.where(strict, jnp.einsum('bhnid,bhnjd->bhnij', k_beta, k) * decay, 0.0)
    eye = jnp.eye(DN_CHUNK, dtype=jnp.float32)
    T = lax.linalg.triangular_solve(eye + L, jnp.broadcast_to(eye, L.shape), left_side=True, lower=True)
    u = jnp.einsum('bhnij,bhnjd->bhnid', T, v_beta)
    w = jnp.einsum('bhnij,bhnjd->bhnid', T, k_beta * jnp.exp(g)[..., None])
    a_intra = jnp.where(tril, jnp.einsum('bhnid,bhnjd->bhnij', q, k) * decay, 0.0)
    q_dec = q * jnp.exp(g)[..., None]
    k_dec = k * jnp.exp(g[..., -1:] - g)[..., None]
    g_last = jnp.exp(g[..., -1])

    def step(state, xs):
        u_n, w_n, qd_n, kd_n, a_n, gl_n = xs
        v_new = u_n - jnp.einsum('bhck,bhkv->bhcv', w_n, state)
        o_n = jnp.einsum('bhck,bhkv->bhcv', qd_n, state) + jnp.einsum('bhij,bhjv->bhiv', a_n, v_new)
        state = state * gl_n[..., None, None] + jnp.einsum('bhck,bhcv->bhkv', kd_n, v_new)
        return state, o_n

    xs = tuple(jnp.moveaxis(t, 2, 0) for t in (u, w, q_dec, k_dec, a_intra, g_last))
    state0 = jnp.zeros((B, H, dk, dv), jnp.float32)
    _, o = lax.scan(step, state0, xs)
    return jnp.moveaxis(o, 0, 2).reshape(B, H, S, dv)


def gated_deltanet(h, w_in, conv_w, a_log, dt_bias, norm_w, w_out):
    B, S, _ = h.shape
    proj = h @ w_in
    qkv = jax.nn.silu(causal_depthwise_conv(proj[..., :3 * DN_WIDTH], conv_w)).astype(jnp.float32)
    z = proj[..., 3 * DN_WIDTH:4 * DN_WIDTH].astype(jnp.float32)
    b_logit = proj[..., 4 * DN_WIDTH:4 * DN_WIDTH + DN_HEADS].astype(jnp.float32)
    a_logit = proj[..., 4 * DN_WIDTH + DN_HEADS:].astype(jnp.float32)
    q, k, v = jnp.split(qkv.reshape(B, S, 3, DN_HEADS, DN_HEAD_DIM), 3, axis=2)
    q = l2_normalize(q[:, :, 0]).transpose(0, 2, 1, 3)
    k = l2_normalize(k[:, :, 0]).transpose(0, 2, 1, 3)
    v = v[:, :, 0].transpose(0, 2, 1, 3)
    beta = jax.nn.sigmoid(b_logit).transpose(0, 2, 1)
    g = (-jnp.exp(a_log.astype(jnp.float32)) * jax.nn.softplus(a_logit + dt_bias.astype(jnp.float32))).transpose(0, 2, 1)
    o = chunk_gated_delta_rule(q, k, v, g, beta).transpose(0, 2, 1, 3)
    o = o * lax.rsqrt(jnp.mean(jnp.square(o), axis=-1, keepdims=True) + RMS_EPS) * norm_w
    o = o * jax.nn.silu(z.reshape(B, S, DN_HEADS, DN_HEAD_DIM))
    return o.reshape(B, S, DN_WIDTH).astype(h.dtype) @ w_out


def moe_ffn(h, router_w, router_b, w_gu, b_gu, w_down, b_down):
    B, S, D = h.shape
    n_tok = B * S
    hf = h.reshape(n_tok, D)
    logits = (hf @ router_w + router_b).astype(jnp.float32)
    top_logit, top_idx = lax.top_k(logits, TOP_K)
    probs = jax.nn.softmax(top_logit, axis=-1)
    n_assign = n_tok * TOP_K
    flat_e = top_idx.reshape(n_assign)
    flat_tok = jnp.arange(n_assign, dtype=jnp.int32) // TOP_K
    order = jnp.argsort(flat_e)
    sorted_e = flat_e[order]
    counts = jnp.bincount(flat_e, length=N_EXPERTS)
    padded = (counts + MOE_BLOCK - 1) // MOE_BLOCK * MOE_BLOCK
    pad_end = jnp.cumsum(padded)
    pad_start = pad_end - padded
    seg_start = jnp.cumsum(counts) - counts
    dest_sorted = (pad_start[sorted_e] + jnp.arange(n_assign, dtype=jnp.int32) - seg_start[sorted_e]).astype(jnp.int32)
    cap = n_assign + N_EXPERTS * MOE_BLOCK
    n_blocks = cap // MOE_BLOCK
    row_token = jnp.full((cap,), n_tok, jnp.int32).at[dest_sorted].set(flat_tok[order])
    hf_pad = jnp.concatenate([hf, jnp.zeros((1, D), hf.dtype)], axis=0)
    x_blocks = hf_pad[row_token].reshape(n_blocks, MOE_BLOCK, D)
    block_expert = jnp.minimum(
        jnp.searchsorted(pad_end, jnp.arange(n_blocks, dtype=jnp.int32) * MOE_BLOCK, side='right'),
        N_EXPERTS - 1)

    def expert_block(args):
        xb, e = args
        gu = xb @ w_gu[e] + b_gu[e]
        glu = jnp.minimum(gu[:, :D_FF], SWIGLU_LIMIT)
        lin = jnp.clip(gu[:, D_FF:], -SWIGLU_LIMIT, SWIGLU_LIMIT)
        act = glu * jax.nn.sigmoid(SWIGLU_ALPHA * glu) * (lin + 1.0)
        return act @ w_down[e] + b_down[e]

    y_rows = lax.map(expert_block, (x_blocks, block_expert)).reshape(cap, D)
    dest = jnp.zeros((n_assign,), jnp.int32).at[order].set(dest_sorted)
    y_assign = y_rows[dest].reshape(n_tok, TOP_K, D).astype(jnp.float32)
    out = jnp.einsum('tk,tkd->td', probs, y_assign)
    return out.astype(h.dtype).reshape(B, S, D)


def setup_inputs(seed: int = 0) -> dict:
    key = jax.random.key(seed)
    ks = jax.random.split(key, 24)
    f32 = jnp.float32
    nrm = lambda k, shape, s: jax.random.normal(k, shape, f32) * s
    D = D_MODEL
    dt = jnp.exp(jax.random.uniform(ks[10], (N_DELTA_LAYERS, DN_HEADS), f32, np.log(1e-3), np.log(1e-1)))
    return {
        'x': nrm(ks[0], (BATCH, SEQ, D), 1.0),
        'c': nrm(ks[1], (BATCH, D), 1.0),
        'ada_w': nrm(ks[2], (DEPTH, D, N_MOD * D), 0.02),
        'ada_b': nrm(ks[3], (DEPTH, N_MOD * D), 0.01),
        'ln_g': 1.0 + nrm(ks[4], (DEPTH, 2, D), 0.05),
        'ln_b': nrm(ks[5], (DEPTH, 2, D), 0.02),
        'pool_w': nrm(ks[6], (N_POOL_LAYERS, N_POOL_GROUPS, POOL_GROUP_WIDTH, POOL_GROUP_WIDTH),
                      POOL_GROUP_WIDTH ** -0.5 * DEEPNORM_BETA),
        'pool_scale': 1.0 + nrm(ks[7], (N_POOL_LAYERS, D), 0.1),
        'dn_w_in': nrm(ks[8], (N_DELTA_LAYERS, D, DN_IN_WIDTH), D ** -0.5),
        'dn_conv_w': nrm(ks[9], (N_DELTA_LAYERS, DN_CONV, 3 * DN_WIDTH), DN_CONV ** -0.5),
        'dn_a_log': jnp.log(jax.random.uniform(ks[11], (N_DELTA_LAYERS, DN_HEADS), f32, 1.0, 16.0)),
        'dn_dt_bias': dt + jnp.log(-jnp.expm1(-dt)),
        'dn_norm_w': 1.0 + nrm(ks[12], (N_DELTA_LAYERS, DN_HEAD_DIM), 0.05),
        'dn_w_out': nrm(ks[13], (N_DELTA_LAYERS, DN_WIDTH, D), DN_WIDTH ** -0.5 * DEEPNORM_BETA),
        'router_w': nrm(ks[14], (DEPTH, D, N_EXPERTS), D ** -0.5),
        'router_b': nrm(ks[15], (DEPTH, N_EXPERTS), 0.01),
        'exp_w_gu': nrm(ks[16], (DEPTH, N_EXPERTS, D, 2 * D_FF), D ** -0.5),
        'exp_b_gu': nrm(ks[17], (DEPTH, N_EXPERTS, 2 * D_FF), 0.01),
        'exp_w_down': nrm(ks[18], (DEPTH, N_EXPERTS, D_FF, D), D_FF ** -0.5 * DEEPNORM_BETA),
        'exp_b_down': nrm(ks[19], (DEPTH, N_EXPERTS, D), 0.01),
    }


def reference(x, c, ada_w, ada_b, ln_g, ln_b, pool_w, pool_scale, dn_w_in, dn_conv_w, dn_a_log,
              dn_dt_bias, dn_norm_w, dn_w_out, router_w, router_b, exp_w_gu, exp_b_gu, exp_w_down,
              exp_b_down):
    c_act = jax.nn.silu(c)
    for i in range(DEPTH):
        mods = c_act @ ada_w[i] + ada_b[i]
        sh_mix, sc_mix, gt_mix, sh_ff, sc_ff, gt_ff = [m[:, None, :] for m in jnp.split(mods, N_MOD, axis=-1)]
        h = x * (1.0 + sc_mix) + sh_mix
        j = i // N_MIXERS
        if i % N_MIXERS == 0:
            y = pool_mixer(h, pool_w[j], pool_scale[j])
        else:
            y = gated_deltanet(h, dn_w_in[j], dn_conv_w[j], dn_a_log[j], dn_dt_bias[j], dn_norm_w[j], dn_w_out[j])
        x = layer_norm(DEEPNORM_ALPHA * x + (1.0 + gt_mix) * y, ln_g[i, 0], ln_b[i, 0])
        h = x * (1.0 + sc_ff) + sh_ff
        y = moe_ffn(h, router_w[i], router_b[i], exp_w_gu[i], exp_b_gu[i], exp_w_down[i], exp_b_down[i])
        x = layer_norm(DEEPNORM_ALPHA * x + (1.0 + gt_ff) * y, ln_g[i, 1], ln_b[i, 1])
    return x
```

```python
import functools
import math

import jax
import jax.numpy as jnp
from jax import lax
from jax.experimental import pallas as pl
from jax.experimental.pallas import tpu as pltpu

F32 = jnp.float32
BF16 = jnp.bfloat16
I32 = jnp.int32
U32 = jnp.uint32

N_MOD = 6
POOL_WINDOWS = (2, 4, 8, 16)
DN_HEADS = 8
DN_HEAD_DIM = 128
DN_CONV = 4
TOP_K = 4
SWIGLU_LIMIT = 7.0
SWIGLU_ALPHA = 1.702
LN_EPS = 1e-5
RMS_EPS = 1e-6

LANES = 128
SUBLANES = 8
VMEM_LIMIT_BYTES = 56 * 1024 * 1024

DN_CHUNK = LANES
POOL_HALO = 32
CONV_HALO = SUBLANES

HIGHEST = lax.Precision.HIGHEST


def _tile_config(seq, n_tok):
    return dict(
        pool_ts=min(512, seq),
        dn_in_ts=min(256, seq),
        dn_core_ts=min(512, seq),
        dn_out_ts=min(512, seq),
        router_tt=min(512, seq),
        dispatch_tt=min(512, n_tok),
        expert_bm=512,
        combine_tt=min(256, seq),
    )


def _cparams(n_axes):
    return pltpu.CompilerParams(dimension_semantics=("arbitrary",) * n_axes,
                                vmem_limit_bytes=VMEM_LIMIT_BYTES)


def _layer_norm(v, g, b):
    mu = jnp.mean(v, axis=-1, keepdims=True)
    d = v - mu
    var = jnp.mean(d * d, axis=-1, keepdims=True)
    return d * lax.rsqrt(var + LN_EPS) * g + b


def _dot(a, b):
    return jnp.dot(a, b, preferred_element_type=F32)


def _dot_nt(a, b):
    return lax.dot_general(a, b, (((1,), (1,)), ((), ())), preferred_element_type=F32)


def _dot_tn(a, b):
    return lax.dot_general(a, b, (((0,), (0,)), ((), ())), preferred_element_type=F32)


def _mods_kernel(c_ref, w_ref, b_ref, o_ref):
    c = c_ref[...]
    c_act = c * jax.nn.sigmoid(c)
    o_ref[0] = jnp.dot(c_act, w_ref[0], preferred_element_type=F32, precision=HIGHEST) + b_ref[0]


def _mods(c, ada_w, ada_b):
    depth, d, n = ada_w.shape
    b = c.shape[0]
    tn = 2048 if n % 2048 == 0 else n
    out = pl.pallas_call(
        _mods_kernel,
        grid=(depth, n // tn),
        in_specs=[pl.BlockSpec((b, d), lambda i, j: (0, 0)),
                  pl.BlockSpec((1, d, tn), lambda i, j: (i, 0, j)),
                  pl.BlockSpec((1, 1, tn), lambda i, j: (i, 0, j))],
        out_specs=pl.BlockSpec((1, b, tn), lambda i, j: (i, 0, j)),
        out_shape=jax.ShapeDtypeStruct((depth, b, n), F32),
        compiler_params=_cparams(2),
        name="adaln_mods",
    )(c, ada_w, ada_b.reshape(depth, 1, n))
    return out.reshape(depth, b, N_MOD, d)


def _pool_kernel(x_ref, m_ref, pw_ref, ps_ref, lg_ref, lb_ref, o_ref, e1, ea, eb, *, ts, alpha):
    s = pl.program_id(1)
    d = x_ref.shape[-1]
    gw = d // len(POOL_WINDOWS)
    halo = POOL_HALO
    rows = halo + ts
    x = x_ref[0]
    sh, sc, gt = m_ref[0, 0:1, :], m_ref[0, 1:2, :], m_ref[0, 2:3, :]
    h = x * (1.0 + sc) + sh

    @pl.when(s == 0)
    def _():
        e1[0:halo, :] = jnp.zeros((halo, d), F32)

    e1[halo:rows, :] = h
    ea[8:rows, :] = e1[8:rows, :] + e1[7:rows - 1, :]
    eb[16:rows, gw:] = ea[16:rows, gw:] + ea[14:rows - 2, gw:]
    ea[24:rows, 2 * gw:] = eb[24:rows, 2 * gw:] + eb[20:rows - 4, 2 * gw:]
    eb[32:rows, 3 * gw:] = ea[32:rows, 3 * gw:] + ea[24:rows - 8, 3 * gw:]

    pos = s * ts + lax.broadcasted_iota(I32, (ts, 1), 0)
    outs = []
    for g, win in enumerate(POOL_WINDOWS):
        src = ea if g % 2 == 0 else eb
        cols = slice(g * gw, (g + 1) * gw)
        cnt = jnp.minimum(pos + 1, win).astype(F32)
        pooled = src[halo:rows, cols] / cnt - h[:, cols]
        outs.append(_dot(pooled.astype(BF16), pw_ref[g]))
    y = jnp.concatenate(outs, axis=1) * ps_ref[...]
    o_ref[0] = _layer_norm(alpha * x + (1.0 + gt) * y, lg_ref[...], lb_ref[...])
    e1[0:halo, :] = e1[ts:rows, :]


def _pool_layer(x, m, pool_w, pool_scale, ln_g, ln_b, alpha, cfg):
    b, s, d = x.shape
    ts = cfg["pool_ts"]
    g, gw, _ = pool_w.shape
    assert POOL_WINDOWS == (2, 4, 8, 16) and g == len(POOL_WINDOWS) and s % ts == 0 and ts >= POOL_HALO
    row = lambda v: v.reshape(1, d)
    return pl.pallas_call(
        functools.partial(_pool_kernel, ts=ts, alpha=alpha),
        grid=(b, s // ts),
        in_specs=[pl.BlockSpec((1, ts, d), lambda i, j: (i, j, 0)),
                  pl.BlockSpec((1, N_MOD, d), lambda i, j: (i, 0, 0)),
                  pl.BlockSpec((g, gw, gw), lambda i, j: (0, 0, 0)),
                  pl.BlockSpec((1, d), lambda i, j: (0, 0)),
                  pl.BlockSpec((1, d), lambda i, j: (0, 0)),
                  pl.BlockSpec((1, d), lambda i, j: (0, 0))],
        out_specs=pl.BlockSpec((1, ts, d), lambda i, j: (i, j, 0)),
        out_shape=jax.ShapeDtypeStruct((b, s, d), F32),
        scratch_shapes=[pltpu.VMEM((POOL_HALO + ts, d), F32)] * 3,
        compiler_params=_cparams(2),
        name="pool_layer",
    )(x, m, pool_w.astype(BF16), row(pool_scale), row(ln_g), row(ln_b))


def _dn_in_kernel(x_ref, m_ref, wm_ref, ws_ref, cw_ref, av_ref, dv_ref,
                  q_ref, k_ref, v_ref, z_ref, bg_ref, ext, *, ts):
    s = pl.program_id(1)
    nh, dh = DN_HEADS, DN_HEAD_DIM
    w = nh * dh
    halo = CONV_HALO
    x = x_ref[0]
    sh, sc = m_ref[0, 0:1, :], m_ref[0, 1:2, :]
    h = (x * (1.0 + sc) + sh).astype(BF16)
    proj = _dot(h, wm_ref[...])

    @pl.when(s == 0)
    def _():
        ext[0:halo, :] = jnp.zeros((halo, 3 * w), F32)

    ext[halo:halo + ts, :] = proj[:, :3 * w]
    z_ref[0] = proj[:, 3 * w:]
    conv = cw_ref[0:1, :] * ext[halo - 3:halo - 3 + ts, :]
    for j in range(1, DN_CONV):
        conv = conv + cw_ref[j:j + 1, :] * ext[halo - 3 + j:halo - 3 + j + ts, :]
    act = conv * jax.nn.sigmoid(conv)
    ext[0:halo, :] = ext[ts:ts + halo, :]
    for hd in range(nh):
        qh = act[:, hd * dh:(hd + 1) * dh]
        kh = act[:, w + hd * dh:w + (hd + 1) * dh]
        q_ref[0, hd] = qh * lax.rsqrt(jnp.sum(qh * qh, axis=-1, keepdims=True) + RMS_EPS) * (dh ** -0.5)
        k_ref[0, hd] = kh * lax.rsqrt(jnp.sum(kh * kh, axis=-1, keepdims=True) + RMS_EPS)
        v_ref[0, hd] = act[:, 2 * w + hd * dh:2 * w + (hd + 1) * dh]
    small = _dot(h, ws_ref[...])
    lane = lax.broadcasted_iota(I32, small.shape, 1)
    beta = jax.nn.sigmoid(small)
    g = -jnp.exp(av_ref[...]) * jax.nn.softplus(small + dv_ref[...])
    bg_ref[0] = jnp.where(lane < nh, beta, jnp.where(lane < 2 * nh, g, 0.0))


def _dn_core_kernel(q_ref, k_ref, v_ref, bg_ref, o_ref,
                    st, bb, gcb, egb, ekb, gt, u_s, w_s, ai_s, qd_s, kd_s, *, ts):
    s = pl.program_id(1)
    nh, dh, c = DN_HEADS, DN_HEAD_DIM, DN_CHUNK
    nc = ts // c

    @pl.when(s == 0)
    def _():
        st[...] = jnp.zeros(st.shape, F32)

    bgv = bg_ref[0]
    lane = lax.broadcasted_iota(I32, bgv.shape, 1)
    rowc = lax.broadcasted_iota(I32, bgv.shape, 0) % c
    gc = jnp.where(lane >= nh, bgv, 0.0)
    shift = 1
    while shift < c:
        gc = gc + jnp.where(rowc >= shift, pltpu.roll(gc, shift, 0), 0.0)
        shift *= 2
    gc_t = gc.T
    eg = jnp.exp(gc)
    glast = jnp.concatenate(
        [jnp.broadcast_to(gc[(ci + 1) * c - 1:(ci + 1) * c, :], (c, LANES)) for ci in range(nc)], axis=0)
    ek = jnp.exp(glast - gc)
    for hd in range(nh):
        gt[hd] = jnp.broadcast_to(gc_t[nh + hd:nh + hd + 1, :], (SUBLANES, ts))
        bb[hd] = jnp.broadcast_to(bgv[:, hd:hd + 1], (ts, LANES))
        gcb[hd] = jnp.broadcast_to(gc[:, nh + hd:nh + hd + 1], (ts, LANES))
        egb[hd] = jnp.broadcast_to(eg[:, nh + hd:nh + hd + 1], (ts, LANES))
        ekb[hd] = jnp.broadcast_to(ek[:, nh + hd:nh + hd + 1], (ts, LANES))

    ri = lax.broadcasted_iota(I32, (c, c), 0)
    ci_ = lax.broadcasted_iota(I32, (c, c), 1)
    tril = ri >= ci_
    eye = (ri == ci_).astype(F32)
    off_masks = []
    blk = 1
    while blk < c:
        off_masks.append((ri // blk != ci_ // blk) & (ri // (2 * blk) == ci_ // (2 * blk)) & (ri > ci_))
        blk *= 2

    def head_body(hd, carry):
        qh, kh, vh = q_ref[0, hd], k_ref[0, hd], v_ref[0, hd]
        beta, egh, gch = bb[hd], egb[hd], gcb[hd]
        kb = kh * beta
        vb = vh * beta
        qd_s[hd] = qh * egh
        kd_s[hd] = kh * ekb[hd]
        for ci in range(nc):
            r = slice(ci * c, (ci + 1) * c)
            kc = kh[r].astype(BF16)
            grow = gt[hd, 0:1, r]
            diff = gch[r] - grow
            dec = jnp.where(tril, jnp.exp(jnp.where(tril, diff, 0.0)), 0.0)
            a_kk = _dot_nt(kb[r].astype(BF16), kc)
            a_qk = _dot_nt(qh[r].astype(BF16), kc)
            ai_s[hd, r] = a_qk * dec
            l_mat = a_kk * dec
            t_inv = eye - jnp.where(off_masks[0], l_mat, 0.0)
            for off in off_masks[1:]:
                tb = t_inv.astype(BF16)
                l_off = jnp.where(off, l_mat, 0.0).astype(BF16)
                t_inv = t_inv - _dot(tb, _dot(l_off, tb).astype(BF16))
            rhs = jnp.concatenate([vb[r], kb[r] * egh[r]], axis=1).astype(BF16)
            uw = _dot(t_inv.astype(BF16), rhs)
            u_s[hd, r] = uw[:, :dh]
            w_s[hd, r] = uw[:, dh:]
        return carry

    lax.fori_loop(0, nh, head_body, 0)

    def chunk_body(ci, carry):
        r0 = pl.multiple_of(ci * c, c)
        rows = pl.ds(r0, c)
        for hd in range(nh):
            state = st[hd]
            lhs = jnp.concatenate([w_s[hd, rows, :], qd_s[hd, rows, :]], axis=0).astype(BF16)
            ws_qs = _dot(lhs, state.astype(BF16))
            v_new = u_s[hd, rows, :] - ws_qs[:c]
            v_nb = v_new.astype(BF16)
            o_ref[0, hd, rows, :] = ws_qs[c:] + _dot(ai_s[hd, rows, :].astype(BF16), v_nb)
            g_last = egb[hd, pl.ds(r0 + c - 1, 1), :]
            st[hd] = state * g_last + _dot_tn(kd_s[hd, rows, :].astype(BF16), v_nb)
        return carry

    lax.fori_loop(0, nc, chunk_body, 0)


def _dn_out_kernel(o_ref, z_ref, x_ref, m_ref, nw_ref, wo_ref, lg_ref, lb_ref, out_ref, *, alpha):
    nh, dh = DN_HEADS, DN_HEAD_DIM
    z = z_ref[0]
    parts = []
    for hd in range(nh):
        oh = o_ref[0, hd]
        on = oh * lax.rsqrt(jnp.mean(oh * oh, axis=-1, keepdims=True) + RMS_EPS) * nw_ref[...]
        zh = z[:, hd * dh:(hd + 1) * dh]
        parts.append((on * (zh * jax.nn.sigmoid(zh))).astype(BF16))
    y = _dot(jnp.concatenate(parts, axis=1), wo_ref[...])
    x = x_ref[0]
    gt = m_ref[0, 2:3, :]
    out_ref[0] = _layer_norm(alpha * x + (1.0 + gt) * y, lg_ref[...], lb_ref[...])


def _deltanet_layer(x, m, w_in, conv_w, a_log, dt_bias, norm_w, w_out, ln_g, ln_b, alpha, cfg):
    b, s, d = x.shape
    nh, dh = DN_HEADS, DN_HEAD_DIM
    w = nh * dh
    assert w_in.shape == (d, 4 * w + 2 * nh) and conv_w.shape == (DN_CONV, 3 * w) and 2 * nh <= LANES
    w_main = w_in[:, :4 * w].astype(BF16)
    w_small = jnp.pad(w_in[:, 4 * w:], ((0, 0), (0, LANES - 2 * nh))).astype(BF16)
    avec = jnp.zeros((1, LANES), F32).at[0, nh:2 * nh].set(a_log.astype(F32))
    dvec = jnp.zeros((1, LANES), F32).at[0, nh:2 * nh].set(dt_bias.astype(F32))
    row = lambda v: v.reshape(1, -1)
    const2 = lambda i, j: (0, 0)

    ts = cfg["dn_in_ts"]
    assert s % ts == 0 and ts >= CONV_HALO
    head_major = jax.ShapeDtypeStruct((b, nh, s, dh), F32)
    hm_spec = lambda t: pl.BlockSpec((1, nh, t, dh), lambda i, j: (i, 0, j, 0))
    q, k, v, z, bg = pl.pallas_call(
        functools.partial(_dn_in_kernel, ts=ts),
        grid=(b, s // ts),
        in_specs=[pl.BlockSpec((1, ts, d), lambda i, j: (i, j, 0)),
                  pl.BlockSpec((1, N_MOD, d), lambda i, j: (i, 0, 0)),
                  pl.BlockSpec((d, 4 * w), const2),
                  pl.BlockSpec((d, LANES), const2),
                  pl.BlockSpec((DN_CONV, 3 * w), const2),
                  pl.BlockSpec((1, LANES), const2),
                  pl.BlockSpec((1, LANES), const2)],
        out_specs=[hm_spec(ts), hm_spec(ts), hm_spec(ts),
                   pl.BlockSpec((1, ts, w), lambda i, j: (i, j, 0)),
                   pl.BlockSpec((1, ts, LANES), lambda i, j: (i, j, 0))],
        out_shape=[head_major, head_major, head_major,
                   jax.ShapeDtypeStruct((b, s, w), F32),
                   jax.ShapeDtypeStruct((b, s, LANES), F32)],
        scratch_shapes=[pltpu.VMEM((CONV_HALO + ts, 3 * w), F32)],
        compiler_params=_cparams(2),
        name="deltanet_in",
    )(x, m, w_main, w_small, conv_w, avec, dvec)

    ts = cfg["dn_core_ts"]
    assert s % ts == 0 and ts % DN_CHUNK == 0 and dh == DN_CHUNK
    per_head = pltpu.VMEM((nh, ts, dh), F32)
    o = pl.pallas_call(
        functools.partial(_dn_core_kernel, ts=ts),
        grid=(b, s // ts),
        in_specs=[hm_spec(ts), hm_spec(ts), hm_spec(ts),
                  pl.BlockSpec((1, ts, LANES), lambda i, j: (i, j, 0))],
        out_specs=hm_spec(ts),
        out_shape=head_major,
        scratch_shapes=[pltpu.VMEM((nh, dh, dh), F32),
                        per_head, per_head, per_head, per_head,
                        pltpu.VMEM((nh, SUBLANES, ts), F32),
                        per_head, per_head, per_head, per_head, per_head],
        compiler_params=_cparams(2),
        name="deltanet_core",
    )(q, k, v, bg)

    ts = cfg["dn_out_ts"]
    assert s % ts == 0
    return pl.pallas_call(
        functools.partial(_dn_out_kernel, alpha=alpha),
        grid=(b, s // ts),
        in_specs=[hm_spec(ts),
                  pl.BlockSpec((1, ts, w), lambda i, j: (i, j, 0)),
                  pl.BlockSpec((1, ts, d), lambda i, j: (i, j, 0)),
                  pl.BlockSpec((1, N_MOD, d), lambda i, j: (i, 0, 0)),
                  pl.BlockSpec((1, dh), const2),
                  pl.BlockSpec((w, d), const2),
                  pl.BlockSpec((1, d), const2),
                  pl.BlockSpec((1, d), const2)],
        out_specs=pl.BlockSpec((1, ts, d), lambda i, j: (i, j, 0)),
        out_shape=jax.ShapeDtypeStruct((b, s, d), F32),
        compiler_params=_cparams(2),
        name="deltanet_out",
    )(o, z, x, m, row(norm_w), w_out.astype(BF16), row(ln_g), row(ln_b))


def _router_kernel(x_ref, m_ref, rwt_ref, rb_ref, tri_ref,
                   hp_ref, idx_ref, p_ref, rank_ref, cnt_ref, run, *, tt):
    i = pl.program_id(0)

    @pl.when(i == 0)
    def _():
        run[...] = jnp.zeros(run.shape, F32)

    x = x_ref[...]
    d = x.shape[-1]
    sh, sc = m_ref[0, 3:4, :], m_ref[0, 4:5, :]
    h = x * (1.0 + sc) + sh
    bits = pltpu.bitcast(h.astype(BF16).astype(F32), U32)
    hp_ref[...] = (bits[:, :d // 2] >> 16) | (bits[:, d // 2:] & jnp.uint32(0xFFFF0000))

    logits = lax.dot_general(rwt_ref[...], h, (((1,), (1,)), ((), ())),
                             preferred_element_type=F32, precision=HIGHEST) + rb_ref[...]
    ne = logits.shape[0]
    eio = lax.broadcasted_iota(I32, logits.shape, 0).astype(F32)
    vals, idxs, sels = [], [], []
    for _ in range(TOP_K):
        mx = jnp.max(logits, axis=0, keepdims=True)
        ix = jnp.min(jnp.where(logits == mx, eio, float(ne)), axis=0, keepdims=True)
        sel = eio == ix
        logits = jnp.where(sel, -jnp.inf, logits)
        vals.append(mx)
        idxs.append(ix)
        sels.append(sel)
    exps = [jnp.exp(v - vals[0]) for v in vals]
    den = functools.reduce(lambda a, b_: a + b_, exps)
    chosen = functools.reduce(jnp.logical_or, sels)
    onehot = jnp.where(chosen, 1.0, 0.0)
    before = _dot(onehot.astype(BF16), tri_ref[...]) + run[...]
    ranks = [jnp.sum(jnp.where(sel, before, 0.0), axis=0, keepdims=True) for sel in sels]
    run[...] = run[...] + jnp.sum(onehot, axis=1, keepdims=True)
    idx_ref[...] = jnp.concatenate(idxs, axis=0).astype(I32)
    p_ref[...] = jnp.concatenate([e / den for e in exps], axis=0)
    rank_ref[...] = jnp.concatenate(ranks, axis=0).astype(I32)
    cnt_ref[...] = jnp.broadcast_to(run[...], cnt_ref.shape)


def _dispatch_kernel(dest_ref, hp_ref, xs_in_ref, xs_ref, sem, *, tt):
    del xs_in_ref
    base = pl.program_id(0) * tt

    def row_copy(t, kk):
        return pltpu.make_async_copy(hp_ref.at[pl.ds(base + t, 1)],
                                     xs_ref.at[pl.ds(dest_ref[kk, t], 1)], sem)

    def start(t, carry):
        for kk in range(TOP_K):
            row_copy(t, kk).start()
        return carry

    def wait(t, carry):
        for kk in range(TOP_K):
            row_copy(t, kk).wait()
        return carry

    lax.fori_loop(0, tt, start, 0)
    lax.fori_loop(0, tt, wait, 0)


def _experts_kernel(be_ref, nu_ref, xs_ref, wgu_ref, bgu_ref, wd_ref, bd_ref, y_ref, wgu_b, wd_b):
    j = pl.program_id(0)

    @pl.when(j < nu_ref[0])
    def _():
        new_expert = jnp.logical_or(j == 0, be_ref[j] != be_ref[jnp.maximum(j - 1, 0)])

        @pl.when(new_expert)
        def _():
            wgu_b[...] = wgu_ref[0, 0].astype(BF16)
            wd_b[...] = wd_ref[0, 0].astype(BF16)

        bits = xs_ref[...]
        lo = pltpu.bitcast(bits << 16, F32)
        hi = pltpu.bitcast(bits & jnp.uint32(0xFFFF0000), F32)
        xb = jnp.concatenate([lo, hi], axis=1).astype(BF16)
        gu = _dot(xb, wgu_b[...]) + bgu_ref[0, 0]
        f = gu.shape[1] // 2
        glu = jnp.minimum(gu[:, :f], SWIGLU_LIMIT)
        lin = jnp.clip(gu[:, f:], -SWIGLU_LIMIT, SWIGLU_LIMIT)
        act = glu * jax.nn.sigmoid(SWIGLU_ALPHA * glu) * (lin + 1.0)
        y_ref[...] = _dot(act.astype(BF16), wd_b[...]) + bd_ref[0, 0]


def _combine_kernel(dest_ref, y_ref, x_ref, m_ref, p_ref, lg_ref, lb_ref, o_ref, ybuf, sem, *, tt, alpha):
    def row_copy(t, kk):
        return pltpu.make_async_copy(y_ref.at[pl.ds(dest_ref[kk, t], 1)],
                                     ybuf.at[kk, pl.ds(t, 1)], sem)

    def start(t, carry):
        for kk in range(TOP_K):
            row_copy(t, kk).start()
        return carry

    def wait(t, carry):
        for kk in range(TOP_K):
            row_copy(t, kk).wait()
        return carry

    lax.fori_loop(0, tt, start, 0)
    lax.fori_loop(0, tt, wait, 0)
    p = p_ref[...]
    y = p[:, 0:1] * ybuf[0]
    for kk in range(1, TOP_K):
        y = y + p[:, kk:kk + 1] * ybuf[kk]
    gt = m_ref[0, 5:6, :]
    o_ref[...] = _layer_norm(alpha * x_ref[...] + (1.0 + gt) * y, lg_ref[...], lb_ref[...])


def _moe_layer(x, m, layer, router_w, router_b, w_gu, b_gu, w_down, b_down, ln_g, ln_b, alpha, cfg):
    b, s, d = x.shape
    n_tok = b * s
    ne = router_w.shape[-1]
    f2 = w_gu.shape[-1]
    ff = f2 // 2
    assert d % (2 * LANES) == 0 and w_down.shape[-2:] == (ff, d)
    xf = x.reshape(n_tok, d)
    const2 = lambda i: (0, 0)

    tt = cfg["router_tt"]
    assert s % tt == 0
    tri = jnp.triu(jnp.ones((tt, tt), BF16), k=1)
    hp, idx, probs, rank, cnt = pl.pallas_call(
        functools.partial(_router_kernel, tt=tt),
        grid=(n_tok // tt,),
        in_specs=[pl.BlockSpec((tt, d), lambda i: (i, 0)),
                  pl.BlockSpec((1, N_MOD, d), lambda i: (i * tt // s, 0, 0)),
                  pl.BlockSpec((ne, d), const2),
                  pl.BlockSpec((ne, 1), const2),
                  pl.BlockSpec((tt, tt), const2)],
        out_specs=[pl.BlockSpec((tt, d // 2), lambda i: (i, 0)),
                   pl.BlockSpec((TOP_K, tt), lambda i: (0, i)),
                   pl.BlockSpec((TOP_K, tt), lambda i: (0, i)),
                   pl.BlockSpec((TOP_K, tt), lambda i: (0, i)),
                   pl.BlockSpec((ne, LANES), const2)],
        out_shape=[jax.ShapeDtypeStruct((n_tok, d // 2), U32),
                   jax.ShapeDtypeStruct((TOP_K, n_tok), I32),
                   jax.ShapeDtypeStruct((TOP_K, n_tok), F32),
                   jax.ShapeDtypeStruct((TOP_K, n_tok), I32),
                   jax.ShapeDtypeStruct((ne, LANES), F32)],
        scratch_shapes=[pltpu.VMEM((ne, 1), F32)],
        compiler_params=_cparams(1),
        name="moe_router",
    )(xf, m, router_w[layer].T, router_b[layer].reshape(ne, 1), tri)

    bm = cfg["expert_bm"]
    cap = n_tok * TOP_K + ne * bm
    nb = cap // bm
    counts = cnt[:, 0].astype(I32)
    padded = (counts + bm - 1) // bm * bm
    pad_end = jnp.cumsum(padded)
    pad_start = pad_end - padded
    eids = jnp.arange(ne, dtype=I32)[:, None, None]
    dest = rank + jnp.sum(jnp.where(idx[None] == eids, pad_start[:, None, None], 0), axis=0)
    block_start = jnp.arange(nb, dtype=I32) * bm
    block_expert = jnp.minimum(jnp.sum(block_start[None, :] >= pad_end[:, None], axis=0), ne - 1).astype(I32)
    n_used = (pad_end[-1:] // bm).astype(I32)

    tt = cfg["dispatch_tt"]
    assert n_tok % tt == 0
    xs = pl.pallas_call(
        functools.partial(_dispatch_kernel, tt=tt),
        grid=(n_tok // tt,),
        in_specs=[pl.BlockSpec((TOP_K, tt), lambda i: (0, i), memory_space=pltpu.SMEM),
                  pl.BlockSpec(memory_space=pl.ANY),
                  pl.BlockSpec(memory_space=pl.ANY)],
        out_specs=pl.BlockSpec(memory_space=pl.ANY),
        out_shape=jax.ShapeDtypeStruct((cap, d // 2), U32),
        scratch_shapes=[pltpu.SemaphoreType.DMA],
        input_output_aliases={2: 0},
        compiler_params=_cparams(1),
        name="moe_dispatch",
    )(dest, hp, jnp.zeros((cap, d // 2), U32))

    last = lambda j, be, nu: jnp.minimum(j, nu[0] - 1)
    y_rows = pl.pallas_call(
        _experts_kernel,
        grid_spec=pltpu.PrefetchScalarGridSpec(
            num_scalar_prefetch=2,
            grid=(nb,),
            in_specs=[pl.BlockSpec((bm, d // 2), lambda j, be, nu: (last(j, be, nu), 0)),
                      pl.BlockSpec((1, 1, d, f2), lambda j, be, nu: (layer, be[last(j, be, nu)], 0, 0)),
                      pl.BlockSpec((1, 1, 1, f2), lambda j, be, nu: (layer, be[last(j, be, nu)], 0, 0)),
                      pl.BlockSpec((1, 1, ff, d), lambda j, be, nu: (layer, be[last(j, be, nu)], 0, 0)),
                      pl.BlockSpec((1, 1, 1, d), lambda j, be, nu: (layer, be[last(j, be, nu)], 0, 0))],
            out_specs=pl.BlockSpec((bm, d), lambda j, be, nu: (last(j, be, nu), 0)),
            scratch_shapes=[pltpu.VMEM((d, f2), BF16), pltpu.VMEM((ff, d), BF16)]),
        out_shape=jax.ShapeDtypeStruct((cap, d), F32),
        compiler_params=_cparams(1),
        name="moe_experts",
    )(block_expert, n_used, xs, w_gu, b_gu.reshape(*b_gu.shape[:2], 1, f2),
      w_down, b_down.reshape(*b_down.shape[:2], 1, d))

    tt = cfg["combine_tt"]
    assert s % tt == 0
    out = pl.pallas_call(
        functools.partial(_combine_kernel, tt=tt, alpha=alpha),
        grid=(n_tok // tt,),
        in_specs=[pl.BlockSpec((TOP_K, tt), lambda i: (0, i), memory_space=pltpu.SMEM),
                  pl.BlockSpec(memory_space=pl.ANY),
                  pl.BlockSpec((tt, d), lambda i: (i, 0)),
                  pl.BlockSpec((1, N_MOD, d), lambda i: (i * tt // s, 0, 0)),
                  pl.BlockSpec((tt, TOP_K), lambda i: (i, 0)),
                  pl.BlockSpec((1, d), const2),
                  pl.BlockSpec((1, d), const2)],
        out_specs=pl.BlockSpec((tt, d), lambda i: (i, 0)),
        out_shape=jax.ShapeDtypeStruct((n_tok, d), F32),
        scratch_shapes=[pltpu.VMEM((TOP_K, tt, d), F32), pltpu.SemaphoreType.DMA],
        compiler_params=_cparams(1),
        name="moe_combine",
    )(dest, y_rows, xf, m, probs.T, ln_g.reshape(1, d), ln_b.reshape(1, d))
    return out.reshape(b, s, d)


def kernel(x, c, ada_w, ada_b, ln_g, ln_b, pool_w, pool_scale, dn_w_in, dn_conv_w, dn_a_log, dn_dt_bias,
           dn_norm_w, dn_w_out, router_w, router_b, exp_w_gu, exp_b_gu, exp_w_down, exp_b_down):
    b, s, d = x.shape
    depth = ada_w.shape[0]
    alpha = (2 * depth) ** 0.25
    cfg = _tile_config(s, b * s)
    mods = _mods(c, ada_w, ada_b)
    n_mixers = 2
    for i in range(depth):
        m = mods[i]
        j = i // n_mixers
        if i % n_mixers == 0:
            x = _pool_layer(x, m, pool_w[j], pool_scale[j], ln_g[i, 0], ln_b[i, 0], alpha, cfg)
        else:
            x = _deltanet_layer(x, m, dn_w_in[j], dn_conv_w[j], dn_a_log[j], dn_dt_bias[j], dn_norm_w[j],
                                dn_w_out[j], ln_g[i, 0], ln_b[i, 0], alpha, cfg)
        x = _moe_layer(x, m, i, router_w, router_b, exp_w_gu, exp_b_gu, exp_w_down, exp_b_down,
                       ln_g[i, 1], ln_b[i, 1], alpha, cfg)
    return x
```

```python
import functools

import jax
import jax.numpy as jnp
from jax import lax
from jax.experimental import pallas as pl
from jax.experimental.pallas import tpu as pltpu

F32 = jnp.float32
BF16 = jnp.bfloat16
I32 = jnp.int32
U32 = jnp.uint32

N_MOD = 6
POOL_WINDOWS = (2, 4, 8, 16)
DN_HEADS = 8
DN_HEAD_DIM = 128
DN_CONV = 4
TOP_K = 4
SWIGLU_LIMIT = 7.0
SWIGLU_ALPHA = 1.702
LN_EPS = 1e-5
RMS_EPS = 1e-6

LANES = 128
SUBLANES = 8
VMEM_LIMIT_BYTES = 56 * 1024 * 1024

DN_CHUNK = LANES
DN_HEAD_GROUP = 2
POOL_HALO = 32
CONV_HALO = SUBLANES

HIGHEST = lax.Precision.HIGHEST


def _tile_config(seq, n_tok):
    return dict(
        pool_ts=min(512, seq),
        dn_in_ts=min(256, seq),
        dn_core_ts=min(512, seq),
        dn_out_ts=min(512, seq),
        router_tt=min(512, seq),
        expert_bm=512,
        combine_tt=min(256, seq),
    )


def _cparams(n_axes):
    return pltpu.CompilerParams(dimension_semantics=("arbitrary",) * n_axes,
                                vmem_limit_bytes=VMEM_LIMIT_BYTES)


def _layer_norm(v, g, b):
    mu = jnp.mean(v, axis=-1, keepdims=True)
    d = v - mu
    var = jnp.mean(d * d, axis=-1, keepdims=True)
    return d * lax.rsqrt(var + LN_EPS) * g + b


def _dot(a, b):
    return jnp.dot(a, b, preferred_element_type=F32)


def _dot_nt(a, b):
    return lax.dot_general(a, b, (((1,), (1,)), ((), ())), preferred_element_type=F32)


def _dot_tn(a, b):
    return lax.dot_general(a, b, (((0,), (0,)), ((), ())), preferred_element_type=F32)


def _bdot(a, b):
    return lax.dot_general(a, b, (((2,), (1,)), ((0,), (0,))), preferred_element_type=F32)


def _bdot_nt(a, b):
    return lax.dot_general(a, b, (((2,), (2,)), ((0,), (0,))), preferred_element_type=F32)


def _mods_kernel(c_ref, w_ref, b_ref, o_ref):
    c = c_ref[...]
    c_act = c * jax.nn.sigmoid(c)
    o_ref[0] = jnp.dot(c_act, w_ref[0], preferred_element_type=F32, precision=HIGHEST) + b_ref[0]


def _mods(c, ada_w, ada_b):
    depth, d, n = ada_w.shape
    b = c.shape[0]
    tn = 2048 if n % 2048 == 0 else n
    out = pl.pallas_call(
        _mods_kernel,
        grid=(depth, n // tn),
        in_specs=[pl.BlockSpec((b, d), lambda i, j: (0, 0)),
                  pl.BlockSpec((1, d, tn), lambda i, j: (i, 0, j)),
                  pl.BlockSpec((1, 1, tn), lambda i, j: (i, 0, j))],
        out_specs=pl.BlockSpec((1, b, tn), lambda i, j: (i, 0, j)),
        out_shape=jax.ShapeDtypeStruct((depth, b, n), F32),
        compiler_params=_cparams(2),
        name="adaln_mods",
    )(c, ada_w, ada_b.reshape(depth, 1, n))
    return out.reshape(depth, b, N_MOD, d)


def _pool_kernel(x_ref, m_ref, pw_ref, ps_ref, lg_ref, lb_ref, o_ref, e1, ea, eb, *, ts, alpha):
    s = pl.program_id(1)
    d = x_ref.shape[-1]
    gw = d // len(POOL_WINDOWS)
    halo = POOL_HALO
    rows = halo + ts
    x = x_ref[0]
    sh, sc, gt = m_ref[0, 0:1, :], m_ref[0, 1:2, :], m_ref[0, 2:3, :]
    h = x * (1.0 + sc) + sh

    @pl.when(s == 0)
    def _():
        e1[0:halo, :] = jnp.zeros((halo, d), F32)

    e1[halo:rows, :] = h
    ea[8:rows, :] = e1[8:rows, :] + e1[7:rows - 1, :]
    eb[16:rows, gw:] = ea[16:rows, gw:] + ea[14:rows - 2, gw:]
    ea[24:rows, 2 * gw:] = eb[24:rows, 2 * gw:] + eb[20:rows - 4, 2 * gw:]
    eb[32:rows, 3 * gw:] = ea[32:rows, 3 * gw:] + ea[24:rows - 8, 3 * gw:]

    pos = s * ts + lax.broadcasted_iota(I32, (ts, 1), 0)
    outs = []
    for g, win in enumerate(POOL_WINDOWS):
        src = ea if g % 2 == 0 else eb
        cols = slice(g * gw, (g + 1) * gw)
        cnt = jnp.minimum(pos + 1, win).astype(F32)
        pooled = src[halo:rows, cols] / cnt - h[:, cols]
        outs.append(_dot(pooled.astype(BF16), pw_ref[g]))
    y = jnp.concatenate(outs, axis=1) * ps_ref[...]
    o_ref[0] = _layer_norm(alpha * x + (1.0 + gt) * y, lg_ref[...], lb_ref[...])
    e1[0:halo, :] = e1[ts:rows, :]


def _pool_layer(x, m, pool_w, pool_scale, ln_g, ln_b, alpha, cfg):
    b, s, d = x.shape
    ts = cfg["pool_ts"]
    g, gw, _ = pool_w.shape
    assert POOL_WINDOWS == (2, 4, 8, 16) and g == len(POOL_WINDOWS) and s % ts == 0 and ts >= POOL_HALO
    row = lambda v: v.reshape(1, d)
    return pl.pallas_call(
        functools.partial(_pool_kernel, ts=ts, alpha=alpha),
        grid=(b, s // ts),
        in_specs=[pl.BlockSpec((1, ts, d), lambda i, j: (i, j, 0)),
                  pl.BlockSpec((1, N_MOD, d), lambda i, j: (i, 0, 0)),
                  pl.BlockSpec((g, gw, gw), lambda i, j: (0, 0, 0)),
                  pl.BlockSpec((1, d), lambda i, j: (0, 0)),
                  pl.BlockSpec((1, d), lambda i, j: (0, 0)),
                  pl.BlockSpec((1, d), lambda i, j: (0, 0))],
        out_specs=pl.BlockSpec((1, ts, d), lambda i, j: (i, j, 0)),
        out_shape=jax.ShapeDtypeStruct((b, s, d), F32),
        scratch_shapes=[pltpu.VMEM((POOL_HALO + ts, d), F32)] * 3,
        compiler_params=_cparams(2),
        name="pool_layer",
    )(x, m, pool_w.astype(BF16), row(pool_scale), row(ln_g), row(ln_b))


def _dn_in_kernel(x_ref, m_ref, wm_ref, ws_ref, cw_ref, av_ref, dv_ref,
                  q_ref, k_ref, v_ref, z_ref, bg_ref, ext, *, ts):
    s = pl.program_id(1)
    nh, dh = DN_HEADS, DN_HEAD_DIM
    w = nh * dh
    halo = CONV_HALO
    x = x_ref[0]
    sh, sc = m_ref[0, 0:1, :], m_ref[0, 1:2, :]
    h = (x * (1.0 + sc) + sh).astype(BF16)
    proj = _dot(h, wm_ref[...])

    @pl.when(s == 0)
    def _():
        ext[0:halo, :] = jnp.zeros((halo, 3 * w), F32)

    ext[halo:halo + ts, :] = proj[:, :3 * w]
    z_ref[0] = proj[:, 3 * w:]
    conv = cw_ref[0:1, :] * ext[halo - 3:halo - 3 + ts, :]
    for j in range(1, DN_CONV):
        conv = conv + cw_ref[j:j + 1, :] * ext[halo - 3 + j:halo - 3 + j + ts, :]
    act = conv * jax.nn.sigmoid(conv)
    ext[0:halo, :] = ext[ts:ts + halo, :]
    for hd in range(nh):
        qh = act[:, hd * dh:(hd + 1) * dh]
        kh = act[:, w + hd * dh:w + (hd + 1) * dh]
        q_ref[0, hd] = qh * lax.rsqrt(jnp.sum(qh * qh, axis=-1, keepdims=True) + RMS_EPS) * (dh ** -0.5)
        k_ref[0, hd] = kh * lax.rsqrt(jnp.sum(kh * kh, axis=-1, keepdims=True) + RMS_EPS)
        v_ref[0, hd] = act[:, 2 * w + hd * dh:2 * w + (hd + 1) * dh]
    small = _dot(h, ws_ref[...])
    lane = lax.broadcasted_iota(I32, small.shape, 1)
    beta = jax.nn.sigmoid(small)
    g = -jnp.exp(av_ref[...]) * jax.nn.softplus(small + dv_ref[...])
    bg_ref[0] = jnp.where(lane < nh, beta, jnp.where(lane < 2 * nh, g, 0.0))


def _dn_core_kernel(q_ref, k_ref, v_ref, bg_ref, o_ref,
                    st, bb, gcb, egb, ekb, gt, u_s, w_s, ai_s, qd_s, kd_s, *, ts):
    s = pl.program_id(1)
    nh, dh, c = DN_HEADS, DN_HEAD_DIM, DN_CHUNK
    nc = ts // c

    @pl.when(s == 0)
    def _():
        st[...] = jnp.zeros(st.shape, F32)

    bgv = bg_ref[0]
    lane = lax.broadcasted_iota(I32, bgv.shape, 1)
    rowc = lax.broadcasted_iota(I32, bgv.shape, 0) % c
    gc = jnp.where(lane >= nh, bgv, 0.0)
    shift = 1
    while shift < c:
        gc = gc + jnp.where(rowc >= shift, pltpu.roll(gc, shift, 0), 0.0)
        shift *= 2
    gc_t = gc.T
    eg = jnp.exp(gc)
    glast = jnp.concatenate(
        [jnp.broadcast_to(gc[(ci + 1) * c - 1:(ci + 1) * c, :], (c, LANES)) for ci in range(nc)], axis=0)
    ek = jnp.exp(glast - gc)
    for hd in range(nh):
        for ci in range(nc):
            gt[hd * nc + ci] = jnp.broadcast_to(gc_t[nh + hd:nh + hd + 1, ci * c:(ci + 1) * c], (SUBLANES, c))
        bb[hd] = jnp.broadcast_to(bgv[:, hd:hd + 1], (ts, LANES))
        gcb[hd] = jnp.broadcast_to(gc[:, nh + hd:nh + hd + 1], (ts, LANES))
        egb[hd] = jnp.broadcast_to(eg[:, nh + hd:nh + hd + 1], (ts, LANES))
        ekb[hd] = jnp.broadcast_to(ek[:, nh + hd:nh + hd + 1], (ts, LANES))

    ri = lax.broadcasted_iota(I32, (c, c), 0)
    ci_ = lax.broadcasted_iota(I32, (c, c), 1)
    tril = ri >= ci_
    eye = (ri == ci_).astype(F32)
    off_masks = []
    blk = 1
    while blk < c:
        off_masks.append((ri // blk != ci_ // blk) & (ri // (2 * blk) == ci_ // (2 * blk)) & (ri > ci_))
        blk *= 2

    hg = DN_HEAD_GROUP
    nb = hg * nc

    def group_body(gi, carry):
        heads = pl.ds(gi * hg, hg)
        chunks = lambda a: a.reshape(nb, c, a.shape[-1])
        per_head = lambda a: a.reshape(hg, ts, a.shape[-1])
        q, k, v = chunks(q_ref[0, heads]), chunks(k_ref[0, heads]), chunks(v_ref[0, heads])
        beta, egh, gch, ekh = chunks(bb[heads]), chunks(egb[heads]), chunks(gcb[heads]), chunks(ekb[heads])
        kb = k * beta
        vb = v * beta
        qd_s[heads] = per_head(q * egh)
        kd_s[heads] = per_head(k * ekh)
        kc = k.astype(BF16)
        grow = gt[pl.ds(gi * nb, nb), 0:1, :]
        dec = jnp.where(tril, jnp.exp(jnp.where(tril, gch - grow, 0.0)), 0.0)
        a_kk = _bdot_nt(kb.astype(BF16), kc)
        a_qk = _bdot_nt(q.astype(BF16), kc)
        ai_s[heads] = per_head(a_qk * dec)
        l_mat = a_kk * dec
        t_inv = eye - jnp.where(off_masks[0], l_mat, 0.0)
        for off in off_masks[1:]:
            tb = t_inv.astype(BF16)
            l_off = jnp.where(off, l_mat, 0.0).astype(BF16)
            t_inv = t_inv - _bdot(tb, _bdot(l_off, tb).astype(BF16))
        rhs = jnp.concatenate([vb, kb * egh], axis=2).astype(BF16)
        uw = _bdot(t_inv.astype(BF16), rhs)
        u_s[heads] = per_head(uw[:, :, :dh])
        w_s[heads] = per_head(uw[:, :, dh:])
        return carry

    lax.fori_loop(0, nh // hg, group_body, 0)

    def chunk_body(ci, carry):
        r0 = pl.multiple_of(ci * c, c)
        rows = pl.ds(r0, c)
        for hd in range(nh):
            state = st[hd]
            lhs = jnp.concatenate([w_s[hd, rows, :], qd_s[hd, rows, :]], axis=0).astype(BF16)
            ws_qs = _dot(lhs, state.astype(BF16))
            v_new = u_s[hd, rows, :] - ws_qs[:c]
            v_nb = v_new.astype(BF16)
            o_ref[0, hd, rows, :] = ws_qs[c:] + _dot(ai_s[hd, rows, :].astype(BF16), v_nb)
            g_last = egb[hd, pl.ds(r0 + c - 1, 1), :]
            st[hd] = state * g_last + _dot_tn(kd_s[hd, rows, :].astype(BF16), v_nb)
        return carry

    lax.fori_loop(0, nc, chunk_body, 0)


def _dn_out_kernel(o_ref, z_ref, x_ref, m_ref, nw_ref, wo_ref, lg_ref, lb_ref, out_ref, *, alpha):
    nh, dh = DN_HEADS, DN_HEAD_DIM
    z = z_ref[0]
    parts = []
    for hd in range(nh):
        oh = o_ref[0, hd]
        on = oh * lax.rsqrt(jnp.mean(oh * oh, axis=-1, keepdims=True) + RMS_EPS) * nw_ref[...]
        zh = z[:, hd * dh:(hd + 1) * dh]
        parts.append((on * (zh * jax.nn.sigmoid(zh))).astype(BF16))
    y = _dot(jnp.concatenate(parts, axis=1), wo_ref[...])
    x = x_ref[0]
    gt = m_ref[0, 2:3, :]
    out_ref[0] = _layer_norm(alpha * x + (1.0 + gt) * y, lg_ref[...], lb_ref[...])


def _deltanet_layer(x, m, w_in, conv_w, a_log, dt_bias, norm_w, w_out, ln_g, ln_b, alpha, cfg):
    b, s, d = x.shape
    nh, dh = DN_HEADS, DN_HEAD_DIM
    w = nh * dh
    assert w_in.shape == (d, 4 * w + 2 * nh) and conv_w.shape == (DN_CONV, 3 * w) and 2 * nh <= LANES
    w_main = w_in[:, :4 * w].astype(BF16)
    w_small = jnp.pad(w_in[:, 4 * w:], ((0, 0), (0, LANES - 2 * nh))).astype(BF16)
    avec = jnp.zeros((1, LANES), F32).at[0, nh:2 * nh].set(a_log.astype(F32))
    dvec = jnp.zeros((1, LANES), F32).at[0, nh:2 * nh].set(dt_bias.astype(F32))
    row = lambda v: v.reshape(1, -1)
    const2 = lambda i, j: (0, 0)

    ts = cfg["dn_in_ts"]
    assert s % ts == 0 and ts >= CONV_HALO
    head_major = jax.ShapeDtypeStruct((b, nh, s, dh), F32)
    hm_spec = lambda t: pl.BlockSpec((1, nh, t, dh), lambda i, j: (i, 0, j, 0))
    q, k, v, z, bg = pl.pallas_call(
        functools.partial(_dn_in_kernel, ts=ts),
        grid=(b, s // ts),
        in_specs=[pl.BlockSpec((1, ts, d), lambda i, j: (i, j, 0)),
                  pl.BlockSpec((1, N_MOD, d), lambda i, j: (i, 0, 0)),
                  pl.BlockSpec((d, 4 * w), const2),
                  pl.BlockSpec((d, LANES), const2),
                  pl.BlockSpec((DN_CONV, 3 * w), const2),
                  pl.BlockSpec((1, LANES), const2),
                  pl.BlockSpec((1, LANES), const2)],
        out_specs=[hm_spec(ts), hm_spec(ts), hm_spec(ts),
                   pl.BlockSpec((1, ts, w), lambda i, j: (i, j, 0)),
                   pl.BlockSpec((1, ts, LANES), lambda i, j: (i, j, 0))],
        out_shape=[head_major, head_major, head_major,
                   jax.ShapeDtypeStruct((b, s, w), F32),
                   jax.ShapeDtypeStruct((b, s, LANES), F32)],
        scratch_shapes=[pltpu.VMEM((CONV_HALO + ts, 3 * w), F32)],
        compiler_params=_cparams(2),
        name="deltanet_in",
    )(x, m, w_main, w_small, conv_w, avec, dvec)

    ts = cfg["dn_core_ts"]
    assert s % ts == 0 and ts % DN_CHUNK == 0 and dh == DN_CHUNK
    per_head = pltpu.VMEM((nh, ts, dh), F32)
    o = pl.pallas_call(
        functools.partial(_dn_core_kernel, ts=ts),
        grid=(b, s // ts),
        in_specs=[hm_spec(ts), hm_spec(ts), hm_spec(ts),
                  pl.BlockSpec((1, ts, LANES), lambda i, j: (i, j, 0))],
        out_specs=hm_spec(ts),
        out_shape=head_major,
        scratch_shapes=[pltpu.VMEM((nh, dh, dh), F32),
                        per_head, per_head, per_head, per_head,
                        pltpu.VMEM((nh * (ts // DN_CHUNK), SUBLANES, DN_CHUNK), F32),
                        per_head, per_head, per_head, per_head, per_head],
        compiler_params=_cparams(2),
        name="deltanet_core",
    )(q, k, v, bg)

    ts = cfg["dn_out_ts"]
    assert s % ts == 0
    return pl.pallas_call(
        functools.partial(_dn_out_kernel, alpha=alpha),
        grid=(b, s // ts),
        in_specs=[hm_spec(ts),
                  pl.BlockSpec((1, ts, w), lambda i, j: (i, j, 0)),
                  pl.BlockSpec((1, ts, d), lambda i, j: (i, j, 0)),
                  pl.BlockSpec((1, N_MOD, d), lambda i, j: (i, 0, 0)),
                  pl.BlockSpec((1, dh), const2),
                  pl.BlockSpec((w, d), const2),
                  pl.BlockSpec((1, d), const2),
                  pl.BlockSpec((1, d), const2)],
        out_specs=pl.BlockSpec((1, ts, d), lambda i, j: (i, j, 0)),
        out_shape=jax.ShapeDtypeStruct((b, s, d), F32),
        compiler_params=_cparams(2),
        name="deltanet_out",
    )(o, z, x, m, row(norm_w), w_out.astype(BF16), row(ln_g), row(ln_b))


def _router_kernel(x_ref, m_ref, rwt_ref, rb_ref, tri_ref,
                   hp_ref, idx_ref, p_ref, rank_ref, cnt_ref, run, *, tt):
    i = pl.program_id(0)

    @pl.when(i == 0)
    def _():
        run[...] = jnp.zeros(run.shape, F32)

    x = x_ref[...]
    d = x.shape[-1]
    sh, sc = m_ref[0, 3:4, :], m_ref[0, 4:5, :]
    h = x * (1.0 + sc) + sh
    bits = pltpu.bitcast(h.astype(BF16).astype(F32), U32)
    hp_ref[...] = (bits[:, :d // 2] >> 16) | (bits[:, d // 2:] & jnp.uint32(0xFFFF0000))

    logits = lax.dot_general(rwt_ref[...], h, (((1,), (1,)), ((), ())),
                             preferred_element_type=F32, precision=HIGHEST) + rb_ref[...]
    ne = logits.shape[0]
    eio = lax.broadcasted_iota(I32, logits.shape, 0).astype(F32)
    vals, idxs, sels = [], [], []
    for _ in range(TOP_K):
        mx = jnp.max(logits, axis=0, keepdims=True)
        ix = jnp.min(jnp.where(logits == mx, eio, float(ne)), axis=0, keepdims=True)
        sel = eio == ix
        logits = jnp.where(sel, -jnp.inf, logits)
        vals.append(mx)
        idxs.append(ix)
        sels.append(sel)
    exps = [jnp.exp(v - vals[0]) for v in vals]
    den = functools.reduce(lambda a, b_: a + b_, exps)
    chosen = functools.reduce(jnp.logical_or, sels)
    onehot = jnp.where(chosen, 1.0, 0.0)
    before = _dot(onehot.astype(BF16), tri_ref[...]) + run[...]
    ranks = [jnp.sum(jnp.where(sel, before, 0.0), axis=0, keepdims=True) for sel in sels]
    run[...] = run[...] + jnp.sum(onehot, axis=1, keepdims=True)
    idx_ref[...] = jnp.concatenate(idxs, axis=0).astype(I32)
    p_ref[...] = jnp.concatenate([e / den for e in exps], axis=0)
    rank_ref[...] = jnp.concatenate(ranks, axis=0).astype(I32)
    cnt_ref[...] = jnp.broadcast_to(run[...], cnt_ref.shape)


def _experts_kernel(be_ref, nu_ref, tok_ref, tok_next_ref, hp_ref, wgu_ref, bgu_ref, wd_ref, bd_ref, y_ref,
                    xbuf, sems, wgu_b, wd_b, *, bm):
    j = pl.program_id(0)
    n_used = nu_ref[0]

    def gather_rows(tok, slot):
        def start(r, carry):
            pltpu.make_async_copy(hp_ref.at[pl.ds(tok[0, 0, r], 1)], xbuf.at[slot, pl.ds(r, 1)],
                                  sems.at[slot]).start()
            return carry
        lax.fori_loop(0, bm, start, 0, unroll=8)

    @pl.when(j == 0)
    def _():
        gather_rows(tok_ref, 0)

    @pl.when(j + 1 < n_used)
    def _():
        gather_rows(tok_next_ref, (j + 1) % 2)

    @pl.when(j >= n_used)
    def _():
        y_ref[...] = jnp.zeros(y_ref.shape, F32)

    @pl.when(j < n_used)
    def _():
        slot = j % 2
        pltpu.make_async_copy(hp_ref.at[pl.ds(0, bm)], xbuf.at[slot], sems.at[slot]).wait()
        new_expert = jnp.logical_or(j == 0, be_ref[j] != be_ref[jnp.maximum(j - 1, 0)])

        @pl.when(new_expert)
        def _():
            wgu_b[...] = wgu_ref[0, 0].astype(BF16)
            wd_b[...] = wd_ref[0, 0].astype(BF16)

        bits = xbuf[slot]
        lo = pltpu.bitcast(bits << 16, F32)
        hi = pltpu.bitcast(bits & jnp.uint32(0xFFFF0000), F32)
        xb = jnp.concatenate([lo, hi], axis=1).astype(BF16)
        gu = _dot(xb, wgu_b[...]) + bgu_ref[0, 0]
        f = gu.shape[1] // 2
        glu = jnp.minimum(gu[:, :f], SWIGLU_LIMIT)
        lin = jnp.clip(gu[:, f:], -SWIGLU_LIMIT, SWIGLU_LIMIT)
        act = glu * jax.nn.sigmoid(SWIGLU_ALPHA * glu) * (lin + 1.0)
        y_ref[...] = _dot(act.astype(BF16), wd_b[...]) + bd_ref[0, 0]


def _combine_kernel(dest_ref, dest_next_ref, y_ref, x_ref, m_ref, p_ref, lg_ref, lb_ref, o_ref, ybuf, sems,
                    *, tt, alpha):
    i = pl.program_id(0)

    def gather_rows(dest, slot):
        def start(t, carry):
            for kk in range(TOP_K):
                pltpu.make_async_copy(y_ref.at[pl.ds(dest[kk, t], 1)], ybuf.at[slot, kk, pl.ds(t, 1)],
                                      sems.at[slot]).start()
            return carry
        lax.fori_loop(0, tt, start, 0, unroll=2)

    @pl.when(i == 0)
    def _():
        gather_rows(dest_ref, 0)

    @pl.when(i + 1 < pl.num_programs(0))
    def _():
        gather_rows(dest_next_ref, (i + 1) % 2)

    slot = i % 2
    for kk in range(TOP_K):
        pltpu.make_async_copy(y_ref.at[pl.ds(0, tt)], ybuf.at[slot, kk], sems.at[slot]).wait()
    p = p_ref[...]
    y = p[:, 0:1] * ybuf[slot, 0]
    for kk in range(1, TOP_K):
        y = y + p[:, kk:kk + 1] * ybuf[slot, kk]
    gt = m_ref[0, 5:6, :]
    o_ref[...] = _layer_norm(alpha * x_ref[...] + (1.0 + gt) * y, lg_ref[...], lb_ref[...])


def _moe_layer(x, m, layer, router_w, router_b, w_gu, b_gu, w_down, b_down, ln_g, ln_b, alpha, cfg):
    b, s, d = x.shape
    n_tok = b * s
    ne = router_w.shape[-1]
    f2 = w_gu.shape[-1]
    ff = f2 // 2
    assert d % (2 * LANES) == 0 and w_down.shape[-2:] == (ff, d)
    xf = x.reshape(n_tok, d)
    const2 = lambda i: (0, 0)

    tt = cfg["router_tt"]
    assert s % tt == 0
    tri = jnp.triu(jnp.ones((tt, tt), BF16), k=1)
    hp, idx, probs, rank, cnt = pl.pallas_call(
        functools.partial(_router_kernel, tt=tt),
        grid=(n_tok // tt,),
        in_specs=[pl.BlockSpec((tt, d), lambda i: (i, 0)),
                  pl.BlockSpec((1, N_MOD, d), lambda i: (i * tt // s, 0, 0)),
                  pl.BlockSpec((ne, d), const2),
                  pl.BlockSpec((ne, 1), const2),
                  pl.BlockSpec((tt, tt), const2)],
        out_specs=[pl.BlockSpec((tt, d // 2), lambda i: (i, 0)),
                   pl.BlockSpec((TOP_K, tt), lambda i: (0, i)),
                   pl.BlockSpec((TOP_K, tt), lambda i: (0, i)),
                   pl.BlockSpec((TOP_K, tt), lambda i: (0, i)),
                   pl.BlockSpec((ne, LANES), const2)],
        out_shape=[jax.ShapeDtypeStruct((n_tok, d // 2), U32),
                   jax.ShapeDtypeStruct((TOP_K, n_tok), I32),
                   jax.ShapeDtypeStruct((TOP_K, n_tok), F32),
                   jax.ShapeDtypeStruct((TOP_K, n_tok), I32),
                   jax.ShapeDtypeStruct((ne, LANES), F32)],
        scratch_shapes=[pltpu.VMEM((ne, 1), F32)],
        compiler_params=_cparams(1),
        name="moe_router",
    )(xf, m, router_w[layer].T, router_b[layer].reshape(ne, 1), tri)

    bm = cfg["expert_bm"]
    cap = n_tok * TOP_K + ne * bm
    nb = cap // bm
    counts = cnt[:, 0].astype(I32)
    padded = (counts + bm - 1) // bm * bm
    pad_end = jnp.cumsum(padded)
    pad_start = pad_end - padded
    eids = jnp.arange(ne, dtype=I32)[:, None, None]
    dest = rank + jnp.sum(jnp.where(idx[None] == eids, pad_start[:, None, None], 0), axis=0)
    block_start = jnp.arange(nb, dtype=I32) * bm
    block_expert = jnp.minimum(jnp.sum(block_start[None, :] >= pad_end[:, None], axis=0), ne - 1).astype(I32)
    n_used = (pad_end[-1:] // bm).astype(I32)

    token_ids = jnp.broadcast_to(jnp.arange(n_tok, dtype=I32)[None, :], (TOP_K, n_tok))
    row_token = jnp.zeros((cap,), I32).at[dest.reshape(-1)].set(token_ids.reshape(-1), unique_indices=True)
    row_token = row_token.reshape(nb, 1, bm)

    last = lambda j, be, nu: jnp.minimum(j, nu[0] - 1)
    nxt = lambda j, be, nu: jnp.minimum(j + 1, nu[0] - 1)
    y_rows = pl.pallas_call(
        functools.partial(_experts_kernel, bm=bm),
        grid_spec=pltpu.PrefetchScalarGridSpec(
            num_scalar_prefetch=2,
            grid=(nb,),
            in_specs=[pl.BlockSpec((1, 1, bm), lambda j, be, nu: (last(j, be, nu), 0, 0),
                                   memory_space=pltpu.SMEM),
                      pl.BlockSpec((1, 1, bm), lambda j, be, nu: (nxt(j, be, nu), 0, 0),
                                   memory_space=pltpu.SMEM),
                      pl.BlockSpec(memory_space=pl.ANY),
                      pl.BlockSpec((1, 1, d, f2), lambda j, be, nu: (layer, be[last(j, be, nu)], 0, 0)),
                      pl.BlockSpec((1, 1, 1, f2), lambda j, be, nu: (layer, be[last(j, be, nu)], 0, 0)),
                      pl.BlockSpec((1, 1, ff, d), lambda j, be, nu: (layer, be[last(j, be, nu)], 0, 0)),
                      pl.BlockSpec((1, 1, 1, d), lambda j, be, nu: (layer, be[last(j, be, nu)], 0, 0))],
            out_specs=pl.BlockSpec((bm, d), lambda j, be, nu: (j, 0)),
            scratch_shapes=[pltpu.VMEM((2, bm, d // 2), U32), pltpu.SemaphoreType.DMA((2,)),
                            pltpu.VMEM((d, f2), BF16), pltpu.VMEM((ff, d), BF16)]),
        out_shape=jax.ShapeDtypeStruct((cap, d), F32),
        compiler_params=_cparams(1),
        name="moe_experts",
    )(block_expert, n_used, row_token, row_token, hp, w_gu, b_gu.reshape(*b_gu.shape[:2], 1, f2),
      w_down, b_down.reshape(*b_down.shape[:2], 1, d))

    tt = cfg["combine_tt"]
    assert s % tt == 0
    n_tiles = n_tok // tt
    out = pl.pallas_call(
        functools.partial(_combine_kernel, tt=tt, alpha=alpha),
        grid=(n_tiles,),
        in_specs=[pl.BlockSpec((TOP_K, tt), lambda i: (0, i), memory_space=pltpu.SMEM),
                  pl.BlockSpec((TOP_K, tt), lambda i: (0, jnp.minimum(i + 1, n_tiles - 1)),
                               memory_space=pltpu.SMEM),
                  pl.BlockSpec(memory_space=pl.ANY),
                  pl.BlockSpec((tt, d), lambda i: (i, 0)),
                  pl.BlockSpec((1, N_MOD, d), lambda i: (i * tt // s, 0, 0)),
                  pl.BlockSpec((tt, TOP_K), lambda i: (i, 0)),
                  pl.BlockSpec((1, d), const2),
                  pl.BlockSpec((1, d), const2)],
        out_specs=pl.BlockSpec((tt, d), lambda i: (i, 0)),
        out_shape=jax.ShapeDtypeStruct((n_tok, d), F32),
        scratch_shapes=[pltpu.VMEM((2, TOP_K, tt, d), F32), pltpu.SemaphoreType.DMA((2,))],
        compiler_params=_cparams(1),
        name="moe_combine",
    )(dest, dest, y_rows, xf, m, probs.T, ln_g.reshape(1, d), ln_b.reshape(1, d))
    return out.reshape(b, s, d)


def kernel(x, c, ada_w, ada_b, ln_g, ln_b, pool_w, pool_scale, dn_w_in, dn_conv_w, dn_a_log, dn_dt_bias,
           dn_norm_w, dn_w_out, router_w, router_b, exp_w_gu, exp_b_gu, exp_w_down, exp_b_down):
    b, s, d = x.shape
    depth = ada_w.shape[0]
    alpha = (2 * depth) ** 0.25
    cfg = _tile_config(s, b * s)
    mods = _mods(c, ada_w, ada_b)
    n_mixers = 2
    for i in range(depth):
        m = mods[i]
        j = i // n_mixers
        if i % n_mixers == 0:
            x = _pool_layer(x, m, pool_w[j], pool_scale[j], ln_g[i, 0], ln_b[i, 0], alpha, cfg)
        else:
            x = _deltanet_layer(x, m, dn_w_in[j], dn_conv_w[j], dn_a_log[j], dn_dt_bias[j], dn_norm_w[j],
                                dn_w_out[j], ln_g[i, 0], ln_b[i, 0], alpha, cfg)
        x = _moe_layer(x, m, i, router_w, router_b, exp_w_gu, exp_b_gu, exp_w_down, exp_b_down,
                       ln_g[i, 1], ln_b[i, 1], alpha, cfg)
    return x
```

```python
import functools

import jax
import jax.numpy as jnp
from jax import lax
from jax.experimental import pallas as pl
from jax.experimental.pallas import tpu as pltpu
from jax.experimental.pallas import tpu_sc as plsc

F32 = jnp.float32
BF16 = jnp.bfloat16
I32 = jnp.int32
U32 = jnp.uint32

N_MOD = 6
POOL_WINDOWS = (2, 4, 8, 16)
DN_HEADS = 8
DN_HEAD_DIM = 128
DN_CONV = 4
TOP_K = 4
SWIGLU_LIMIT = 7.0
SWIGLU_ALPHA = 1.702
LN_EPS = 1e-5
RMS_EPS = 1e-6

LANES = 128
SUBLANES = 8
VMEM_LIMIT_BYTES = 56 * 1024 * 1024
SC_WORKERS = 32
SC_WINDOW = 128

DN_CHUNK = LANES
DN_HEAD_GROUP = 2
POOL_HALO = 32
CONV_HALO = SUBLANES

HIGHEST = lax.Precision.HIGHEST


def _tile_config(seq, n_tok):
    return dict(
        pool_ts=min(512, seq),
        dn_in_ts=min(256, seq),
        dn_core_ts=min(512, seq),
        dn_out_ts=min(512, seq),
        router_tt=min(512, seq),
        expert_bm=512,
        combine_tt=min(256, seq),
    )


def _cparams(n_axes):
    return pltpu.CompilerParams(dimension_semantics=("arbitrary",) * n_axes,
                                vmem_limit_bytes=VMEM_LIMIT_BYTES)


def _layer_norm(v, g, b):
    mu = jnp.mean(v, axis=-1, keepdims=True)
    d = v - mu
    var = jnp.mean(d * d, axis=-1, keepdims=True)
    return d * lax.rsqrt(var + LN_EPS) * g + b


def _dot(a, b):
    return jnp.dot(a, b, preferred_element_type=F32)


def _dot_nt(a, b):
    return lax.dot_general(a, b, (((1,), (1,)), ((), ())), preferred_element_type=F32)


def _dot_tn(a, b):
    return lax.dot_general(a, b, (((0,), (0,)), ((), ())), preferred_element_type=F32)


def _bdot(a, b):
    return lax.dot_general(a, b, (((2,), (1,)), ((0,), (0,))), preferred_element_type=F32)


def _bdot_nt(a, b):
    return lax.dot_general(a, b, (((2,), (2,)), ((0,), (0,))), preferred_element_type=F32)


def _mods_kernel(c_ref, w_ref, b_ref, o_ref):
    c = c_ref[...]
    c_act = c * jax.nn.sigmoid(c)
    o_ref[0] = jnp.dot(c_act, w_ref[0], preferred_element_type=F32, precision=HIGHEST) + b_ref[0]


def _mods(c, ada_w, ada_b):
    depth, d, n = ada_w.shape
    b = c.shape[0]
    tn = 2048 if n % 2048 == 0 else n
    out = pl.pallas_call(
        _mods_kernel,
        grid=(depth, n // tn),
        in_specs=[pl.BlockSpec((b, d), lambda i, j: (0, 0)),
                  pl.BlockSpec((1, d, tn), lambda i, j: (i, 0, j)),
                  pl.BlockSpec((1, 1, tn), lambda i, j: (i, 0, j))],
        out_specs=pl.BlockSpec((1, b, tn), lambda i, j: (i, 0, j)),
        out_shape=jax.ShapeDtypeStruct((depth, b, n), F32),
        compiler_params=_cparams(2),
        name="adaln_mods",
    )(c, ada_w, ada_b.reshape(depth, 1, n))
    return out.reshape(depth, b, N_MOD, d)


def _pool_kernel(x_ref, m_ref, pw_ref, ps_ref, lg_ref, lb_ref, o_ref, e1, ea, eb, *, ts, alpha):
    s = pl.program_id(1)
    d = x_ref.shape[-1]
    gw = d // len(POOL_WINDOWS)
    halo = POOL_HALO
    rows = halo + ts
    x = x_ref[0]
    sh, sc, gt = m_ref[0, 0:1, :], m_ref[0, 1:2, :], m_ref[0, 2:3, :]
    h = x * (1.0 + sc) + sh

    @pl.when(s == 0)
    def _():
        e1[0:halo, :] = jnp.zeros((halo, d), F32)

    e1[halo:rows, :] = h
    ea[8:rows, :] = e1[8:rows, :] + e1[7:rows - 1, :]
    eb[16:rows, gw:] = ea[16:rows, gw:] + ea[14:rows - 2, gw:]
    ea[24:rows, 2 * gw:] = eb[24:rows, 2 * gw:] + eb[20:rows - 4, 2 * gw:]
    eb[32:rows, 3 * gw:] = ea[32:rows, 3 * gw:] + ea[24:rows - 8, 3 * gw:]

    pos = s * ts + lax.broadcasted_iota(I32, (ts, 1), 0)
    outs = []
    for g, win in enumerate(POOL_WINDOWS):
        src = ea if g % 2 == 0 else eb
        cols = slice(g * gw, (g + 1) * gw)
        cnt = jnp.minimum(pos + 1, win).astype(F32)
        pooled = src[halo:rows, cols] / cnt - h[:, cols]
        outs.append(_dot(pooled.astype(BF16), pw_ref[g]))
    y = jnp.concatenate(outs, axis=1) * ps_ref[...]
    o_ref[0] = _layer_norm(alpha * x + (1.0 + gt) * y, lg_ref[...], lb_ref[...])
    e1[0:halo, :] = e1[ts:rows, :]


def _pool_layer(x, m, pool_w, pool_scale, ln_g, ln_b, alpha, cfg):
    b, s, d = x.shape
    ts = cfg["pool_ts"]
    g, gw, _ = pool_w.shape
    assert POOL_WINDOWS == (2, 4, 8, 16) and g == len(POOL_WINDOWS) and s % ts == 0 and ts >= POOL_HALO
    row = lambda v: v.reshape(1, d)
    return pl.pallas_call(
        functools.partial(_pool_kernel, ts=ts, alpha=alpha),
        grid=(b, s // ts),
        in_specs=[pl.BlockSpec((1, ts, d), lambda i, j: (i, j, 0)),
                  pl.BlockSpec((1, N_MOD, d), lambda i, j: (i, 0, 0)),
                  pl.BlockSpec((g, gw, gw), lambda i, j: (0, 0, 0)),
                  pl.BlockSpec((1, d), lambda i, j: (0, 0)),
                  pl.BlockSpec((1, d), lambda i, j: (0, 0)),
                  pl.BlockSpec((1, d), lambda i, j: (0, 0))],
        out_specs=pl.BlockSpec((1, ts, d), lambda i, j: (i, j, 0)),
        out_shape=jax.ShapeDtypeStruct((b, s, d), F32),
        scratch_shapes=[pltpu.VMEM((POOL_HALO + ts, d), F32)] * 3,
        compiler_params=_cparams(2),
        name="pool_layer",
    )(x, m, pool_w.astype(BF16), row(pool_scale), row(ln_g), row(ln_b))


def _dn_in_kernel(x_ref, m_ref, wm_ref, ws_ref, cw_ref, av_ref, dv_ref,
                  q_ref, k_ref, v_ref, z_ref, bg_ref, ext, *, ts):
    s = pl.program_id(1)
    nh, dh = DN_HEADS, DN_HEAD_DIM
    w = nh * dh
    halo = CONV_HALO
    x = x_ref[0]
    sh, sc = m_ref[0, 0:1, :], m_ref[0, 1:2, :]
    h = (x * (1.0 + sc) + sh).astype(BF16)
    proj = _dot(h, wm_ref[...])

    @pl.when(s == 0)
    def _():
        ext[0:halo, :] = jnp.zeros((halo, 3 * w), F32)

    ext[halo:halo + ts, :] = proj[:, :3 * w]
    z_ref[0] = proj[:, 3 * w:]
    conv = cw_ref[0:1, :] * ext[halo - 3:halo - 3 + ts, :]
    for j in range(1, DN_CONV):
        conv = conv + cw_ref[j:j + 1, :] * ext[halo - 3 + j:halo - 3 + j + ts, :]
    act = conv * jax.nn.sigmoid(conv)
    ext[0:halo, :] = ext[ts:ts + halo, :]
    for hd in range(nh):
        qh = act[:, hd * dh:(hd + 1) * dh]
        kh = act[:, w + hd * dh:w + (hd + 1) * dh]
        q_ref[0, hd] = qh * lax.rsqrt(jnp.sum(qh * qh, axis=-1, keepdims=True) + RMS_EPS) * (dh ** -0.5)
        k_ref[0, hd] = kh * lax.rsqrt(jnp.sum(kh * kh, axis=-1, keepdims=True) + RMS_EPS)
        v_ref[0, hd] = act[:, 2 * w + hd * dh:2 * w + (hd + 1) * dh]
    small = _dot(h, ws_ref[...])
    lane = lax.broadcasted_iota(I32, small.shape, 1)
    beta = jax.nn.sigmoid(small)
    g = -jnp.exp(av_ref[...]) * jax.nn.softplus(small + dv_ref[...])
    bg_ref[0] = jnp.where(lane < nh, beta, jnp.where(lane < 2 * nh, g, 0.0))


def _dn_core_kernel(q_ref, k_ref, v_ref, bg_ref, o_ref,
                    st, bb, gcb, egb, ekb, gt, u_s, w_s, ai_s, qd_s, kd_s, *, ts):
    s = pl.program_id(1)
    nh, dh, c = DN_HEADS, DN_HEAD_DIM, DN_CHUNK
    nc = ts // c

    @pl.when(s == 0)
    def _():
        st[...] = jnp.zeros(st.shape, F32)

    bgv = bg_ref[0]
    lane = lax.broadcasted_iota(I32, bgv.shape, 1)
    rowc = lax.broadcasted_iota(I32, bgv.shape, 0) % c
    gc = jnp.where(lane >= nh, bgv, 0.0)
    shift = 1
    while shift < c:
        gc = gc + jnp.where(rowc >= shift, pltpu.roll(gc, shift, 0), 0.0)
        shift *= 2
    gc_t = gc.T
    eg = jnp.exp(gc)
    glast = jnp.concatenate(
        [jnp.broadcast_to(gc[(ci + 1) * c - 1:(ci + 1) * c, :], (c, LANES)) for ci in range(nc)], axis=0)
    ek = jnp.exp(glast - gc)
    for hd in range(nh):
        for ci in range(nc):
            gt[hd * nc + ci] = jnp.broadcast_to(gc_t[nh + hd:nh + hd + 1, ci * c:(ci + 1) * c], (SUBLANES, c))
        bb[hd] = jnp.broadcast_to(bgv[:, hd:hd + 1], (ts, LANES))
        gcb[hd] = jnp.broadcast_to(gc[:, nh + hd:nh + hd + 1], (ts, LANES))
        egb[hd] = jnp.broadcast_to(eg[:, nh + hd:nh + hd + 1], (ts, LANES))
        ekb[hd] = jnp.broadcast_to(ek[:, nh + hd:nh + hd + 1], (ts, LANES))

    ri = lax.broadcasted_iota(I32, (c, c), 0)
    ci_ = lax.broadcasted_iota(I32, (c, c), 1)
    tril = ri >= ci_
    eye = (ri == ci_).astype(F32)
    off_masks = []
    blk = 1
    while blk < c:
        off_masks.append((ri // blk != ci_ // blk) & (ri // (2 * blk) == ci_ // (2 * blk)) & (ri > ci_))
        blk *= 2

    hg = DN_HEAD_GROUP
    nb = hg * nc

    def group_body(gi, carry):
        heads = pl.ds(gi * hg, hg)
        chunks = lambda a: a.reshape(nb, c, a.shape[-1])
        per_head = lambda a: a.reshape(hg, ts, a.shape[-1])
        q, k, v = chunks(q_ref[0, heads]), chunks(k_ref[0, heads]), chunks(v_ref[0, heads])
        beta, egh, gch, ekh = chunks(bb[heads]), chunks(egb[heads]), chunks(gcb[heads]), chunks(ekb[heads])
        kb = k * beta
        vb = v * beta
        qd_s[heads] = per_head(q * egh)
        kd_s[heads] = per_head(k * ekh)
        kc = k.astype(BF16)
        grow = gt[pl.ds(gi * nb, nb), 0:1, :]
        dec = jnp.where(tril, jnp.exp(jnp.where(tril, gch - grow, 0.0)), 0.0)
        a_kk = _bdot_nt(kb.astype(BF16), kc)
        a_qk = _bdot_nt(q.astype(BF16), kc)
        ai_s[heads] = per_head(a_qk * dec)
        l_mat = a_kk * dec
        t_inv = eye - jnp.where(off_masks[0], l_mat, 0.0)
        for off in off_masks[1:]:
            tb = t_inv.astype(BF16)
            l_off = jnp.where(off, l_mat, 0.0).astype(BF16)
            t_inv = t_inv - _bdot(tb, _bdot(l_off, tb).astype(BF16))
        rhs = jnp.concatenate([vb, kb * egh], axis=2).astype(BF16)
        uw = _bdot(t_inv.astype(BF16), rhs)
        u_s[heads] = per_head(uw[:, :, :dh])
        w_s[heads] = per_head(uw[:, :, dh:])
        return carry

    lax.fori_loop(0, nh // hg, group_body, 0)

    def chunk_body(ci, carry):
        r0 = pl.multiple_of(ci * c, c)
        rows = pl.ds(r0, c)
        for hd in range(nh):
            state = st[hd]
            lhs = jnp.concatenate([w_s[hd, rows, :], qd_s[hd, rows, :]], axis=0).astype(BF16)
            ws_qs = _dot(lhs, state.astype(BF16))
            v_new = u_s[hd, rows, :] - ws_qs[:c]
            v_nb = v_new.astype(BF16)
            o_ref[0, hd, rows, :] = ws_qs[c:] + _dot(ai_s[hd, rows, :].astype(BF16), v_nb)
            g_last = egb[hd, pl.ds(r0 + c - 1, 1), :]
            st[hd] = state * g_last + _dot_tn(kd_s[hd, rows, :].astype(BF16), v_nb)
        return carry

    lax.fori_loop(0, nc, chunk_body, 0)


def _dn_out_kernel(o_ref, z_ref, x_ref, m_ref, nw_ref, wo_ref, lg_ref, lb_ref, out_ref, *, alpha):
    nh, dh = DN_HEADS, DN_HEAD_DIM
    z = z_ref[0]
    parts = []
    for hd in range(nh):
        oh = o_ref[0, hd]
        on = oh * lax.rsqrt(jnp.mean(oh * oh, axis=-1, keepdims=True) + RMS_EPS) * nw_ref[...]
        zh = z[:, hd * dh:(hd + 1) * dh]
        parts.append((on * (zh * jax.nn.sigmoid(zh))).astype(BF16))
    y = _dot(jnp.concatenate(parts, axis=1), wo_ref[...])
    x = x_ref[0]
    gt = m_ref[0, 2:3, :]
    out_ref[0] = _layer_norm(alpha * x + (1.0 + gt) * y, lg_ref[...], lb_ref[...])


def _deltanet_layer(x, m, w_in, conv_w, a_log, dt_bias, norm_w, w_out, ln_g, ln_b, alpha, cfg):
    b, s, d = x.shape
    nh, dh = DN_HEADS, DN_HEAD_DIM
    w = nh * dh
    assert w_in.shape == (d, 4 * w + 2 * nh) and conv_w.shape == (DN_CONV, 3 * w) and 2 * nh <= LANES
    w_main = w_in[:, :4 * w].astype(BF16)
    w_small = jnp.pad(w_in[:, 4 * w:], ((0, 0), (0, LANES - 2 * nh))).astype(BF16)
    avec = jnp.zeros((1, LANES), F32).at[0, nh:2 * nh].set(a_log.astype(F32))
    dvec = jnp.zeros((1, LANES), F32).at[0, nh:2 * nh].set(dt_bias.astype(F32))
    row = lambda v: v.reshape(1, -1)
    const2 = lambda i, j: (0, 0)

    ts = cfg["dn_in_ts"]
    assert s % ts == 0 and ts >= CONV_HALO
    head_major = jax.ShapeDtypeStruct((b, nh, s, dh), F32)
    hm_spec = lambda t: pl.BlockSpec((1, nh, t, dh), lambda i, j: (i, 0, j, 0))
    q, k, v, z, bg = pl.pallas_call(
        functools.partial(_dn_in_kernel, ts=ts),
        grid=(b, s // ts),
        in_specs=[pl.BlockSpec((1, ts, d), lambda i, j: (i, j, 0)),
                  pl.BlockSpec((1, N_MOD, d), lambda i, j: (i, 0, 0)),
                  pl.BlockSpec((d, 4 * w), const2),
                  pl.BlockSpec((d, LANES), const2),
                  pl.BlockSpec((DN_CONV, 3 * w), const2),
                  pl.BlockSpec((1, LANES), const2),
                  pl.BlockSpec((1, LANES), const2)],
        out_specs=[hm_spec(ts), hm_spec(ts), hm_spec(ts),
                   pl.BlockSpec((1, ts, w), lambda i, j: (i, j, 0)),
                   pl.BlockSpec((1, ts, LANES), lambda i, j: (i, j, 0))],
        out_shape=[head_major, head_major, head_major,
                   jax.ShapeDtypeStruct((b, s, w), F32),
                   jax.ShapeDtypeStruct((b, s, LANES), F32)],
        scratch_shapes=[pltpu.VMEM((CONV_HALO + ts, 3 * w), F32)],
        compiler_params=_cparams(2),
        name="deltanet_in",
    )(x, m, w_main, w_small, conv_w, avec, dvec)

    ts = cfg["dn_core_ts"]
    assert s % ts == 0 and ts % DN_CHUNK == 0 and dh == DN_CHUNK
    per_head = pltpu.VMEM((nh, ts, dh), F32)
    o = pl.pallas_call(
        functools.partial(_dn_core_kernel, ts=ts),
        grid=(b, s // ts),
        in_specs=[hm_spec(ts), hm_spec(ts), hm_spec(ts),
                  pl.BlockSpec((1, ts, LANES), lambda i, j: (i, j, 0))],
        out_specs=hm_spec(ts),
        out_shape=head_major,
        scratch_shapes=[pltpu.VMEM((nh, dh, dh), F32),
                        per_head, per_head, per_head, per_head,
                        pltpu.VMEM((nh * (ts // DN_CHUNK), SUBLANES, DN_CHUNK), F32),
                        per_head, per_head, per_head, per_head, per_head],
        compiler_params=_cparams(2),
        name="deltanet_core",
    )(q, k, v, bg)

    ts = cfg["dn_out_ts"]
    assert s % ts == 0
    return pl.pallas_call(
        functools.partial(_dn_out_kernel, alpha=alpha),
        grid=(b, s // ts),
        in_specs=[hm_spec(ts),
                  pl.BlockSpec((1, ts, w), lambda i, j: (i, j, 0)),
                  pl.BlockSpec((1, ts, d), lambda i, j: (i, j, 0)),
                  pl.BlockSpec((1, N_MOD, d), lambda i, j: (i, 0, 0)),
                  pl.BlockSpec((1, dh), const2),
                  pl.BlockSpec((w, d), const2),
                  pl.BlockSpec((1, d), const2),
                  pl.BlockSpec((1, d), const2)],
        out_specs=pl.BlockSpec((1, ts, d), lambda i, j: (i, j, 0)),
        out_shape=jax.ShapeDtypeStruct((b, s, d), F32),
        compiler_params=_cparams(2),
        name="deltanet_out",
    )(o, z, x, m, row(norm_w), w_out.astype(BF16), row(ln_g), row(ln_b))


def _pack_row_chunks(v):
    half = v.shape[-1] // 2
    bits = pltpu.bitcast(v.astype(BF16).astype(F32), U32)
    packed = (bits[:, :half] >> 16) | (bits[:, half:] & jnp.uint32(0xFFFF0000))
    return [packed[:, i * LANES:(i + 1) * LANES] for i in range(half // LANES)]


def _unpack_row_chunks(chunks):
    lo = [pltpu.bitcast(c << 16, F32) for c in chunks]
    hi = [pltpu.bitcast(c & jnp.uint32(0xFFFF0000), F32) for c in chunks]
    return jnp.concatenate(lo + hi, axis=1)


def _store_row_chunks(ref, lead, chunks):
    n = chunks[0].shape[0]
    for i, ch in enumerate(chunks):
        ref[(*lead, pl.ds(i, n, stride=len(chunks)), slice(None))] = ch


def _load_row_chunks(ref, lead, n, n_chunks):
    return [ref[(*lead, pl.ds(i, n, stride=n_chunks), slice(None))] for i in range(n_chunks)]


def _sc_mesh():
    return plsc.VectorSubcoreMesh(core_axis_name="c", subcore_axis_name="s")


def _sc_gather_rows(table, idx):
    n = idx.shape[0]
    assert table.shape[1] == LANES and n % (SC_WINDOW * SC_WORKERS) == 0

    @functools.partial(pl.kernel, out_type=jax.ShapeDtypeStruct((n, LANES), table.dtype), mesh=_sc_mesh(),
                       name="sc_gather_rows")
    def gather(table_hbm, idx_hbm, out_hbm):
        def body(idx_vmem, out_vmem):
            pltpu.sync_copy(table_hbm.at[idx_vmem.at[0]], out_vmem)

        pltpu.emit_pipeline(
            body,
            grid=(n // SC_WINDOW,),
            in_specs=[pl.BlockSpec((1, SC_WINDOW), lambda i: (0, i))],
            out_specs=[pl.BlockSpec((SC_WINDOW, LANES), lambda i: (i, 0))],
            core_axis_name=("c", "s"),
            dimension_semantics=(pltpu.PARALLEL,),
        )(idx_hbm, out_hbm)

    return gather(table, idx.reshape(1, n))


def _sc_scatter_rows(x, idx_list, n_out):
    n = x.shape[0]
    assert x.shape[1] == LANES and n % (SC_WINDOW * SC_WORKERS) == 0

    @functools.partial(pl.kernel, out_type=jax.ShapeDtypeStruct((n_out, LANES), x.dtype), mesh=_sc_mesh(),
                       name="sc_scatter_rows")
    def scatter(x_hbm, *refs):
        idx_hbms, out_hbm = refs[:-1], refs[-1]

        def body(x_vmem, *idx_vmems):
            for idx_vmem in idx_vmems:
                pltpu.sync_copy(x_vmem, out_hbm.at[idx_vmem.at[0]])

        pltpu.emit_pipeline(
            body,
            grid=(n // SC_WINDOW,),
            in_specs=[pl.BlockSpec((SC_WINDOW, LANES), lambda i: (i, 0))]
                     + [pl.BlockSpec((1, SC_WINDOW), lambda i: (0, i))] * len(idx_list),
            out_specs=[],
            core_axis_name=("c", "s"),
            dimension_semantics=(pltpu.PARALLEL,),
        )(x_hbm, *idx_hbms)

    return scatter(x, *[idx.reshape(1, n) for idx in idx_list])


def _router_kernel(x_ref, m_ref, rwt_ref, rb_ref, tri_ref,
                   hp_ref, idx_ref, p_ref, rank_ref, cnt_ref, run, *, tt):
    i = pl.program_id(0)

    @pl.when(i == 0)
    def _():
        run[...] = jnp.zeros(run.shape, F32)

    x = x_ref[...]
    d = x.shape[-1]
    sh, sc = m_ref[0, 3:4, :], m_ref[0, 4:5, :]
    h = x * (1.0 + sc) + sh
    _store_row_chunks(hp_ref, (), _pack_row_chunks(h))

    logits = lax.dot_general(rwt_ref[...], h, (((1,), (1,)), ((), ())),
                             preferred_element_type=F32, precision=HIGHEST) + rb_ref[...]
    ne = logits.shape[0]
    eio = lax.broadcasted_iota(I32, logits.shape, 0).astype(F32)
    vals, idxs, sels = [], [], []
    for _ in range(TOP_K):
        mx = jnp.max(logits, axis=0, keepdims=True)
        ix = jnp.min(jnp.where(logits == mx, eio, float(ne)), axis=0, keepdims=True)
        sel = eio == ix
        logits = jnp.where(sel, -jnp.inf, logits)
        vals.append(mx)
        idxs.append(ix)
        sels.append(sel)
    exps = [jnp.exp(v - vals[0]) for v in vals]
    den = functools.reduce(lambda a, b_: a + b_, exps)
    chosen = functools.reduce(jnp.logical_or, sels)
    onehot = jnp.where(chosen, 1.0, 0.0)
    before = _dot(onehot.astype(BF16), tri_ref[...]) + run[...]
    ranks = [jnp.sum(jnp.where(sel, before, 0.0), axis=0, keepdims=True) for sel in sels]
    run[...] = run[...] + jnp.sum(onehot, axis=1, keepdims=True)
    idx_ref[...] = jnp.concatenate(idxs, axis=0).astype(I32)
    p_ref[...] = jnp.concatenate([e / den for e in exps], axis=0)
    rank_ref[...] = jnp.concatenate(ranks, axis=0).astype(I32)
    cnt_ref[...] = jnp.broadcast_to(run[...], cnt_ref.shape)


def _experts_kernel(be_ref, nu_ref, xs_ref, wgu_ref, bgu_ref, wd_ref, bd_ref, y_ref, wgu_b, wd_b, *, bm):
    j = pl.program_id(0)
    n_chunks = xs_ref.shape[0] // bm

    @pl.when(j >= nu_ref[0])
    def _():
        y_ref[...] = jnp.zeros(y_ref.shape, U32)

    @pl.when(j < nu_ref[0])
    def _():
        new_expert = jnp.logical_or(j == 0, be_ref[j] != be_ref[jnp.maximum(j - 1, 0)])

        @pl.when(new_expert)
        def _():
            wgu_b[...] = wgu_ref[0, 0].astype(BF16)
            wd_b[...] = wd_ref[0, 0].astype(BF16)

        xb = _unpack_row_chunks(_load_row_chunks(xs_ref, (), bm, n_chunks)).astype(BF16)
        gu = _dot(xb, wgu_b[...]) + bgu_ref[0, 0]
        f = gu.shape[1] // 2
        glu = jnp.minimum(gu[:, :f], SWIGLU_LIMIT)
        lin = jnp.clip(gu[:, f:], -SWIGLU_LIMIT, SWIGLU_LIMIT)
        act = glu * jax.nn.sigmoid(SWIGLU_ALPHA * glu) * (lin + 1.0)
        y = _dot(act.astype(BF16), wd_b[...]) + bd_ref[0, 0]
        _store_row_chunks(y_ref, (), _pack_row_chunks(y))


def _combine_kernel(ya_ref, x_ref, m_ref, p_ref, lg_ref, lb_ref, o_ref, *, tt, alpha):
    n_chunks = ya_ref.shape[1] // tt
    p = p_ref[...]
    y = None
    for kk in range(TOP_K):
        yk = p[:, kk:kk + 1] * _unpack_row_chunks(_load_row_chunks(ya_ref, (kk,), tt, n_chunks))
        y = yk if y is None else y + yk
    gt = m_ref[0, 5:6, :]
    o_ref[...] = _layer_norm(alpha * x_ref[...] + (1.0 + gt) * y, lg_ref[...], lb_ref[...])


def _moe_layer(x, m, layer, router_w, router_b, w_gu, b_gu, w_down, b_down, ln_g, ln_b, alpha, cfg):
    b, s, d = x.shape
    n_tok = b * s
    ne = router_w.shape[-1]
    f2 = w_gu.shape[-1]
    ff = f2 // 2
    assert d % (2 * LANES) == 0 and w_down.shape[-2:] == (ff, d)
    rc = d // (2 * LANES)
    xf = x.reshape(n_tok, d)
    const2 = lambda i: (0, 0)

    tt = cfg["router_tt"]
    assert s % tt == 0
    tri = jnp.triu(jnp.ones((tt, tt), BF16), k=1)
    hp, idx, probs, rank, cnt = pl.pallas_call(
        functools.partial(_router_kernel, tt=tt),
        grid=(n_tok // tt,),
        in_specs=[pl.BlockSpec((tt, d), lambda i: (i, 0)),
                  pl.BlockSpec((1, N_MOD, d), lambda i: (i * tt // s, 0, 0)),
                  pl.BlockSpec((ne, d), const2),
                  pl.BlockSpec((ne, 1), const2),
                  pl.BlockSpec((tt, tt), const2)],
        out_specs=[pl.BlockSpec((rc * tt, LANES), lambda i: (i, 0)),
                   pl.BlockSpec((TOP_K, tt), lambda i: (0, i)),
                   pl.BlockSpec((TOP_K, tt), lambda i: (0, i)),
                   pl.BlockSpec((TOP_K, tt), lambda i: (0, i)),
                   pl.BlockSpec((ne, LANES), const2)],
        out_shape=[jax.ShapeDtypeStruct((rc * n_tok, LANES), U32),
                   jax.ShapeDtypeStruct((TOP_K, n_tok), I32),
                   jax.ShapeDtypeStruct((TOP_K, n_tok), F32),
                   jax.ShapeDtypeStruct((TOP_K, n_tok), I32),
                   jax.ShapeDtypeStruct((ne, LANES), F32)],
        scratch_shapes=[pltpu.VMEM((ne, 1), F32)],
        compiler_params=_cparams(1),
        name="moe_router",
    )(xf, m, router_w[layer].T, router_b[layer].reshape(ne, 1), tri)

    bm = cfg["expert_bm"]
    cap = n_tok * TOP_K + ne * bm
    nb = cap // bm
    counts = cnt[:, 0].astype(I32)
    padded = (counts + bm - 1) // bm * bm
    pad_end = jnp.cumsum(padded)
    pad_start = pad_end - padded
    eids = jnp.arange(ne, dtype=I32)[:, None, None]
    dest = rank + jnp.sum(jnp.where(idx[None] == eids, pad_start[:, None, None], 0), axis=0)
    block_start = jnp.arange(nb, dtype=I32) * bm
    block_expert = jnp.minimum(jnp.sum(block_start[None, :] >= pad_end[:, None], axis=0), ne - 1).astype(I32)
    n_used = (pad_end[-1:] // bm).astype(I32)

    dest_rows = (dest[:, :, None] * rc + jnp.arange(rc, dtype=I32)).reshape(TOP_K, rc * n_tok)

    xs = _sc_scatter_rows(hp, [dest_rows[kk] for kk in range(TOP_K)], rc * cap)

    last = lambda j, be, nu: jnp.minimum(j, nu[0] - 1)
    y_rows = pl.pallas_call(
        functools.partial(_experts_kernel, bm=bm),
        grid_spec=pltpu.PrefetchScalarGridSpec(
            num_scalar_prefetch=2,
            grid=(nb,),
            in_specs=[pl.BlockSpec((rc * bm, LANES), lambda j, be, nu: (last(j, be, nu), 0)),
                      pl.BlockSpec((1, 1, d, f2), lambda j, be, nu: (layer, be[last(j, be, nu)], 0, 0)),
                      pl.BlockSpec((1, 1, 1, f2), lambda j, be, nu: (layer, be[last(j, be, nu)], 0, 0)),
                      pl.BlockSpec((1, 1, ff, d), lambda j, be, nu: (layer, be[last(j, be, nu)], 0, 0)),
                      pl.BlockSpec((1, 1, 1, d), lambda j, be, nu: (layer, be[last(j, be, nu)], 0, 0))],
            out_specs=pl.BlockSpec((rc * bm, LANES), lambda j, be, nu: (j, 0)),
            scratch_shapes=[pltpu.VMEM((d, f2), BF16), pltpu.VMEM((ff, d), BF16)]),
        out_shape=jax.ShapeDtypeStruct((rc * cap, LANES), U32),
        compiler_params=_cparams(1),
        name="moe_experts",
    )(block_expert, n_used, xs, w_gu, b_gu.reshape(*b_gu.shape[:2], 1, f2),
      w_down, b_down.reshape(*b_down.shape[:2], 1, d))

    y_assign = _sc_gather_rows(y_rows, dest_rows.reshape(-1)).reshape(TOP_K, rc * n_tok, LANES)

    tt = cfg["combine_tt"]
    assert s % tt == 0
    n_tiles = n_tok // tt
    out = pl.pallas_call(
        functools.partial(_combine_kernel, tt=tt, alpha=alpha),
        grid=(n_tiles,),
        in_specs=[pl.BlockSpec((TOP_K, rc * tt, LANES), lambda i: (0, i, 0)),
                  pl.BlockSpec((tt, d), lambda i: (i, 0)),
                  pl.BlockSpec((1, N_MOD, d), lambda i: (i * tt // s, 0, 0)),
                  pl.BlockSpec((tt, TOP_K), lambda i: (i, 0)),
                  pl.BlockSpec((1, d), const2),
                  pl.BlockSpec((1, d), const2)],
        out_specs=pl.BlockSpec((tt, d), lambda i: (i, 0)),
        out_shape=jax.ShapeDtypeStruct((n_tok, d), F32),
        compiler_params=_cparams(1),
        name="moe_combine",
    )(y_assign, xf, m, probs.T, ln_g.reshape(1, d), ln_b.reshape(1, d))
    return out.reshape(b, s, d)


def kernel(x, c, ada_w, ada_b, ln_g, ln_b, pool_w, pool_scale, dn_w_in, dn_conv_w, dn_a_log, dn_dt_bias,
           dn_norm_w, dn_w_out, router_w, router_b, exp_w_gu, exp_b_gu, exp_w_down, exp_b_down):
    b, s, d = x.shape
    depth = ada_w.shape[0]
    alpha = (2 * depth) ** 0.25
    cfg = _tile_config(s, b * s)
    mods = _mods(c, ada_w, ada_b)
    n_mixers = 2
    for i in range(depth):
        m = mods[i]
        j = i // n_mixers
        if i % n_mixers == 0:
            x = _pool_layer(x, m, pool_w[j], pool_scale[j], ln_g[i, 0], ln_b[i, 0], alpha, cfg)
        else:
            x = _deltanet_layer(x, m, dn_w_in[j], dn_conv_w[j], dn_a_log[j], dn_dt_bias[j], dn_norm_w[j],
                                dn_w_out[j], ln_g[i, 0], ln_b[i, 0], alpha, cfg)
        x = _moe_layer(x, m, i, router_w, router_b, exp_w_gu, exp_b_gu, exp_w_down, exp_b_down,
                       ln_g[i, 1], ln_b[i, 1], alpha, cfg)
    return x
```

```python
import functools

import jax
import jax.numpy as jnp
from jax import lax
from jax.experimental import pallas as pl
from jax.experimental.pallas import tpu as pltpu
from jax.experimental.pallas import tpu_sc as plsc

F32 = jnp.float32
BF16 = jnp.bfloat16
I32 = jnp.int32
U32 = jnp.uint32

N_MOD = 6
POOL_WINDOWS = (2, 4, 8, 16)
DN_HEADS = 8
DN_HEAD_DIM = 128
DN_CONV = 4
TOP_K = 4
SWIGLU_LIMIT = 7.0
SWIGLU_ALPHA = 1.702
LN_EPS = 1e-5
RMS_EPS = 1e-6

LANES = 128
SUBLANES = 8
VMEM_LIMIT_BYTES = 56 * 1024 * 1024
SC_WORKERS = 32
SC_WINDOW = 128

DN_CHUNK = LANES
DN_HEAD_GROUP = 2
POOL_HALO = 32
CONV_HALO = SUBLANES

HIGHEST = lax.Precision.HIGHEST


def _tile_config(seq, n_tok):
    return dict(
        pool_ts=min(512, seq),
        dn_in_ts=min(256, seq),
        dn_core_ts=min(512, seq),
        dn_out_ts=min(512, seq),
        router_tt=min(512, seq),
        expert_bm=512,
        combine_tt=min(256, seq),
    )


def _cparams(n_axes):
    return pltpu.CompilerParams(dimension_semantics=("arbitrary",) * n_axes,
                                vmem_limit_bytes=VMEM_LIMIT_BYTES)


def _layer_norm(v, g, b):
    mu = jnp.mean(v, axis=-1, keepdims=True)
    d = v - mu
    var = jnp.mean(d * d, axis=-1, keepdims=True)
    return d * lax.rsqrt(var + LN_EPS) * g + b


def _dot(a, b):
    return jnp.dot(a, b, preferred_element_type=F32)


def _dot_nt(a, b):
    return lax.dot_general(a, b, (((1,), (1,)), ((), ())), preferred_element_type=F32)


def _dot_tn(a, b):
    return lax.dot_general(a, b, (((0,), (0,)), ((), ())), preferred_element_type=F32)


def _bdot(a, b):
    return lax.dot_general(a, b, (((2,), (1,)), ((0,), (0,))), preferred_element_type=F32)


def _bdot_nt(a, b):
    return lax.dot_general(a, b, (((2,), (2,)), ((0,), (0,))), preferred_element_type=F32)


def _mods_kernel(c_ref, w_ref, b_ref, o_ref):
    c = c_ref[...]
    c_act = c * jax.nn.sigmoid(c)
    o_ref[0] = jnp.dot(c_act, w_ref[0], preferred_element_type=F32, precision=HIGHEST) + b_ref[0]


def _mods(c, ada_w, ada_b):
    depth, d, n = ada_w.shape
    b = c.shape[0]
    tn = 2048 if n % 2048 == 0 else n
    out = pl.pallas_call(
        _mods_kernel,
        grid=(depth, n // tn),
        in_specs=[pl.BlockSpec((b, d), lambda i, j: (0, 0)),
                  pl.BlockSpec((1, d, tn), lambda i, j: (i, 0, j)),
                  pl.BlockSpec((1, 1, tn), lambda i, j: (i, 0, j))],
        out_specs=pl.BlockSpec((1, b, tn), lambda i, j: (i, 0, j)),
        out_shape=jax.ShapeDtypeStruct((depth, b, n), F32),
        compiler_params=_cparams(2),
        name="adaln_mods",
    )(c, ada_w, ada_b.reshape(depth, 1, n))
    return out.reshape(depth, b, N_MOD, d)


def _pool_kernel(x_ref, m_ref, pw_ref, ps_ref, lg_ref, lb_ref, o_ref, e1, ea, eb, *, ts, alpha):
    s = pl.program_id(1)
    d = x_ref.shape[-1]
    gw = d // len(POOL_WINDOWS)
    halo = POOL_HALO
    rows = halo + ts
    x = x_ref[0]
    sh, sc, gt = m_ref[0, 0:1, :], m_ref[0, 1:2, :], m_ref[0, 2:3, :]
    h = x * (1.0 + sc) + sh

    @pl.when(s == 0)
    def _():
        e1[0:halo, :] = jnp.zeros((halo, d), F32)

    e1[halo:rows, :] = h
    ea[8:rows, :] = e1[8:rows, :] + e1[7:rows - 1, :]
    eb[16:rows, gw:] = ea[16:rows, gw:] + ea[14:rows - 2, gw:]
    ea[24:rows, 2 * gw:] = eb[24:rows, 2 * gw:] + eb[20:rows - 4, 2 * gw:]
    eb[32:rows, 3 * gw:] = ea[32:rows, 3 * gw:] + ea[24:rows - 8, 3 * gw:]

    pos = s * ts + lax.broadcasted_iota(I32, (ts, 1), 0)
    outs = []
    for g, win in enumerate(POOL_WINDOWS):
        src = ea if g % 2 == 0 else eb
        cols = slice(g * gw, (g + 1) * gw)
        cnt = jnp.minimum(pos + 1, win).astype(F32)
        pooled = src[halo:rows, cols] / cnt - h[:, cols]
        outs.append(_dot(pooled.astype(BF16), pw_ref[g]))
    y = jnp.concatenate(outs, axis=1) * ps_ref[...]
    o_ref[0] = _layer_norm(alpha * x + (1.0 + gt) * y, lg_ref[...], lb_ref[...])
    e1[0:halo, :] = e1[ts:rows, :]


def _pool_layer(x, m, pool_w, pool_scale, ln_g, ln_b, alpha, cfg):
    b, s, d = x.shape
    ts = cfg["pool_ts"]
    g, gw, _ = pool_w.shape
    assert POOL_WINDOWS == (2, 4, 8, 16) and g == len(POOL_WINDOWS) and s % ts == 0 and ts >= POOL_HALO
    row = lambda v: v.reshape(1, d)
    return pl.pallas_call(
        functools.partial(_pool_kernel, ts=ts, alpha=alpha),
        grid=(b, s // ts),
        in_specs=[pl.BlockSpec((1, ts, d), lambda i, j: (i, j, 0)),
                  pl.BlockSpec((1, N_MOD, d), lambda i, j: (i, 0, 0)),
                  pl.BlockSpec((g, gw, gw), lambda i, j: (0, 0, 0)),
                  pl.BlockSpec((1, d), lambda i, j: (0, 0)),
                  pl.BlockSpec((1, d), lambda i, j: (0, 0)),
                  pl.BlockSpec((1, d), lambda i, j: (0, 0))],
        out_specs=pl.BlockSpec((1, ts, d), lambda i, j: (i, j, 0)),
        out_shape=jax.ShapeDtypeStruct((b, s, d), F32),
        scratch_shapes=[pltpu.VMEM((POOL_HALO + ts, d), F32)] * 3,
        compiler_params=_cparams(2),
        name="pool_layer",
    )(x, m, pool_w.astype(BF16), row(pool_scale), row(ln_g), row(ln_b))


def _dn_in_kernel(x_ref, m_ref, wm_ref, ws_ref, cw_ref, av_ref, dv_ref,
                  q_ref, k_ref, v_ref, z_ref, bg_ref, ext, *, ts):
    s = pl.program_id(1)
    nh, dh = DN_HEADS, DN_HEAD_DIM
    w = nh * dh
    halo = CONV_HALO
    x = x_ref[0]
    sh, sc = m_ref[0, 0:1, :], m_ref[0, 1:2, :]
    h = (x * (1.0 + sc) + sh).astype(BF16)
    proj = _dot(h, wm_ref[...])

    @pl.when(s == 0)
    def _():
        ext[0:halo, :] = jnp.zeros((halo, 3 * w), F32)

    ext[halo:halo + ts, :] = proj[:, :3 * w]
    z_ref[0] = proj[:, 3 * w:]
    conv = cw_ref[0:1, :] * ext[halo - 3:halo - 3 + ts, :]
    for j in range(1, DN_CONV):
        conv = conv + cw_ref[j:j + 1, :] * ext[halo - 3 + j:halo - 3 + j + ts, :]
    act = conv * jax.nn.sigmoid(conv)
    ext[0:halo, :] = ext[ts:ts + halo, :]
    for hd in range(nh):
        qh = act[:, hd * dh:(hd + 1) * dh]
        kh = act[:, w + hd * dh:w + (hd + 1) * dh]
        q_ref[0, hd] = qh * lax.rsqrt(jnp.sum(qh * qh, axis=-1, keepdims=True) + RMS_EPS) * (dh ** -0.5)
        k_ref[0, hd] = kh * lax.rsqrt(jnp.sum(kh * kh, axis=-1, keepdims=True) + RMS_EPS)
        v_ref[0, hd] = act[:, 2 * w + hd * dh:2 * w + (hd + 1) * dh]
    small = _dot(h, ws_ref[...])
    lane = lax.broadcasted_iota(I32, small.shape, 1)
    beta = jax.nn.sigmoid(small)
    g = -jnp.exp(av_ref[...]) * jax.nn.softplus(small + dv_ref[...])
    bg_ref[0] = jnp.where(lane < nh, beta, jnp.where(lane < 2 * nh, g, 0.0))


def _dn_core_kernel(q_ref, k_ref, v_ref, bg_ref, o_ref,
                    st, bb, gcb, gt, gl_s, u_s, wq_s, ai_s, kd_s, *, ts):
    s = pl.program_id(1)
    nh, dh, c = DN_HEADS, DN_HEAD_DIM, DN_CHUNK
    nc = ts // c

    @pl.when(s == 0)
    def _():
        st[...] = jnp.zeros(st.shape, F32)

    bgv = bg_ref[0]
    lane = lax.broadcasted_iota(I32, bgv.shape, 1)
    rowc = lax.broadcasted_iota(I32, bgv.shape, 0) % c
    gc = jnp.where(lane >= nh, bgv, 0.0)
    shift = 1
    while shift < c:
        gc = gc + jnp.where(rowc >= shift, pltpu.roll(gc, shift, 0), 0.0)
        shift *= 2
    gc_t = gc.T
    for hd in range(nh):
        for ci in range(nc):
            gt[hd * nc + ci] = jnp.broadcast_to(gc_t[nh + hd:nh + hd + 1, ci * c:(ci + 1) * c], (SUBLANES, c))
        bb[hd] = jnp.broadcast_to(bgv[:, hd:hd + 1], (ts, LANES))
        gcb[hd] = jnp.broadcast_to(gc[:, nh + hd:nh + hd + 1], (ts, LANES))

    ri = lax.broadcasted_iota(I32, (c, c), 0)
    ci_ = lax.broadcasted_iota(I32, (c, c), 1)
    tril = ri >= ci_
    eye = (ri == ci_).astype(F32)
    off_masks = []
    blk = 1
    while blk < c:
        off_masks.append((ri // blk != ci_ // blk) & (ri // (2 * blk) == ci_ // (2 * blk)) & (ri > ci_))
        blk *= 2

    hg = DN_HEAD_GROUP
    nb = hg * nc

    def group_body(gi, carry):
        heads = pl.ds(gi * hg, hg)
        chunks = lambda a: a.reshape(nb, c, a.shape[-1])
        per_head = lambda a: a.reshape(hg, ts, a.shape[-1])
        q, k, v = chunks(q_ref[0, heads]), chunks(k_ref[0, heads]), chunks(v_ref[0, heads])
        beta, gch = chunks(bb[heads]), chunks(gcb[heads])
        egh = jnp.exp(gch)
        g_last = jnp.broadcast_to(gch[:, c - 1:c, :], gch.shape)
        kb = k * beta
        vb = v * beta
        kd_s[heads] = per_head((k * jnp.exp(g_last - gch)).astype(BF16))
        gl_s[heads] = jnp.broadcast_to(egh[:, c - 1:c, :], (nb, SUBLANES, LANES)).reshape(hg, nc * SUBLANES, LANES)
        kc = k.astype(BF16)
        grow = gt[pl.ds(gi * nb, nb), 0:1, :]
        dec = jnp.where(tril, jnp.exp(jnp.where(tril, gch - grow, 0.0)), 0.0)
        a_kk = _bdot_nt(kb.astype(BF16), kc)
        a_qk = _bdot_nt(q.astype(BF16), kc)
        ai_s[heads] = per_head((a_qk * dec).astype(BF16))
        l_mat = a_kk * dec
        t_inv = eye - jnp.where(off_masks[0], l_mat, 0.0)
        for off in off_masks[1:]:
            tb = t_inv.astype(BF16)
            l_off = jnp.where(off, l_mat, 0.0).astype(BF16)
            t_inv = t_inv - _bdot(tb, _bdot(l_off, tb).astype(BF16))
        rhs = jnp.concatenate([vb, kb * egh], axis=2).astype(BF16)
        uw = _bdot(t_inv.astype(BF16), rhs)
        u_s[heads] = per_head(uw[:, :, :dh])
        wq = jnp.concatenate([uw[:, :, dh:], q * egh], axis=1).astype(BF16)
        wq_s[heads] = wq.reshape(hg, 2 * ts, dh)
        return carry

    lax.fori_loop(0, nh // hg, group_body, 0)

    def chunk_body(ci, carry):
        rows = pl.ds(pl.multiple_of(ci * c, c), c)
        rows2 = pl.ds(pl.multiple_of(ci * 2 * c, 2 * c), 2 * c)
        heads = range(nh)
        states = [st[hd] for hd in heads]
        ws_qs = [_dot(wq_s[hd, rows2, :], states[hd].astype(BF16)) for hd in heads]
        v_nb = [(u_s[hd, rows, :] - ws_qs[hd][:c]).astype(BF16) for hd in heads]
        outs = [ws_qs[hd][c:] + _dot(ai_s[hd, rows, :], v_nb[hd]) for hd in heads]
        new_states = [states[hd] * gl_s[hd, pl.ds(pl.multiple_of(ci * SUBLANES, SUBLANES), 1), :]
                      + _dot_tn(kd_s[hd, rows, :], v_nb[hd]) for hd in heads]
        for hd in heads:
            o_ref[0, hd, rows, :] = outs[hd]
            st[hd] = new_states[hd]
        return carry

    lax.fori_loop(0, nc, chunk_body, 0)


def _dn_out_kernel(o_ref, z_ref, x_ref, m_ref, nw_ref, wo_ref, lg_ref, lb_ref, out_ref, *, alpha):
    nh, dh = DN_HEADS, DN_HEAD_DIM
    z = z_ref[0]
    parts = []
    for hd in range(nh):
        oh = o_ref[0, hd]
        on = oh * lax.rsqrt(jnp.mean(oh * oh, axis=-1, keepdims=True) + RMS_EPS) * nw_ref[...]
        zh = z[:, hd * dh:(hd + 1) * dh]
        parts.append((on * (zh * jax.nn.sigmoid(zh))).astype(BF16))
    y = _dot(jnp.concatenate(parts, axis=1), wo_ref[...])
    x = x_ref[0]
    gt = m_ref[0, 2:3, :]
    out_ref[0] = _layer_norm(alpha * x + (1.0 + gt) * y, lg_ref[...], lb_ref[...])


def _deltanet_layer(x, m, w_in, conv_w, a_log, dt_bias, norm_w, w_out, ln_g, ln_b, alpha, cfg):
    b, s, d = x.shape
    nh, dh = DN_HEADS, DN_HEAD_DIM
    w = nh * dh
    assert w_in.shape == (d, 4 * w + 2 * nh) and conv_w.shape == (DN_CONV, 3 * w) and 2 * nh <= LANES
    w_main = w_in[:, :4 * w].astype(BF16)
    w_small = jnp.pad(w_in[:, 4 * w:], ((0, 0), (0, LANES - 2 * nh))).astype(BF16)
    avec = jnp.zeros((1, LANES), F32).at[0, nh:2 * nh].set(a_log.astype(F32))
    dvec = jnp.zeros((1, LANES), F32).at[0, nh:2 * nh].set(dt_bias.astype(F32))
    row = lambda v: v.reshape(1, -1)
    const2 = lambda i, j: (0, 0)

    ts = cfg["dn_in_ts"]
    assert s % ts == 0 and ts >= CONV_HALO
    head_major = jax.ShapeDtypeStruct((b, nh, s, dh), F32)
    hm_spec = lambda t: pl.BlockSpec((1, nh, t, dh), lambda i, j: (i, 0, j, 0))
    q, k, v, z, bg = pl.pallas_call(
        functools.partial(_dn_in_kernel, ts=ts),
        grid=(b, s // ts),
        in_specs=[pl.BlockSpec((1, ts, d), lambda i, j: (i, j, 0)),
                  pl.BlockSpec((1, N_MOD, d), lambda i, j: (i, 0, 0)),
                  pl.BlockSpec((d, 4 * w), const2),
                  pl.BlockSpec((d, LANES), const2),
                  pl.BlockSpec((DN_CONV, 3 * w), const2),
                  pl.BlockSpec((1, LANES), const2),
                  pl.BlockSpec((1, LANES), const2)],
        out_specs=[hm_spec(ts), hm_spec(ts), hm_spec(ts),
                   pl.BlockSpec((1, ts, w), lambda i, j: (i, j, 0)),
                   pl.BlockSpec((1, ts, LANES), lambda i, j: (i, j, 0))],
        out_shape=[head_major, head_major, head_major,
                   jax.ShapeDtypeStruct((b, s, w), F32),
                   jax.ShapeDtypeStruct((b, s, LANES), F32)],
        scratch_shapes=[pltpu.VMEM((CONV_HALO + ts, 3 * w), F32)],
        compiler_params=_cparams(2),
        name="deltanet_in",
    )(x, m, w_main, w_small, conv_w, avec, dvec)

    ts = cfg["dn_core_ts"]
    assert s % ts == 0 and ts % DN_CHUNK == 0 and dh == DN_CHUNK
    per_head = pltpu.VMEM((nh, ts, dh), F32)
    o = pl.pallas_call(
        functools.partial(_dn_core_kernel, ts=ts),
        grid=(b, s // ts),
        in_specs=[hm_spec(ts), hm_spec(ts), hm_spec(ts),
                  pl.BlockSpec((1, ts, LANES), lambda i, j: (i, j, 0))],
        out_specs=hm_spec(ts),
        out_shape=head_major,
        scratch_shapes=[pltpu.VMEM((nh, dh, dh), F32),
                        per_head, per_head,
                        pltpu.VMEM((nh * (ts // DN_CHUNK), SUBLANES, DN_CHUNK), F32),
                        pltpu.VMEM((nh, ts // DN_CHUNK * SUBLANES, LANES), F32),
                        per_head,
                        pltpu.VMEM((nh, 2 * ts, dh), BF16),
                        pltpu.VMEM((nh, ts, DN_CHUNK), BF16),
                        pltpu.VMEM((nh, ts, dh), BF16)],
        compiler_params=_cparams(2),
        name="deltanet_core",
    )(q, k, v, bg)

    ts = cfg["dn_out_ts"]
    assert s % ts == 0
    return pl.pallas_call(
        functools.partial(_dn_out_kernel, alpha=alpha),
        grid=(b, s // ts),
        in_specs=[hm_spec(ts),
                  pl.BlockSpec((1, ts, w), lambda i, j: (i, j, 0)),
                  pl.BlockSpec((1, ts, d), lambda i, j: (i, j, 0)),
                  pl.BlockSpec((1, N_MOD, d), lambda i, j: (i, 0, 0)),
                  pl.BlockSpec((1, dh), const2),
                  pl.BlockSpec((w, d), const2),
                  pl.BlockSpec((1, d), const2),
                  pl.BlockSpec((1, d), const2)],
        out_specs=pl.BlockSpec((1, ts, d), lambda i, j: (i, j, 0)),
        out_shape=jax.ShapeDtypeStruct((b, s, d), F32),
        compiler_params=_cparams(2),
        name="deltanet_out",
    )(o, z, x, m, row(norm_w), w_out.astype(BF16), row(ln_g), row(ln_b))


def _pack_row_chunks(v):
    half = v.shape[-1] // 2
    bits = pltpu.bitcast(v.astype(BF16).astype(F32), U32)
    packed = (bits[:, :half] >> 16) | (bits[:, half:] & jnp.uint32(0xFFFF0000))
    return [packed[:, i * LANES:(i + 1) * LANES] for i in range(half // LANES)]


def _unpack_row_chunks(chunks):
    lo = [pltpu.bitcast(c << 16, F32) for c in chunks]
    hi = [pltpu.bitcast(c & jnp.uint32(0xFFFF0000), F32) for c in chunks]
    return jnp.concatenate(lo + hi, axis=1)


def _store_row_chunks(ref, chunks):
    for i, ch in enumerate(chunks):
        ref[i] = ch


def _load_row_chunks(ref, lead=()):
    return [ref[(*lead, i)] for i in range(ref.shape[len(lead)])]


def _sc_mesh():
    return plsc.VectorSubcoreMesh(core_axis_name="c", subcore_axis_name="s")


def _sc_gather_rows(table, idx):
    n = idx.shape[1]
    assert table.shape[1] == LANES and idx.shape[0] == 1 and n % (SC_WINDOW * SC_WORKERS) == 0

    @functools.partial(pl.kernel, out_type=jax.ShapeDtypeStruct((n, LANES), table.dtype), mesh=_sc_mesh(),
                       name="sc_gather_rows")
    def gather(table_hbm, idx_hbm, out_hbm):
        def body(idx_vmem, out_vmem):
            pltpu.sync_copy(table_hbm.at[idx_vmem.at[0]], out_vmem)

        pltpu.emit_pipeline(
            body,
            grid=(n // SC_WINDOW,),
            in_specs=[pl.BlockSpec((1, SC_WINDOW), lambda i: (0, i))],
            out_specs=[pl.BlockSpec((SC_WINDOW, LANES), lambda i: (i, 0))],
            core_axis_name=("c", "s"),
            dimension_semantics=(pltpu.PARALLEL,),
        )(idx_hbm, out_hbm)

    return gather(table, idx)


def _sc_scatter_rows(x, idx_list, n_out):
    n = x.shape[0]
    assert x.shape[1] == LANES and n % (SC_WINDOW * SC_WORKERS) == 0
    assert all(idx.shape == (1, n) for idx in idx_list)

    @functools.partial(pl.kernel, out_type=jax.ShapeDtypeStruct((n_out, LANES), x.dtype), mesh=_sc_mesh(),
                       name="sc_scatter_rows")
    def scatter(x_hbm, *refs):
        idx_hbms, out_hbm = refs[:-1], refs[-1]

        def body(x_vmem, *idx_vmems):
            for idx_vmem in idx_vmems:
                pltpu.sync_copy(x_vmem, out_hbm.at[idx_vmem.at[0]])

        pltpu.emit_pipeline(
            body,
            grid=(n // SC_WINDOW,),
            in_specs=[pl.BlockSpec((SC_WINDOW, LANES), lambda i: (i, 0))]
                     + [pl.BlockSpec((1, SC_WINDOW), lambda i: (0, i))] * len(idx_list),
            out_specs=[],
            core_axis_name=("c", "s"),
            dimension_semantics=(pltpu.PARALLEL,),
        )(x_hbm, *idx_hbms)

    return scatter(x, *idx_list)


def _router_kernel(x_ref, m_ref, rwt_ref, rb_ref, tri_ref,
                   hp_ref, idx_ref, p_ref, rank_ref, cnt_ref, run, *, tt):
    i = pl.program_id(0)

    @pl.when(i == 0)
    def _():
        run[...] = jnp.zeros(run.shape, F32)

    x = x_ref[...]
    d = x.shape[-1]
    sh, sc = m_ref[0, 3:4, :], m_ref[0, 4:5, :]
    h = x * (1.0 + sc) + sh
    _store_row_chunks(hp_ref, _pack_row_chunks(h))

    logits = lax.dot_general(rwt_ref[...], h, (((1,), (1,)), ((), ())),
                             preferred_element_type=F32, precision=HIGHEST) + rb_ref[...]
    ne = logits.shape[0]
    eio = lax.broadcasted_iota(I32, logits.shape, 0).astype(F32)
    vals, idxs, sels = [], [], []
    for _ in range(TOP_K):
        mx = jnp.max(logits, axis=0, keepdims=True)
        ix = jnp.min(jnp.where(logits == mx, eio, float(ne)), axis=0, keepdims=True)
        sel = eio == ix
        logits = jnp.where(sel, -jnp.inf, logits)
        vals.append(mx)
        idxs.append(ix)
        sels.append(sel)
    exps = [jnp.exp(v - vals[0]) for v in vals]
    den = functools.reduce(lambda a, b_: a + b_, exps)
    chosen = functools.reduce(jnp.logical_or, sels)
    onehot = jnp.where(chosen, 1.0, 0.0)
    before = _dot(onehot.astype(BF16), tri_ref[...]) + run[...]
    ranks = [jnp.sum(jnp.where(sel, before, 0.0), axis=0, keepdims=True) for sel in sels]
    run[...] = run[...] + jnp.sum(onehot, axis=1, keepdims=True)
    idx_ref[...] = jnp.concatenate(idxs, axis=0).astype(I32)
    p_ref[...] = jnp.concatenate([e / den for e in exps]
                                 + [jnp.zeros((p_ref.shape[0] - TOP_K, den.shape[1]), F32)], axis=0)
    rank_ref[...] = jnp.concatenate(ranks, axis=0).astype(I32)
    cnt_ref[...] = jnp.broadcast_to(run[...], cnt_ref.shape)


def _experts_kernel(be_ref, nu_ref, xs_ref, wgu_ref, bgu_ref, wd_ref, bd_ref, y_ref, wgu_b, wd_b):
    j = pl.program_id(0)

    @pl.when(j >= nu_ref[0])
    def _():
        y_ref[...] = jnp.zeros(y_ref.shape, U32)

    @pl.when(j < nu_ref[0])
    def _():
        new_expert = jnp.logical_or(j == 0, be_ref[j] != be_ref[jnp.maximum(j - 1, 0)])

        @pl.when(new_expert)
        def _():
            wgu_b[...] = wgu_ref[0, 0].astype(BF16)
            wd_b[...] = wd_ref[0, 0].astype(BF16)

        xb = _unpack_row_chunks(_load_row_chunks(xs_ref)).astype(BF16)
        gu = _dot(xb, wgu_b[...]) + bgu_ref[0, 0]
        f = gu.shape[1] // 2
        glu = jnp.minimum(gu[:, :f], SWIGLU_LIMIT)
        lin = jnp.clip(gu[:, f:], -SWIGLU_LIMIT, SWIGLU_LIMIT)
        act = glu * jax.nn.sigmoid(SWIGLU_ALPHA * glu) * (lin + 1.0)
        y = _dot(act.astype(BF16), wd_b[...]) + bd_ref[0, 0]
        _store_row_chunks(y_ref, _pack_row_chunks(y))


def _combine_kernel(ya_ref, x_ref, m_ref, p_ref, lg_ref, lb_ref, o_ref, *, alpha):
    p_rows = p_ref[...]
    eye = (lax.broadcasted_iota(I32, (p_rows.shape[0], LANES), 0)
           == lax.broadcasted_iota(I32, (p_rows.shape[0], LANES), 1)).astype(F32)
    p = lax.dot_general(p_rows, eye, (((0,), (0,)), ((), ())), preferred_element_type=F32,
                        precision=HIGHEST)
    y = None
    for kk in range(TOP_K):
        yk = p[:, kk:kk + 1] * _unpack_row_chunks(_load_row_chunks(ya_ref, (kk,)))
        y = yk if y is None else y + yk
    gt = m_ref[0, 5:6, :]
    o_ref[...] = _layer_norm(alpha * x_ref[...] + (1.0 + gt) * y, lg_ref[...], lb_ref[...])


def _moe_layer(x, m, layer, router_w, router_b, w_gu, b_gu, w_down, b_down, ln_g, ln_b, alpha, cfg):
    b, s, d = x.shape
    n_tok = b * s
    ne = router_w.shape[-1]
    f2 = w_gu.shape[-1]
    ff = f2 // 2
    assert d % (2 * LANES) == 0 and w_down.shape[-2:] == (ff, d)
    rc = d // (2 * LANES)
    xf = x.reshape(n_tok, d)
    const2 = lambda i: (0, 0)

    tt = cfg["router_tt"]
    assert s % tt == 0
    tri = jnp.triu(jnp.ones((tt, tt), BF16), k=1)
    hp, idx, probs, rank, cnt = pl.pallas_call(
        functools.partial(_router_kernel, tt=tt),
        grid=(n_tok // tt,),
        in_specs=[pl.BlockSpec((tt, d), lambda i: (i, 0)),
                  pl.BlockSpec((1, N_MOD, d), lambda i: (i * tt // s, 0, 0)),
                  pl.BlockSpec((ne, d), const2),
                  pl.BlockSpec((ne, 1), const2),
                  pl.BlockSpec((tt, tt), const2)],
        out_specs=[pl.BlockSpec((rc, tt, LANES), lambda i: (0, i, 0)),
                   pl.BlockSpec((TOP_K, tt), lambda i: (0, i)),
                   pl.BlockSpec((SUBLANES, tt), lambda i: (0, i)),
                   pl.BlockSpec((TOP_K, tt), lambda i: (0, i)),
                   pl.BlockSpec((ne, LANES), const2)],
        out_shape=[jax.ShapeDtypeStruct((rc, n_tok, LANES), U32),
                   jax.ShapeDtypeStruct((TOP_K, n_tok), I32),
                   jax.ShapeDtypeStruct((SUBLANES, n_tok), F32),
                   jax.ShapeDtypeStruct((TOP_K, n_tok), I32),
                   jax.ShapeDtypeStruct((ne, LANES), F32)],
        scratch_shapes=[pltpu.VMEM((ne, 1), F32)],
        compiler_params=_cparams(1),
        name="moe_router",
    )(xf, m, router_w[layer].T, router_b[layer].reshape(ne, 1), tri)

    bm = cfg["expert_bm"]
    cap = n_tok * TOP_K + ne * bm
    nb = cap // bm
    counts = cnt[:, 0].astype(I32)
    padded = (counts + bm - 1) // bm * bm
    pad_end = jnp.cumsum(padded)
    pad_start = pad_end - padded
    eids = jnp.arange(ne, dtype=I32)[:, None, None]
    dest = rank + jnp.sum(jnp.where(idx[None] == eids, pad_start[:, None, None], 0), axis=0)
    block_start = jnp.arange(nb, dtype=I32) * bm
    block_expert = jnp.minimum(jnp.sum(block_start[None, :] >= pad_end[:, None], axis=0), ne - 1).astype(I32)
    n_used = (pad_end[-1:] // bm).astype(I32)

    plane_offset = (jnp.arange(rc * n_tok, dtype=I32) // n_tok * cap)[None, :]
    slot_rows = [jnp.tile(dest[kk:kk + 1], (1, rc)) + plane_offset for kk in range(TOP_K)]

    xs = _sc_scatter_rows(hp.reshape(rc * n_tok, LANES), slot_rows, rc * cap).reshape(rc, cap, LANES)

    last = lambda j, be, nu: jnp.minimum(j, nu[0] - 1)
    y_rows = pl.pallas_call(
        _experts_kernel,
        grid_spec=pltpu.PrefetchScalarGridSpec(
            num_scalar_prefetch=2,
            grid=(nb,),
            in_specs=[pl.BlockSpec((rc, bm, LANES), lambda j, be, nu: (0, last(j, be, nu), 0)),
                      pl.BlockSpec((1, 1, d, f2), lambda j, be, nu: (layer, be[last(j, be, nu)], 0, 0)),
                      pl.BlockSpec((1, 1, 1, f2), lambda j, be, nu: (layer, be[last(j, be, nu)], 0, 0)),
                      pl.BlockSpec((1, 1, ff, d), lambda j, be, nu: (layer, be[last(j, be, nu)], 0, 0)),
                      pl.BlockSpec((1, 1, 1, d), lambda j, be, nu: (layer, be[last(j, be, nu)], 0, 0))],
            out_specs=pl.BlockSpec((rc, bm, LANES), lambda j, be, nu: (0, j, 0)),
            scratch_shapes=[pltpu.VMEM((d, f2), BF16), pltpu.VMEM((ff, d), BF16)]),
        out_shape=jax.ShapeDtypeStruct((rc, cap, LANES), U32),
        compiler_params=_cparams(1),
        name="moe_experts",
    )(block_expert, n_used, xs, w_gu, b_gu.reshape(*b_gu.shape[:2], 1, f2),
      w_down, b_down.reshape(*b_down.shape[:2], 1, d))

    y_assign = _sc_gather_rows(y_rows.reshape(rc * cap, LANES), jnp.concatenate(slot_rows, axis=1))
    y_assign = y_assign.reshape(TOP_K, rc, n_tok, LANES)

    tt = cfg["combine_tt"]
    assert s % tt == 0
    n_tiles = n_tok // tt
    out = pl.pallas_call(
        functools.partial(_combine_kernel, alpha=alpha),
        grid=(n_tiles,),
        in_specs=[pl.BlockSpec((TOP_K, rc, tt, LANES), lambda i: (0, 0, i, 0)),
                  pl.BlockSpec((tt, d), lambda i: (i, 0)),
                  pl.BlockSpec((1, N_MOD, d), lambda i: (i * tt // s, 0, 0)),
                  pl.BlockSpec((SUBLANES, tt), lambda i: (0, i)),
                  pl.BlockSpec((1, d), const2),
                  pl.BlockSpec((1, d), const2)],
        out_specs=pl.BlockSpec((tt, d), lambda i: (i, 0)),
        out_shape=jax.ShapeDtypeStruct((n_tok, d), F32),
        compiler_params=_cparams(1),
        name="moe_combine",
    )(y_assign, xf, m, probs, ln_g.reshape(1, d), ln_b.reshape(1, d))
    return out.reshape(b, s, d)


def kernel(x, c, ada_w, ada_b, ln_g, ln_b, pool_w, pool_scale, dn_w_in, dn_conv_w, dn_a_log, dn_dt_bias,
           dn_norm_w, dn_w_out, router_w, router_b, exp_w_gu, exp_b_gu, exp_w_down, exp_b_down):
    b, s, d = x.shape
    depth = ada_w.shape[0]
    alpha = (2 * depth) ** 0.25
    cfg = _tile_config(s, b * s)
    mods = _mods(c, ada_w, ada_b)
    n_mixers = 2
    for i in range(depth):
        m = mods[i]
        j = i // n_mixers
        if i % n_mixers == 0:
            x = _pool_layer(x, m, pool_w[j], pool_scale[j], ln_g[i, 0], ln_b[i, 0], alpha, cfg)
        else:
            x = _deltanet_layer(x, m, dn_w_in[j], dn_conv_w[j], dn_a_log[j], dn_dt_bias[j], dn_norm_w[j],
                                dn_w_out[j], ln_g[i, 0], ln_b[i, 0], alpha, cfg)
        x = _moe_layer(x, m, i, router_w, router_b, exp_w_gu, exp_b_gu, exp_w_down, exp_b_down,
                       ln_g[i, 1], ln_b[i, 1], alpha, cfg)
    return x
```

```python
import functools

import jax
import jax.numpy as jnp
from jax import lax
from jax.experimental import pallas as pl
from jax.experimental.pallas import tpu as pltpu
from jax.experimental.pallas import tpu_sc as plsc

F32 = jnp.float32
BF16 = jnp.bfloat16
I32 = jnp.int32
U32 = jnp.uint32

N_MOD = 6
POOL_WINDOWS = (2, 4, 8, 16)
DN_HEADS = 8
DN_HEAD_DIM = 128
DN_CONV = 4
TOP_K = 4
SWIGLU_LIMIT = 7.0
SWIGLU_ALPHA = 1.702
LN_EPS = 1e-5
RMS_EPS = 1e-6

LANES = 128
SUBLANES = 8
VMEM_LIMIT_BYTES = 56 * 1024 * 1024
SC_WORKERS = 32
SC_WINDOW = 128

DN_CHUNK = LANES
DN_HEAD_GROUP = 2
POOL_HALO = 32
CONV_HALO = SUBLANES

HIGHEST = lax.Precision.HIGHEST


def _tile_config(seq, n_tok):
    return dict(
        pool_ts=min(512, seq),
        dn_in_ts=min(512, seq),
        dn_in_sub=128,
        dn_core_ts=min(512, seq),
        dn_out_ts=min(512, seq),
        router_tt=min(512, seq),
        expert_bm=512,
        combine_tt=min(256, seq),
    )


def _cparams(n_axes):
    return pltpu.CompilerParams(dimension_semantics=("arbitrary",) * n_axes,
                                vmem_limit_bytes=VMEM_LIMIT_BYTES)


def _layer_norm(v, g, b):
    mu = jnp.mean(v, axis=-1, keepdims=True)
    d = v - mu
    var = jnp.mean(d * d, axis=-1, keepdims=True)
    return d * lax.rsqrt(var + LN_EPS) * g + b


def _dot(a, b):
    return jnp.dot(a, b, preferred_element_type=F32)


def _dot_nt(a, b):
    return lax.dot_general(a, b, (((1,), (1,)), ((), ())), preferred_element_type=F32)


def _dot_tn(a, b):
    return lax.dot_general(a, b, (((0,), (0,)), ((), ())), preferred_element_type=F32)


def _bdot(a, b):
    return lax.dot_general(a, b, (((2,), (1,)), ((0,), (0,))), preferred_element_type=F32)


def _bdot_nt(a, b):
    return lax.dot_general(a, b, (((2,), (2,)), ((0,), (0,))), preferred_element_type=F32)


def _mods_kernel(c_ref, w_ref, b_ref, o_ref):
    c = c_ref[...]
    c_act = c * jax.nn.sigmoid(c)
    o_ref[0] = jnp.dot(c_act, w_ref[0], preferred_element_type=F32, precision=HIGHEST) + b_ref[0]


def _mods(c, ada_w, ada_b):
    depth, d, n = ada_w.shape
    b = c.shape[0]
    tn = 2048 if n % 2048 == 0 else n
    out = pl.pallas_call(
        _mods_kernel,
        grid=(depth, n // tn),
        in_specs=[pl.BlockSpec((b, d), lambda i, j: (0, 0)),
                  pl.BlockSpec((1, d, tn), lambda i, j: (i, 0, j)),
                  pl.BlockSpec((1, 1, tn), lambda i, j: (i, 0, j))],
        out_specs=pl.BlockSpec((1, b, tn), lambda i, j: (i, 0, j)),
        out_shape=jax.ShapeDtypeStruct((depth, b, n), F32),
        compiler_params=_cparams(2),
        name="adaln_mods",
    )(c, ada_w, ada_b.reshape(depth, 1, n))
    return out.reshape(depth, b, N_MOD, d)


def _pool_kernel(x_ref, m_ref, pw_ref, ps_ref, lg_ref, lb_ref, o_ref, e1, ea, eb, *, ts, alpha):
    s = pl.program_id(1)
    d = x_ref.shape[-1]
    gw = d // len(POOL_WINDOWS)
    halo = POOL_HALO
    rows = halo + ts
    x = x_ref[0]
    sh, sc, gt = m_ref[0, 0:1, :], m_ref[0, 1:2, :], m_ref[0, 2:3, :]
    h = x * (1.0 + sc) + sh

    @pl.when(s == 0)
    def _():
        e1[0:halo, :] = jnp.zeros((halo, d), F32)

    e1[halo:rows, :] = h
    ea[8:rows, :] = e1[8:rows, :] + e1[7:rows - 1, :]
    eb[16:rows, gw:] = ea[16:rows, gw:] + ea[14:rows - 2, gw:]
    ea[24:rows, 2 * gw:] = eb[24:rows, 2 * gw:] + eb[20:rows - 4, 2 * gw:]
    eb[32:rows, 3 * gw:] = ea[32:rows, 3 * gw:] + ea[24:rows - 8, 3 * gw:]

    pos = s * ts + lax.broadcasted_iota(I32, (ts, 1), 0)
    outs = []
    for g, win in enumerate(POOL_WINDOWS):
        src = ea if g % 2 == 0 else eb
        cols = slice(g * gw, (g + 1) * gw)
        cnt = jnp.minimum(pos + 1, win).astype(F32)
        pooled = src[halo:rows, cols] / cnt - h[:, cols]
        outs.append(_dot(pooled.astype(BF16), pw_ref[g]))
    y = jnp.concatenate(outs, axis=1) * ps_ref[...]
    o_ref[0] = _layer_norm(alpha * x + (1.0 + gt) * y, lg_ref[...], lb_ref[...])
    e1[0:halo, :] = e1[ts:rows, :]


def _pool_layer(x, m, pool_w, pool_scale, ln_g, ln_b, alpha, cfg):
    b, s, d = x.shape
    ts = cfg["pool_ts"]
    g, gw, _ = pool_w.shape
    assert POOL_WINDOWS == (2, 4, 8, 16) and g == len(POOL_WINDOWS) and s % ts == 0 and ts >= POOL_HALO
    row = lambda v: v.reshape(1, d)
    return pl.pallas_call(
        functools.partial(_pool_kernel, ts=ts, alpha=alpha),
        grid=(b, s // ts),
        in_specs=[pl.BlockSpec((1, ts, d), lambda i, j: (i, j, 0)),
                  pl.BlockSpec((1, N_MOD, d), lambda i, j: (i, 0, 0)),
                  pl.BlockSpec((g, gw, gw), lambda i, j: (0, 0, 0)),
                  pl.BlockSpec((1, d), lambda i, j: (0, 0)),
                  pl.BlockSpec((1, d), lambda i, j: (0, 0)),
                  pl.BlockSpec((1, d), lambda i, j: (0, 0))],
        out_specs=pl.BlockSpec((1, ts, d), lambda i, j: (i, j, 0)),
        out_shape=jax.ShapeDtypeStruct((b, s, d), F32),
        scratch_shapes=[pltpu.VMEM((POOL_HALO + ts, d), F32)] * 3,
        compiler_params=_cparams(2),
        name="pool_layer",
    )(x, m, pool_w.astype(BF16), row(pool_scale), row(ln_g), row(ln_b))


def _dn_in_kernel(x_ref, m_ref, wm_ref, ws_ref, cw_ref, av_ref, dv_ref,
                  q_ref, k_ref, v_ref, z_ref, bg_ref, ext, *, ts, sr):
    s = pl.program_id(1)
    nh, dh = DN_HEADS, DN_HEAD_DIM
    w = nh * dh
    halo = CONV_HALO
    sh, sc = m_ref[0, 0:1, :], m_ref[0, 1:2, :]

    @pl.when(s == 0)
    def _():
        ext[0:halo, :] = jnp.zeros((halo, 3 * w), F32)

    def project(i):
        rows = slice(i * sr, (i + 1) * sr)
        h = (x_ref[0, rows, :] * (1.0 + sc) + sh).astype(BF16)
        proj = _dot(h, wm_ref[...])
        ext[halo + i * sr:halo + (i + 1) * sr, :] = proj[:, :3 * w]
        z_ref[0, rows, :] = proj[:, 3 * w:]
        small = _dot(h, ws_ref[...])
        lane = lax.broadcasted_iota(I32, small.shape, 1)
        beta = jax.nn.sigmoid(small)
        g = -jnp.exp(av_ref[...]) * jax.nn.softplus(small + dv_ref[...])
        bg_ref[0, rows, :] = jnp.where(lane < nh, beta, jnp.where(lane < 2 * nh, g, 0.0))

    def mix(i):
        rows = slice(i * sr, (i + 1) * sr)
        base = halo - (DN_CONV - 1) + i * sr
        conv = cw_ref[0:1, :] * ext[base:base + sr, :]
        for j in range(1, DN_CONV):
            conv = conv + cw_ref[j:j + 1, :] * ext[base + j:base + j + sr, :]
        act = conv * jax.nn.sigmoid(conv)
        for hd in range(nh):
            qh = act[:, hd * dh:(hd + 1) * dh]
            kh = act[:, w + hd * dh:w + (hd + 1) * dh]
            q_ref[0, hd, rows, :] = (qh * lax.rsqrt(jnp.sum(qh * qh, axis=-1, keepdims=True) + RMS_EPS)
                                     * (dh ** -0.5))
            k_ref[0, hd, rows, :] = kh * lax.rsqrt(jnp.sum(kh * kh, axis=-1, keepdims=True) + RMS_EPS)
            v_ref[0, hd, rows, :] = act[:, 2 * w + hd * dh:2 * w + (hd + 1) * dh]

    n_sub = ts // sr
    project(0)
    for i in range(1, n_sub):
        project(i)
        mix(i - 1)
    mix(n_sub - 1)
    ext[0:halo, :] = ext[ts:ts + halo, :]


def _dn_core_kernel(q_ref, k_ref, v_ref, bg_ref, o_ref,
                    st, bb, gcb, gt, gl_s, u_s, wq_s, ai_s, kd_s, *, ts):
    s = pl.program_id(1)
    nh, dh, c = DN_HEADS, DN_HEAD_DIM, DN_CHUNK
    nc = ts // c

    @pl.when(s == 0)
    def _():
        st[...] = jnp.zeros(st.shape, F32)

    bgv = bg_ref[0]
    lane = lax.broadcasted_iota(I32, bgv.shape, 1)
    rowc = lax.broadcasted_iota(I32, bgv.shape, 0) % c
    gc = jnp.where(lane >= nh, bgv, 0.0)
    shift = 1
    while shift < c:
        gc = gc + jnp.where(rowc >= shift, pltpu.roll(gc, shift, 0), 0.0)
        shift *= 2
    gc_t = gc.T
    for hd in range(nh):
        for ci in range(nc):
            gt[hd * nc + ci] = jnp.broadcast_to(gc_t[nh + hd:nh + hd + 1, ci * c:(ci + 1) * c], (SUBLANES, c))
        bb[hd] = jnp.broadcast_to(bgv[:, hd:hd + 1], (ts, LANES))
        gcb[hd] = jnp.broadcast_to(gc[:, nh + hd:nh + hd + 1], (ts, LANES))

    ri = lax.broadcasted_iota(I32, (c, c), 0)
    ci_ = lax.broadcasted_iota(I32, (c, c), 1)
    tril = ri >= ci_
    eye = (ri == ci_).astype(F32)
    off_masks = []
    blk = 1
    while blk < c:
        off_masks.append((ri // blk != ci_ // blk) & (ri // (2 * blk) == ci_ // (2 * blk)) & (ri > ci_))
        blk *= 2

    hg = DN_HEAD_GROUP
    nb = hg * nc

    def group_body(gi, carry):
        heads = pl.ds(gi * hg, hg)
        chunks = lambda a: a.reshape(nb, c, a.shape[-1])
        per_head = lambda a: a.reshape(hg, ts, a.shape[-1])
        q, k, v = chunks(q_ref[0, heads]), chunks(k_ref[0, heads]), chunks(v_ref[0, heads])
        beta, gch = chunks(bb[heads]), chunks(gcb[heads])
        egh = jnp.exp(gch)
        g_last = jnp.broadcast_to(gch[:, c - 1:c, :], gch.shape)
        kb = k * beta
        vb = v * beta
        kd_s[heads] = per_head((k * jnp.exp(g_last - gch)).astype(BF16))
        gl_s[heads] = jnp.broadcast_to(egh[:, c - 1:c, :], (nb, SUBLANES, LANES)).reshape(hg, nc * SUBLANES, LANES)
        kc = k.astype(BF16)
        grow = gt[pl.ds(gi * nb, nb), 0:1, :]
        dec = jnp.where(tril, jnp.exp(jnp.where(tril, gch - grow, 0.0)), 0.0)
        a_kk = _bdot_nt(kb.astype(BF16), kc)
        a_qk = _bdot_nt(q.astype(BF16), kc)
        ai_s[heads] = per_head((a_qk * dec).astype(BF16))
        l_mat = a_kk * dec
        t_inv = eye - jnp.where(off_masks[0], l_mat, 0.0)
        for off in off_masks[1:]:
            tb = t_inv.astype(BF16)
            l_off = jnp.where(off, l_mat, 0.0).astype(BF16)
            t_inv = t_inv - _bdot(tb, _bdot(l_off, tb).astype(BF16))
        rhs = jnp.concatenate([vb, kb * egh], axis=2).astype(BF16)
        uw = _bdot(t_inv.astype(BF16), rhs)
        u_s[heads] = per_head(uw[:, :, :dh])
        wq = jnp.concatenate([uw[:, :, dh:], q * egh], axis=1).astype(BF16)
        wq_s[heads] = wq.reshape(hg, 2 * ts, dh)
        return carry

    lax.fori_loop(0, nh // hg, group_body, 0)

    def chunk_body(ci, carry):
        rows = pl.ds(pl.multiple_of(ci * c, c), c)
        rows2 = pl.ds(pl.multiple_of(ci * 2 * c, 2 * c), 2 * c)
        heads = range(nh)
        states = [st[hd] for hd in heads]
        ws_qs = [_dot(wq_s[hd, rows2, :], states[hd].astype(BF16)) for hd in heads]
        v_nb = [(u_s[hd, rows, :] - ws_qs[hd][:c]).astype(BF16) for hd in heads]
        outs = [ws_qs[hd][c:] + _dot(ai_s[hd, rows, :], v_nb[hd]) for hd in heads]
        new_states = [states[hd] * gl_s[hd, pl.ds(pl.multiple_of(ci * SUBLANES, SUBLANES), 1), :]
                      + _dot_tn(kd_s[hd, rows, :], v_nb[hd]) for hd in heads]
        for hd in heads:
            o_ref[0, hd, rows, :] = outs[hd]
            st[hd] = new_states[hd]
        return carry

    lax.fori_loop(0, nc, chunk_body, 0)


def _dn_out_kernel(o_ref, z_ref, x_ref, m_ref, nw_ref, wo_ref, lg_ref, lb_ref, out_ref, *, alpha):
    nh, dh = DN_HEADS, DN_HEAD_DIM
    z = z_ref[0]
    parts = []
    for hd in range(nh):
        oh = o_ref[0, hd]
        on = oh * lax.rsqrt(jnp.mean(oh * oh, axis=-1, keepdims=True) + RMS_EPS) * nw_ref[...]
        zh = z[:, hd * dh:(hd + 1) * dh]
        parts.append((on * (zh * jax.nn.sigmoid(zh))).astype(BF16))
    y = _dot(jnp.concatenate(parts, axis=1), wo_ref[...])
    x = x_ref[0]
    gt = m_ref[0, 2:3, :]
    out_ref[0] = _layer_norm(alpha * x + (1.0 + gt) * y, lg_ref[...], lb_ref[...])


def _deltanet_layer(x, m, w_in, conv_w, a_log, dt_bias, norm_w, w_out, ln_g, ln_b, alpha, cfg):
    b, s, d = x.shape
    nh, dh = DN_HEADS, DN_HEAD_DIM
    w = nh * dh
    assert w_in.shape == (d, 4 * w + 2 * nh) and conv_w.shape == (DN_CONV, 3 * w) and 2 * nh <= LANES
    w_main = w_in[:, :4 * w].astype(BF16)
    w_small = jnp.pad(w_in[:, 4 * w:], ((0, 0), (0, LANES - 2 * nh))).astype(BF16)
    avec = jnp.zeros((1, LANES), F32).at[0, nh:2 * nh].set(a_log.astype(F32))
    dvec = jnp.zeros((1, LANES), F32).at[0, nh:2 * nh].set(dt_bias.astype(F32))
    row = lambda v: v.reshape(1, -1)
    const2 = lambda i, j: (0, 0)

    ts = cfg["dn_in_ts"]
    assert s % ts == 0 and ts >= CONV_HALO
    head_major = jax.ShapeDtypeStruct((b, nh, s, dh), F32)
    hm_spec = lambda t: pl.BlockSpec((1, nh, t, dh), lambda i, j: (i, 0, j, 0))
    q, k, v, z, bg = pl.pallas_call(
        functools.partial(_dn_in_kernel, ts=ts, sr=min(cfg["dn_in_sub"], ts)),
        grid=(b, s // ts),
        in_specs=[pl.BlockSpec((1, ts, d), lambda i, j: (i, j, 0)),
                  pl.BlockSpec((1, N_MOD, d), lambda i, j: (i, 0, 0)),
                  pl.BlockSpec((d, 4 * w), const2),
                  pl.BlockSpec((d, LANES), const2),
                  pl.BlockSpec((DN_CONV, 3 * w), const2),
                  pl.BlockSpec((1, LANES), const2),
                  pl.BlockSpec((1, LANES), const2)],
        out_specs=[hm_spec(ts), hm_spec(ts), hm_spec(ts),
                   pl.BlockSpec((1, ts, w), lambda i, j: (i, j, 0)),
                   pl.BlockSpec((1, ts, LANES), lambda i, j: (i, j, 0))],
        out_shape=[head_major, head_major, head_major,
                   jax.ShapeDtypeStruct((b, s, w), F32),
                   jax.ShapeDtypeStruct((b, s, LANES), F32)],
        scratch_shapes=[pltpu.VMEM((CONV_HALO + ts, 3 * w), F32)],
        compiler_params=_cparams(2),
        name="deltanet_in",
    )(x, m, w_main, w_small, conv_w, avec, dvec)

    ts = cfg["dn_core_ts"]
    assert s % ts == 0 and ts % DN_CHUNK == 0 and dh == DN_CHUNK
    per_head = pltpu.VMEM((nh, ts, dh), F32)
    o = pl.pallas_call(
        functools.partial(_dn_core_kernel, ts=ts),
        grid=(b, s // ts),
        in_specs=[hm_spec(ts), hm_spec(ts), hm_spec(ts),
                  pl.BlockSpec((1, ts, LANES), lambda i, j: (i, j, 0))],
        out_specs=hm_spec(ts),
        out_shape=head_major,
        scratch_shapes=[pltpu.VMEM((nh, dh, dh), F32),
                        per_head, per_head,
                        pltpu.VMEM((nh * (ts // DN_CHUNK), SUBLANES, DN_CHUNK), F32),
                        pltpu.VMEM((nh, ts // DN_CHUNK * SUBLANES, LANES), F32),
                        per_head,
                        pltpu.VMEM((nh, 2 * ts, dh), BF16),
                        pltpu.VMEM((nh, ts, DN_CHUNK), BF16),
                        pltpu.VMEM((nh, ts, dh), BF16)],
        compiler_params=_cparams(2),
        name="deltanet_core",
    )(q, k, v, bg)

    ts = cfg["dn_out_ts"]
    assert s % ts == 0
    return pl.pallas_call(
        functools.partial(_dn_out_kernel, alpha=alpha),
        grid=(b, s // ts),
        in_specs=[hm_spec(ts),
                  pl.BlockSpec((1, ts, w), lambda i, j: (i, j, 0)),
                  pl.BlockSpec((1, ts, d), lambda i, j: (i, j, 0)),
                  pl.BlockSpec((1, N_MOD, d), lambda i, j: (i, 0, 0)),
                  pl.BlockSpec((1, dh), const2),
                  pl.BlockSpec((w, d), const2),
                  pl.BlockSpec((1, d), const2),
                  pl.BlockSpec((1, d), const2)],
        out_specs=pl.BlockSpec((1, ts, d), lambda i, j: (i, j, 0)),
        out_shape=jax.ShapeDtypeStruct((b, s, d), F32),
        compiler_params=_cparams(2),
        name="deltanet_out",
    )(o, z, x, m, row(norm_w), w_out.astype(BF16), row(ln_g), row(ln_b))


def _pack_row_chunks(v):
    half = v.shape[-1] // 2
    bits = pltpu.bitcast(v.astype(BF16).astype(F32), U32)
    packed = (bits[:, :half] >> 16) | (bits[:, half:] & jnp.uint32(0xFFFF0000))
    return [packed[:, i * LANES:(i + 1) * LANES] for i in range(half // LANES)]


def _unpack_row_chunks(chunks):
    lo = [pltpu.bitcast(c << 16, F32) for c in chunks]
    hi = [pltpu.bitcast(c & jnp.uint32(0xFFFF0000), F32) for c in chunks]
    return jnp.concatenate(lo + hi, axis=1)


def _store_row_chunks(ref, chunks):
    for i, ch in enumerate(chunks):
        ref[i] = ch


def _load_row_chunks(ref, lead=()):
    return [ref[(*lead, i)] for i in range(ref.shape[len(lead)])]


def _sc_mesh():
    return plsc.VectorSubcoreMesh(core_axis_name="c", subcore_axis_name="s")


def _sc_gather_rows(table, idx):
    n = idx.shape[1]
    assert table.shape[1] == LANES and idx.shape[0] == 1 and n % (SC_WINDOW * SC_WORKERS) == 0

    @functools.partial(pl.kernel, out_type=jax.ShapeDtypeStruct((n, LANES), table.dtype), mesh=_sc_mesh(),
                       name="sc_gather_rows")
    def gather(table_hbm, idx_hbm, out_hbm):
        def body(idx_vmem, out_vmem):
            pltpu.sync_copy(table_hbm.at[idx_vmem.at[0]], out_vmem)

        pltpu.emit_pipeline(
            body,
            grid=(n // SC_WINDOW,),
            in_specs=[pl.BlockSpec((1, SC_WINDOW), lambda i: (0, i))],
            out_specs=[pl.BlockSpec((SC_WINDOW, LANES), lambda i: (i, 0))],
            core_axis_name=("c", "s"),
            dimension_semantics=(pltpu.PARALLEL,),
        )(idx_hbm, out_hbm)

    return gather(table, idx)


def _sc_scatter_rows(x, idx_list, n_out):
    n = x.shape[0]
    assert x.shape[1] == LANES and n % (SC_WINDOW * SC_WORKERS) == 0
    assert all(idx.shape == (1, n) for idx in idx_list)

    @functools.partial(pl.kernel, out_type=jax.ShapeDtypeStruct((n_out, LANES), x.dtype), mesh=_sc_mesh(),
                       name="sc_scatter_rows")
    def scatter(x_hbm, *refs):
        idx_hbms, out_hbm = refs[:-1], refs[-1]

        def body(x_vmem, *idx_vmems):
            for idx_vmem in idx_vmems:
                pltpu.sync_copy(x_vmem, out_hbm.at[idx_vmem.at[0]])

        pltpu.emit_pipeline(
            body,
            grid=(n // SC_WINDOW,),
            in_specs=[pl.BlockSpec((SC_WINDOW, LANES), lambda i: (i, 0))]
                     + [pl.BlockSpec((1, SC_WINDOW), lambda i: (0, i))] * len(idx_list),
            out_specs=[],
            core_axis_name=("c", "s"),
            dimension_semantics=(pltpu.PARALLEL,),
        )(x_hbm, *idx_hbms)

    return scatter(x, *idx_list)


def _router_kernel(x_ref, m_ref, rwt_ref, rb_ref, tri_ref,
                   hp_ref, idx_ref, p_ref, rank_ref, cnt_ref, run, *, tt):
    i = pl.program_id(0)

    @pl.when(i == 0)
    def _():
        run[...] = jnp.zeros(run.shape, F32)

    x = x_ref[...]
    d = x.shape[-1]
    sh, sc = m_ref[0, 3:4, :], m_ref[0, 4:5, :]
    h = x * (1.0 + sc) + sh
    _store_row_chunks(hp_ref, _pack_row_chunks(h))

    h_hi = h.astype(BF16)
    h_lo = (h - h_hi.astype(F32)).astype(BF16)
    rw = rwt_ref[...]
    w_hi = rw.astype(BF16)
    w_lo = (rw - w_hi.astype(F32)).astype(BF16)
    ne = rw.shape[0]
    hi_terms = _dot_nt(jnp.concatenate([w_hi, w_lo], axis=0), h_hi)
    logits = hi_terms[:ne] + hi_terms[ne:] + _dot_nt(w_hi, h_lo) + rb_ref[...]
    eio = lax.broadcasted_iota(I32, logits.shape, 0).astype(F32)
    vals, idxs, sels = [], [], []
    for _ in range(TOP_K):
        mx = jnp.max(logits, axis=0, keepdims=True)
        ix = jnp.min(jnp.where(logits == mx, eio, float(ne)), axis=0, keepdims=True)
        sel = eio == ix
        logits = jnp.where(sel, -jnp.inf, logits)
        vals.append(mx)
        idxs.append(ix)
        sels.append(sel)
    exps = [jnp.exp(v - vals[0]) for v in vals]
    den = functools.reduce(lambda a, b_: a + b_, exps)
    chosen = functools.reduce(jnp.logical_or, sels)
    onehot = jnp.where(chosen, 1.0, 0.0)
    before = _dot(onehot.astype(BF16), tri_ref[...]) + run[...]
    ranks = [jnp.sum(jnp.where(sel, before, 0.0), axis=0, keepdims=True) for sel in sels]
    run[...] = run[...] + jnp.sum(onehot, axis=1, keepdims=True)
    idx_ref[...] = jnp.concatenate(idxs, axis=0).astype(I32)
    p_ref[...] = jnp.concatenate([e / den for e in exps]
                                 + [jnp.zeros((p_ref.shape[0] - TOP_K, den.shape[1]), F32)], axis=0)
    rank_ref[...] = jnp.concatenate(ranks, axis=0).astype(I32)
    cnt_ref[...] = jnp.broadcast_to(run[...], cnt_ref.shape)


def _experts_kernel(be_ref, nu_ref, xs_ref, wgu_ref, bgu_ref, wd_ref, bd_ref, y_ref, wgu_b, wd_b):
    j = pl.program_id(0)

    @pl.when(j >= nu_ref[0])
    def _():
        y_ref[...] = jnp.zeros(y_ref.shape, U32)

    @pl.when(j < nu_ref[0])
    def _():
        new_expert = jnp.logical_or(j == 0, be_ref[j] != be_ref[jnp.maximum(j - 1, 0)])

        @pl.when(new_expert)
        def _():
            wgu_b[...] = wgu_ref[0, 0].astype(BF16)
            wd_b[...] = wd_ref[0, 0].astype(BF16)

        xb = _unpack_row_chunks(_load_row_chunks(xs_ref)).astype(BF16)
        gu = _dot(xb, wgu_b[...]) + bgu_ref[0, 0]
        f = gu.shape[1] // 2
        glu = jnp.minimum(gu[:, :f], SWIGLU_LIMIT)
        lin = jnp.clip(gu[:, f:], -SWIGLU_LIMIT, SWIGLU_LIMIT)
        act = glu * jax.nn.sigmoid(SWIGLU_ALPHA * glu) * (lin + 1.0)
        y = _dot(act.astype(BF16), wd_b[...]) + bd_ref[0, 0]
        _store_row_chunks(y_ref, _pack_row_chunks(y))


def _combine_kernel(ya_ref, x_ref, m_ref, p_ref, lg_ref, lb_ref, o_ref, *, alpha):
    p_rows = p_ref[...]
    eye = (lax.broadcasted_iota(I32, (p_rows.shape[0], LANES), 0)
           == lax.broadcasted_iota(I32, (p_rows.shape[0], LANES), 1)).astype(F32)
    p = lax.dot_general(p_rows, eye, (((0,), (0,)), ((), ())), preferred_element_type=F32,
                        precision=HIGHEST)
    y = None
    for kk in range(TOP_K):
        yk = p[:, kk:kk + 1] * _unpack_row_chunks(_load_row_chunks(ya_ref, (kk,)))
        y = yk if y is None else y + yk
    gt = m_ref[0, 5:6, :]
    o_ref[...] = _layer_norm(alpha * x_ref[...] + (1.0 + gt) * y, lg_ref[...], lb_ref[...])


def _moe_layer(x, m, layer, router_w, router_b, w_gu, b_gu, w_down, b_down, ln_g, ln_b, alpha, cfg):
    b, s, d = x.shape
    n_tok = b * s
    ne = router_w.shape[-1]
    f2 = w_gu.shape[-1]
    ff = f2 // 2
    assert d % (2 * LANES) == 0 and w_down.shape[-2:] == (ff, d)
    rc = d // (2 * LANES)
    xf = x.reshape(n_tok, d)
    const2 = lambda i: (0, 0)

    tt = cfg["router_tt"]
    assert s % tt == 0
    tri = jnp.triu(jnp.ones((tt, tt), BF16), k=1)
    hp, idx, probs, rank, cnt = pl.pallas_call(
        functools.partial(_router_kernel, tt=tt),
        grid=(n_tok // tt,),
        in_specs=[pl.BlockSpec((tt, d), lambda i: (i, 0)),
                  pl.BlockSpec((1, N_MOD, d), lambda i: (i * tt // s, 0, 0)),
                  pl.BlockSpec((ne, d), const2),
                  pl.BlockSpec((ne, 1), const2),
                  pl.BlockSpec((tt, tt), const2)],
        out_specs=[pl.BlockSpec((rc, tt, LANES), lambda i: (0, i, 0)),
                   pl.BlockSpec((TOP_K, tt), lambda i: (0, i)),
                   pl.BlockSpec((SUBLANES, tt), lambda i: (0, i)),
                   pl.BlockSpec((TOP_K, tt), lambda i: (0, i)),
                   pl.BlockSpec((ne, LANES), const2)],
        out_shape=[jax.ShapeDtypeStruct((rc, n_tok, LANES), U32),
                   jax.ShapeDtypeStruct((TOP_K, n_tok), I32),
                   jax.ShapeDtypeStruct((SUBLANES, n_tok), F32),
                   jax.ShapeDtypeStruct((TOP_K, n_tok), I32),
                   jax.ShapeDtypeStruct((ne, LANES), F32)],
        scratch_shapes=[pltpu.VMEM((ne, 1), F32)],
        compiler_params=_cparams(1),
        name="moe_router",
    )(xf, m, router_w[layer].T, router_b[layer].reshape(ne, 1), tri)

    bm = cfg["expert_bm"]
    cap = n_tok * TOP_K + ne * bm
    nb = cap // bm
    counts = cnt[:, 0].astype(I32)
    padded = (counts + bm - 1) // bm * bm
    pad_end = jnp.cumsum(padded)
    pad_start = pad_end - padded
    eids = jnp.arange(ne, dtype=I32)[:, None, None]
    dest = rank + jnp.sum(jnp.where(idx[None] == eids, pad_start[:, None, None], 0), axis=0)
    block_start = jnp.arange(nb, dtype=I32) * bm
    block_expert = jnp.minimum(jnp.sum(block_start[None, :] >= pad_end[:, None], axis=0), ne - 1).astype(I32)
    n_used = (pad_end[-1:] // bm).astype(I32)

    plane_offset = (jnp.arange(rc * n_tok, dtype=I32) // n_tok * cap)[None, :]
    slot_rows = [jnp.tile(dest[kk:kk + 1], (1, rc)) + plane_offset for kk in range(TOP_K)]

    xs = _sc_scatter_rows(hp.reshape(rc * n_tok, LANES), slot_rows, rc * cap).reshape(rc, cap, LANES)

    last = lambda j, be, nu: jnp.minimum(j, nu[0] - 1)
    y_rows = pl.pallas_call(
        _experts_kernel,
        grid_spec=pltpu.PrefetchScalarGridSpec(
            num_scalar_prefetch=2,
            grid=(nb,),
            in_specs=[pl.BlockSpec((rc, bm, LANES), lambda j, be, nu: (0, last(j, be, nu), 0)),
                      pl.BlockSpec((1, 1, d, f2), lambda j, be, nu: (layer, be[last(j, be, nu)], 0, 0)),
                      pl.BlockSpec((1, 1, 1, f2), lambda j, be, nu: (layer, be[last(j, be, nu)], 0, 0)),
                      pl.BlockSpec((1, 1, ff, d), lambda j, be, nu: (layer, be[last(j, be, nu)], 0, 0)),
                      pl.BlockSpec((1, 1, 1, d), lambda j, be, nu: (layer, be[last(j, be, nu)], 0, 0))],
            out_specs=pl.BlockSpec((rc, bm, LANES), lambda j, be, nu: (0, j, 0)),
            scratch_shapes=[pltpu.VMEM((d, f2), BF16), pltpu.VMEM((ff, d), BF16)]),
        out_shape=jax.ShapeDtypeStruct((rc, cap, LANES), U32),
        compiler_params=_cparams(1),
        name="moe_experts",
    )(block_expert, n_used, xs, w_gu, b_gu.reshape(*b_gu.shape[:2], 1, f2),
      w_down, b_down.reshape(*b_down.shape[:2], 1, d))

    y_assign = _sc_gather_rows(y_rows.reshape(rc * cap, LANES), jnp.concatenate(slot_rows, axis=1))
    y_assign = y_assign.reshape(TOP_K, rc, n_tok, LANES)

    tt = cfg["combine_tt"]
    assert s % tt == 0
    n_tiles = n_tok // tt
    out = pl.pallas_call(
        functools.partial(_combine_kernel, alpha=alpha),
        grid=(n_tiles,),
        in_specs=[pl.BlockSpec((TOP_K, rc, tt, LANES), lambda i: (0, 0, i, 0)),
                  pl.BlockSpec((tt, d), lambda i: (i, 0)),
                  pl.BlockSpec((1, N_MOD, d), lambda i: (i * tt // s, 0, 0)),
                  pl.BlockSpec((SUBLANES, tt), lambda i: (0, i)),
                  pl.BlockSpec((1, d), const2),
                  pl.BlockSpec((1, d), const2)],
        out_specs=pl.BlockSpec((tt, d), lambda i: (i, 0)),
        out_shape=jax.ShapeDtypeStruct((n_tok, d), F32),
        compiler_params=_cparams(1),
        name="moe_combine",
    )(y_assign, xf, m, probs, ln_g.reshape(1, d), ln_b.reshape(1, d))
    return out.reshape(b, s, d)


def kernel(x, c, ada_w, ada_b, ln_g, ln_b, pool_w, pool_scale, dn_w_in, dn_conv_w, dn_a_log, dn_dt_bias,
           dn_norm_w, dn_w_out, router_w, router_b, exp_w_gu, exp_b_gu, exp_w_down, exp_b_down):
    b, s, d = x.shape
    depth = ada_w.shape[0]
    alpha = (2 * depth) ** 0.25
    cfg = _tile_config(s, b * s)
    mods = _mods(c, ada_w, ada_b)
    n_mixers = 2
    for i in range(depth):
        m = mods[i]
        j = i // n_mixers
        if i % n_mixers == 0:
            x = _pool_layer(x, m, pool_w[j], pool_scale[j], ln_g[i, 0], ln_b[i, 0], alpha, cfg)
        else:
            x = _deltanet_layer(x, m, dn_w_in[j], dn_conv_w[j], dn_a_log[j], dn_dt_bias[j], dn_norm_w[j],
                                dn_w_out[j], ln_g[i, 0], ln_b[i, 0], alpha, cfg)
        x = _moe_layer(x, m, i, router_w, router_b, exp_w_gu, exp_b_gu, exp_w_down, exp_b_down,
                       ln_g[i, 1], ln_b[i, 1], alpha, cfg)
    return x
```

```python
import functools

import jax
import jax.numpy as jnp
from jax import lax
from jax.experimental import pallas as pl
from jax.experimental.pallas import tpu as pltpu
from jax.experimental.pallas import tpu_sc as plsc

F32 = jnp.float32
BF16 = jnp.bfloat16
I32 = jnp.int32
U32 = jnp.uint32

N_MOD = 6
POOL_WINDOWS = (2, 4, 8, 16)
DN_HEADS = 8
DN_HEAD_DIM = 128
DN_CONV = 4
TOP_K = 4
SWIGLU_LIMIT = 7.0
SWIGLU_ALPHA = 1.702
LN_EPS = 1e-5
RMS_EPS = 1e-6

LANES = 128
SUBLANES = 8
VMEM_LIMIT_BYTES = 56 * 1024 * 1024
SC_WORKERS = 32
SC_WINDOW = 128

DN_CHUNK = LANES
DN_HEAD_GROUP = 8
POOL_HALO = 32
CONV_HALO = SUBLANES

HIGHEST = lax.Precision.HIGHEST


def _tile_config(seq, n_tok):
    return dict(
        pool_ts=min(512, seq),
        dn_in_ts=min(512, seq),
        dn_in_sub=128,
        dn_core_ts=min(512, seq),
        dn_out_ts=min(512, seq),
        router_tt=min(512, seq),
        expert_bm=512,
        combine_tt=min(256, seq),
        combine_groups=4,
    )


def _cparams(n_axes):
    return pltpu.CompilerParams(dimension_semantics=("arbitrary",) * n_axes,
                                vmem_limit_bytes=VMEM_LIMIT_BYTES)


def _layer_norm(v, g, b):
    mu = jnp.mean(v, axis=-1, keepdims=True)
    d = v - mu
    var = jnp.mean(d * d, axis=-1, keepdims=True)
    return d * lax.rsqrt(var + LN_EPS) * g + b


def _dot(a, b):
    return jnp.dot(a, b, preferred_element_type=F32)


def _dot_nt(a, b):
    return lax.dot_general(a, b, (((1,), (1,)), ((), ())), preferred_element_type=F32)


def _dot_tn(a, b):
    return lax.dot_general(a, b, (((0,), (0,)), ((), ())), preferred_element_type=F32)


def _bdot(a, b):
    return lax.dot_general(a, b, (((2,), (1,)), ((0,), (0,))), preferred_element_type=F32)


def _bdot_nt(a, b):
    return lax.dot_general(a, b, (((2,), (2,)), ((0,), (0,))), preferred_element_type=F32)


def _mods_kernel(c_ref, w_ref, b_ref, o_ref):
    c = c_ref[...]
    c_act = c * jax.nn.sigmoid(c)
    o_ref[0] = jnp.dot(c_act, w_ref[0], preferred_element_type=F32, precision=HIGHEST) + b_ref[0]


def _mods(c, ada_w, ada_b):
    depth, d, n = ada_w.shape
    b = c.shape[0]
    tn = 2048 if n % 2048 == 0 else n
    out = pl.pallas_call(
        _mods_kernel,
        grid=(depth, n // tn),
        in_specs=[pl.BlockSpec((b, d), lambda i, j: (0, 0)),
                  pl.BlockSpec((1, d, tn), lambda i, j: (i, 0, j)),
                  pl.BlockSpec((1, 1, tn), lambda i, j: (i, 0, j))],
        out_specs=pl.BlockSpec((1, b, tn), lambda i, j: (i, 0, j)),
        out_shape=jax.ShapeDtypeStruct((depth, b, n), F32),
        compiler_params=_cparams(2),
        name="adaln_mods",
    )(c, ada_w, ada_b.reshape(depth, 1, n))
    return out.reshape(depth, b, N_MOD, d)


def _pool_kernel(x_ref, m_ref, pw_ref, ps_ref, lg_ref, lb_ref, o_ref, e1, ea, eb, *, ts, alpha):
    s = pl.program_id(1)
    d = x_ref.shape[-1]
    gw = d // len(POOL_WINDOWS)
    halo = POOL_HALO
    rows = halo + ts
    x = x_ref[0]
    sh, sc, gt = m_ref[0, 0:1, :], m_ref[0, 1:2, :], m_ref[0, 2:3, :]
    h = x * (1.0 + sc) + sh

    @pl.when(s == 0)
    def _():
        e1[0:halo, :] = jnp.zeros((halo, d), F32)

    e1[halo:rows, :] = h
    ea[8:rows, :] = e1[8:rows, :] + e1[7:rows - 1, :]
    eb[16:rows, gw:] = ea[16:rows, gw:] + ea[14:rows - 2, gw:]
    ea[24:rows, 2 * gw:] = eb[24:rows, 2 * gw:] + eb[20:rows - 4, 2 * gw:]
    eb[32:rows, 3 * gw:] = ea[32:rows, 3 * gw:] + ea[24:rows - 8, 3 * gw:]

    pos = s * ts + lax.broadcasted_iota(I32, (ts, 1), 0)
    outs = []
    for g, win in enumerate(POOL_WINDOWS):
        src = ea if g % 2 == 0 else eb
        cols = slice(g * gw, (g + 1) * gw)
        cnt = jnp.minimum(pos + 1, win).astype(F32)
        pooled = src[halo:rows, cols] / cnt - h[:, cols]
        outs.append(_dot(pooled.astype(BF16), pw_ref[g]))
    y = jnp.concatenate(outs, axis=1) * ps_ref[...]
    o_ref[0] = _layer_norm(alpha * x + (1.0 + gt) * y, lg_ref[...], lb_ref[...])
    e1[0:halo, :] = e1[ts:rows, :]


def _pool_layer(x, m, pool_w, pool_scale, ln_g, ln_b, alpha, cfg):
    b, s, d = x.shape
    ts = cfg["pool_ts"]
    g, gw, _ = pool_w.shape
    assert POOL_WINDOWS == (2, 4, 8, 16) and g == len(POOL_WINDOWS) and s % ts == 0 and ts >= POOL_HALO
    row = lambda v: v.reshape(1, d)
    return pl.pallas_call(
        functools.partial(_pool_kernel, ts=ts, alpha=alpha),
        grid=(b, s // ts),
        in_specs=[pl.BlockSpec((1, ts, d), lambda i, j: (i, j, 0)),
                  pl.BlockSpec((1, N_MOD, d), lambda i, j: (i, 0, 0)),
                  pl.BlockSpec((g, gw, gw), lambda i, j: (0, 0, 0)),
                  pl.BlockSpec((1, d), lambda i, j: (0, 0)),
                  pl.BlockSpec((1, d), lambda i, j: (0, 0)),
                  pl.BlockSpec((1, d), lambda i, j: (0, 0))],
        out_specs=pl.BlockSpec((1, ts, d), lambda i, j: (i, j, 0)),
        out_shape=jax.ShapeDtypeStruct((b, s, d), F32),
        scratch_shapes=[pltpu.VMEM((POOL_HALO + ts, d), F32)] * 3,
        compiler_params=_cparams(2),
        name="pool_layer",
    )(x, m, pool_w.astype(BF16), row(pool_scale), row(ln_g), row(ln_b))


def _dn_in_kernel(x_ref, m_ref, wm_ref, ws_ref, cw_ref, av_ref, dv_ref,
                  q_ref, k_ref, v_ref, z_ref, bg_ref, ext, *, ts, sr):
    s = pl.program_id(1)
    nh, dh = DN_HEADS, DN_HEAD_DIM
    w = nh * dh
    halo = CONV_HALO
    sh, sc = m_ref[0, 0:1, :], m_ref[0, 1:2, :]

    @pl.when(s == 0)
    def _():
        ext[0:halo, :] = jnp.zeros((halo, 3 * w), F32)

    def project(i):
        rows = slice(i * sr, (i + 1) * sr)
        h = (x_ref[0, rows, :] * (1.0 + sc) + sh).astype(BF16)
        proj = _dot(h, wm_ref[...])
        ext[halo + i * sr:halo + (i + 1) * sr, :] = proj[:, :3 * w]
        z_ref[0, rows, :] = proj[:, 3 * w:]
        small = _dot(h, ws_ref[...])
        lane = lax.broadcasted_iota(I32, small.shape, 1)
        beta = jax.nn.sigmoid(small)
        g = -jnp.exp(av_ref[...]) * jax.nn.softplus(small + dv_ref[...])
        bg_ref[0, rows, :] = jnp.where(lane < nh, beta, jnp.where(lane < 2 * nh, g, 0.0))

    def mix(i):
        rows = slice(i * sr, (i + 1) * sr)
        base = halo - (DN_CONV - 1) + i * sr
        conv = cw_ref[0:1, :] * ext[base:base + sr, :]
        for j in range(1, DN_CONV):
            conv = conv + cw_ref[j:j + 1, :] * ext[base + j:base + j + sr, :]
        act = conv * jax.nn.sigmoid(conv)
        for hd in range(nh):
            qh = act[:, hd * dh:(hd + 1) * dh]
            kh = act[:, w + hd * dh:w + (hd + 1) * dh]
            q_ref[0, hd, rows, :] = (qh * lax.rsqrt(jnp.sum(qh * qh, axis=-1, keepdims=True) + RMS_EPS)
                                     * (dh ** -0.5))
            k_ref[0, hd, rows, :] = kh * lax.rsqrt(jnp.sum(kh * kh, axis=-1, keepdims=True) + RMS_EPS)
            v_ref[0, hd, rows, :] = act[:, 2 * w + hd * dh:2 * w + (hd + 1) * dh]

    n_sub = ts // sr
    project(0)
    for i in range(1, n_sub):
        project(i)
        mix(i - 1)
    mix(n_sub - 1)
    ext[0:halo, :] = ext[ts:ts + halo, :]


def _dn_core_kernel(q_ref, k_ref, v_ref, bg_ref, o_ref,
                    st, bb, gcb, gt, gl_s, u_s, wq_s, ai_s, kd_s, *, ts):
    s = pl.program_id(1)
    nh, dh, c = DN_HEADS, DN_HEAD_DIM, DN_CHUNK
    nc = ts // c

    @pl.when(s == 0)
    def _():
        st[...] = jnp.zeros(st.shape, F32)

    bgv = bg_ref[0]
    lane = lax.broadcasted_iota(I32, bgv.shape, 1)
    rowc = lax.broadcasted_iota(I32, bgv.shape, 0) % c
    gc = jnp.where(lane >= nh, bgv, 0.0)
    shift = 1
    while shift < c:
        gc = gc + jnp.where(rowc >= shift, pltpu.roll(gc, shift, 0), 0.0)
        shift *= 2
    gc_t = gc.T
    for hd in range(nh):
        for ci in range(nc):
            gt[hd * nc + ci] = jnp.broadcast_to(gc_t[nh + hd:nh + hd + 1, ci * c:(ci + 1) * c], (SUBLANES, c))
        bb[hd] = jnp.broadcast_to(bgv[:, hd:hd + 1], (ts, LANES))
        gcb[hd] = jnp.broadcast_to(gc[:, nh + hd:nh + hd + 1], (ts, LANES))

    ri = lax.broadcasted_iota(I32, (c, c), 0)
    ci_ = lax.broadcasted_iota(I32, (c, c), 1)
    tril = ri >= ci_
    eye = (ri == ci_).astype(F32)
    off_masks = []
    blk = 1
    while blk < c:
        off_masks.append((ri // blk != ci_ // blk) & (ri // (2 * blk) == ci_ // (2 * blk)) & (ri > ci_))
        blk *= 2

    hg = DN_HEAD_GROUP
    nb = hg * nc
    n_pairs = nb // 2
    two = lambda msk: jnp.concatenate([msk, msk], axis=1)
    tril2, eye2, off2 = two(tril), two(eye), [two(msk) for msk in off_masks]

    def pair(a):
        a = a.reshape(n_pairs, 2, a.shape[1], a.shape[2])
        return jnp.concatenate([a[:, 0], a[:, 1]], axis=-1)

    def unpair(a):
        w = a.shape[-1] // 2
        return jnp.stack([a[..., :w], a[..., w:]], axis=1).reshape(nb, a.shape[1], w)

    def block_diag(a):
        left = lax.broadcasted_iota(I32, a.shape[1:], 1) < a.shape[-1] // 2
        zero = jnp.zeros_like(a)
        return jnp.concatenate([jnp.where(left, a, zero), jnp.where(left, zero, a)], axis=1)

    def group_body(gi, carry):
        heads = pl.ds(gi * hg, hg)
        chunks = lambda a: a.reshape(nb, c, a.shape[-1])
        per_head = lambda a: a.reshape(hg, ts, a.shape[-1])
        q, k, v = chunks(q_ref[0, heads]), chunks(k_ref[0, heads]), chunks(v_ref[0, heads])
        beta, gch = chunks(bb[heads]), chunks(gcb[heads])
        egh = jnp.exp(gch)
        g_last = jnp.broadcast_to(gch[:, c - 1:c, :], gch.shape)
        kb = k * beta
        vb = v * beta
        kd_s[heads] = per_head((k * jnp.exp(g_last - gch)).astype(BF16))
        gl_s[heads] = jnp.broadcast_to(egh[:, c - 1:c, :], (nb, SUBLANES, LANES)).reshape(hg, nc * SUBLANES, LANES)
        grow = gt[pl.ds(gi * nb, nb), 0:1, :]
        dec = jnp.where(tril2, jnp.exp(jnp.where(tril2, pair(gch) - pair(grow), 0.0)), 0.0)
        kq = jnp.concatenate([pair(kb), pair(q)], axis=1).astype(BF16)
        a_all = _bdot_nt(kq, block_diag(pair(k).astype(BF16)))
        ai_s[heads] = per_head(unpair((a_all[:, c:] * dec).astype(BF16)))
        l_mat = a_all[:, :c] * dec
        t_inv = eye2 - jnp.where(off2[0], l_mat, 0.0)
        for off in off2[1:]:
            tb = t_inv.astype(BF16)
            l_off = jnp.where(off, l_mat, 0.0).astype(BF16)
            t_inv = t_inv - _bdot(tb, block_diag(_bdot(l_off, block_diag(tb)).astype(BF16)))
        rhs = jnp.concatenate([vb, kb * egh], axis=2).astype(BF16)
        uw = unpair(_bdot(t_inv.astype(BF16), block_diag(pair(rhs))))
        u_s[heads] = per_head(uw[:, :, :dh])
        wq = jnp.concatenate([uw[:, :, dh:], q * egh], axis=1).astype(BF16)
        wq_s[heads] = wq.reshape(hg, 2 * ts, dh)
        return carry

    lax.fori_loop(0, nh // hg, group_body, 0)

    def chunk_body(ci, carry):
        rows = pl.ds(pl.multiple_of(ci * c, c), c)
        rows2 = pl.ds(pl.multiple_of(ci * 2 * c, 2 * c), 2 * c)
        heads = range(nh)
        states = [st[hd] for hd in heads]
        ws_qs = [_dot(wq_s[hd, rows2, :], states[hd].astype(BF16)) for hd in heads]
        v_nb = [(u_s[hd, rows, :] - ws_qs[hd][:c]).astype(BF16) for hd in heads]
        outs = [ws_qs[hd][c:] + _dot(ai_s[hd, rows, :], v_nb[hd]) for hd in heads]
        new_states = [states[hd] * gl_s[hd, pl.ds(pl.multiple_of(ci * SUBLANES, SUBLANES), 1), :]
                      + _dot_tn(kd_s[hd, rows, :], v_nb[hd]) for hd in heads]
        for hd in heads:
            o_ref[0, hd, rows, :] = outs[hd]
            st[hd] = new_states[hd]
        return carry

    lax.fori_loop(0, nc, chunk_body, 0)


def _dn_out_kernel(o_ref, z_ref, x_ref, m_ref, nw_ref, wo_ref, lg_ref, lb_ref, out_ref, *, alpha):
    nh, dh = DN_HEADS, DN_HEAD_DIM
    z = z_ref[0]
    parts = []
    for hd in range(nh):
        oh = o_ref[0, hd]
        on = oh * lax.rsqrt(jnp.mean(oh * oh, axis=-1, keepdims=True) + RMS_EPS) * nw_ref[...]
        zh = z[:, hd * dh:(hd + 1) * dh]
        parts.append((on * (zh * jax.nn.sigmoid(zh))).astype(BF16))
    y = _dot(jnp.concatenate(parts, axis=1), wo_ref[...])
    x = x_ref[0]
    gt = m_ref[0, 2:3, :]
    out_ref[0] = _layer_norm(alpha * x + (1.0 + gt) * y, lg_ref[...], lb_ref[...])


def _deltanet_layer(x, m, w_in, conv_w, a_log, dt_bias, norm_w, w_out, ln_g, ln_b, alpha, cfg):
    b, s, d = x.shape
    nh, dh = DN_HEADS, DN_HEAD_DIM
    w = nh * dh
    assert w_in.shape == (d, 4 * w + 2 * nh) and conv_w.shape == (DN_CONV, 3 * w) and 2 * nh <= LANES
    w_main = w_in[:, :4 * w].astype(BF16)
    w_small = jnp.pad(w_in[:, 4 * w:], ((0, 0), (0, LANES - 2 * nh))).astype(BF16)
    avec = jnp.zeros((1, LANES), F32).at[0, nh:2 * nh].set(a_log.astype(F32))
    dvec = jnp.zeros((1, LANES), F32).at[0, nh:2 * nh].set(dt_bias.astype(F32))
    row = lambda v: v.reshape(1, -1)
    const2 = lambda i, j: (0, 0)

    ts = cfg["dn_in_ts"]
    assert s % ts == 0 and ts >= CONV_HALO
    head_major = jax.ShapeDtypeStruct((b, nh, s, dh), F32)
    hm_spec = lambda t: pl.BlockSpec((1, nh, t, dh), lambda i, j: (i, 0, j, 0))
    q, k, v, z, bg = pl.pallas_call(
        functools.partial(_dn_in_kernel, ts=ts, sr=min(cfg["dn_in_sub"], ts)),
        grid=(b, s // ts),
        in_specs=[pl.BlockSpec((1, ts, d), lambda i, j: (i, j, 0)),
                  pl.BlockSpec((1, N_MOD, d), lambda i, j: (i, 0, 0)),
                  pl.BlockSpec((d, 4 * w), const2),
                  pl.BlockSpec((d, LANES), const2),
                  pl.BlockSpec((DN_CONV, 3 * w), const2),
                  pl.BlockSpec((1, LANES), const2),
                  pl.BlockSpec((1, LANES), const2)],
        out_specs=[hm_spec(ts), hm_spec(ts), hm_spec(ts),
                   pl.BlockSpec((1, ts, w), lambda i, j: (i, j, 0)),
                   pl.BlockSpec((1, ts, LANES), lambda i, j: (i, j, 0))],
        out_shape=[head_major, head_major, head_major,
                   jax.ShapeDtypeStruct((b, s, w), F32),
                   jax.ShapeDtypeStruct((b, s, LANES), F32)],
        scratch_shapes=[pltpu.VMEM((CONV_HALO + ts, 3 * w), F32)],
        compiler_params=_cparams(2),
        name="deltanet_in",
    )(x, m, w_main, w_small, conv_w, avec, dvec)

    ts = cfg["dn_core_ts"]
    assert s % ts == 0 and ts % DN_CHUNK == 0 and dh == DN_CHUNK
    per_head = pltpu.VMEM((nh, ts, dh), F32)
    o = pl.pallas_call(
        functools.partial(_dn_core_kernel, ts=ts),
        grid=(b, s // ts),
        in_specs=[hm_spec(ts), hm_spec(ts), hm_spec(ts),
                  pl.BlockSpec((1, ts, LANES), lambda i, j: (i, j, 0))],
        out_specs=hm_spec(ts),
        out_shape=head_major,
        scratch_shapes=[pltpu.VMEM((nh, dh, dh), F32),
                        per_head, per_head,
                        pltpu.VMEM((nh * (ts // DN_CHUNK), SUBLANES, DN_CHUNK), F32),
                        pltpu.VMEM((nh, ts // DN_CHUNK * SUBLANES, LANES), F32),
                        per_head,
                        pltpu.VMEM((nh, 2 * ts, dh), BF16),
                        pltpu.VMEM((nh, ts, DN_CHUNK), BF16),
                        pltpu.VMEM((nh, ts, dh), BF16)],
        compiler_params=_cparams(2),
        name="deltanet_core",
    )(q, k, v, bg)

    ts = cfg["dn_out_ts"]
    assert s % ts == 0
    return pl.pallas_call(
        functools.partial(_dn_out_kernel, alpha=alpha),
        grid=(b, s // ts),
        in_specs=[hm_spec(ts),
                  pl.BlockSpec((1, ts, w), lambda i, j: (i, j, 0)),
                  pl.BlockSpec((1, ts, d), lambda i, j: (i, j, 0)),
                  pl.BlockSpec((1, N_MOD, d), lambda i, j: (i, 0, 0)),
                  pl.BlockSpec((1, dh), const2),
                  pl.BlockSpec((w, d), const2),
                  pl.BlockSpec((1, d), const2),
                  pl.BlockSpec((1, d), const2)],
        out_specs=pl.BlockSpec((1, ts, d), lambda i, j: (i, j, 0)),
        out_shape=jax.ShapeDtypeStruct((b, s, d), F32),
        compiler_params=_cparams(2),
        name="deltanet_out",
    )(o, z, x, m, row(norm_w), w_out.astype(BF16), row(ln_g), row(ln_b))


def _pack_row_chunks(v):
    half = v.shape[-1] // 2
    bits = pltpu.bitcast(v.astype(BF16).astype(F32), U32)
    packed = (bits[:, :half] >> 16) | (bits[:, half:] & jnp.uint32(0xFFFF0000))
    return [packed[:, i * LANES:(i + 1) * LANES] for i in range(half // LANES)]


def _unpack_row_chunks(chunks):
    lo = [pltpu.bitcast(c << 16, F32) for c in chunks]
    hi = [pltpu.bitcast(c & jnp.uint32(0xFFFF0000), F32) for c in chunks]
    return jnp.concatenate(lo + hi, axis=1)


def _store_row_chunks(ref, chunks):
    for i, ch in enumerate(chunks):
        ref[i] = ch


def _load_row_chunks(ref, lead=()):
    return [ref[(*lead, i)] for i in range(ref.shape[len(lead)])]


def _sc_mesh():
    return plsc.VectorSubcoreMesh(core_axis_name="c", subcore_axis_name="s")


def _sc_gather_rows(table, idx):
    n = idx.shape[1]
    assert table.shape[1] == LANES and idx.shape[0] == 1 and n % (SC_WINDOW * SC_WORKERS) == 0

    @functools.partial(pl.kernel, out_type=jax.ShapeDtypeStruct((n, LANES), table.dtype), mesh=_sc_mesh(),
                       name="sc_gather_rows")
    def gather(table_hbm, idx_hbm, out_hbm):
        def body(idx_vmem, out_vmem):
            pltpu.sync_copy(table_hbm.at[idx_vmem.at[0]], out_vmem)

        pltpu.emit_pipeline(
            body,
            grid=(n // SC_WINDOW,),
            in_specs=[pl.BlockSpec((1, SC_WINDOW), lambda i: (0, i))],
            out_specs=[pl.BlockSpec((SC_WINDOW, LANES), lambda i: (i, 0))],
            core_axis_name=("c", "s"),
            dimension_semantics=(pltpu.PARALLEL,),
        )(idx_hbm, out_hbm)

    return gather(table, idx)


def _sc_scatter_rows(x, idx_list, n_out):
    n = x.shape[0]
    assert x.shape[1] == LANES and n % (SC_WINDOW * SC_WORKERS) == 0
    assert all(idx.shape == (1, n) for idx in idx_list)

    @functools.partial(pl.kernel, out_type=jax.ShapeDtypeStruct((n_out, LANES), x.dtype), mesh=_sc_mesh(),
                       name="sc_scatter_rows")
    def scatter(x_hbm, *refs):
        idx_hbms, out_hbm = refs[:-1], refs[-1]

        def body(x_vmem, *idx_vmems):
            for idx_vmem in idx_vmems:
                pltpu.sync_copy(x_vmem, out_hbm.at[idx_vmem.at[0]])

        pltpu.emit_pipeline(
            body,
            grid=(n // SC_WINDOW,),
            in_specs=[pl.BlockSpec((SC_WINDOW, LANES), lambda i: (i, 0))]
                     + [pl.BlockSpec((1, SC_WINDOW), lambda i: (0, i))] * len(idx_list),
            out_specs=[],
            core_axis_name=("c", "s"),
            dimension_semantics=(pltpu.PARALLEL,),
        )(x_hbm, *idx_hbms)

    return scatter(x, *idx_list)


def _router_kernel(x_ref, m_ref, rwt_ref, rb_ref, tri_ref,
                   hp_ref, idx_ref, p_ref, rank_ref, cnt_ref, run, *, tt):
    i = pl.program_id(0)

    @pl.when(i == 0)
    def _():
        run[...] = jnp.zeros(run.shape, F32)

    x = x_ref[...]
    d = x.shape[-1]
    sh, sc = m_ref[0, 3:4, :], m_ref[0, 4:5, :]
    h = x * (1.0 + sc) + sh
    _store_row_chunks(hp_ref, _pack_row_chunks(h))

    h_hi = h.astype(BF16)
    h_lo = (h - h_hi.astype(F32)).astype(BF16)
    rw = rwt_ref[...]
    w_hi = rw.astype(BF16)
    w_lo = (rw - w_hi.astype(F32)).astype(BF16)
    ne = rw.shape[0]
    hi_terms = _dot_nt(jnp.concatenate([w_hi, w_lo], axis=0), h_hi)
    logits = hi_terms[:ne] + hi_terms[ne:] + _dot_nt(w_hi, h_lo) + rb_ref[...]
    eio = lax.broadcasted_iota(I32, logits.shape, 0).astype(F32)
    vals, idxs, sels = [], [], []
    for _ in range(TOP_K):
        mx = jnp.max(logits, axis=0, keepdims=True)
        ix = jnp.min(jnp.where(logits == mx, eio, float(ne)), axis=0, keepdims=True)
        sel = eio == ix
        logits = jnp.where(sel, -jnp.inf, logits)
        vals.append(mx)
        idxs.append(ix)
        sels.append(sel)
    exps = [jnp.exp(v - vals[0]) for v in vals]
    den = functools.reduce(lambda a, b_: a + b_, exps)
    chosen = functools.reduce(jnp.logical_or, sels)
    onehot = jnp.where(chosen, 1.0, 0.0)
    before = _dot(onehot.astype(BF16), tri_ref[...]) + run[...]
    ranks = [jnp.sum(jnp.where(sel, before, 0.0), axis=0, keepdims=True) for sel in sels]
    run[...] = run[...] + jnp.sum(onehot, axis=1, keepdims=True)
    idx_ref[...] = jnp.concatenate(idxs, axis=0).astype(I32)
    p_ref[...] = jnp.concatenate([e / den for e in exps]
                                 + [jnp.zeros((p_ref.shape[0] - TOP_K, den.shape[1]), F32)], axis=0)
    rank_ref[...] = jnp.concatenate(ranks, axis=0).astype(I32)
    cnt_ref[...] = jnp.broadcast_to(run[...], cnt_ref.shape)


def _experts_kernel(be_ref, nu_ref, xs_ref, wgu_ref, bgu_ref, wd_ref, bd_ref, y_ref, wgu_b, wd_b):
    j = pl.program_id(0)

    @pl.when(j >= nu_ref[0])
    def _():
        y_ref[...] = jnp.zeros(y_ref.shape, U32)

    @pl.when(j < nu_ref[0])
    def _():
        new_expert = jnp.logical_or(j == 0, be_ref[j] != be_ref[jnp.maximum(j - 1, 0)])

        @pl.when(new_expert)
        def _():
            wgu_b[...] = wgu_ref[0, 0].astype(BF16)
            wd_b[...] = wd_ref[0, 0].astype(BF16)

        xb = _unpack_row_chunks(_load_row_chunks(xs_ref)).astype(BF16)
        gu = _dot(xb, wgu_b[...]) + bgu_ref[0, 0]
        f = gu.shape[1] // 2
        glu = jnp.minimum(gu[:, :f], SWIGLU_LIMIT)
        lin = jnp.clip(gu[:, f:], -SWIGLU_LIMIT, SWIGLU_LIMIT)
        act = glu * jax.nn.sigmoid(SWIGLU_ALPHA * glu) * (lin + 1.0)
        y = _dot(act.astype(BF16), wd_b[...]) + bd_ref[0, 0]
        _store_row_chunks(y_ref, _pack_row_chunks(y))


def _combine_kernel(ya_ref, x_ref, m_ref, p_ref, lg_ref, lb_ref, *rest, alpha):
    o_ref = rest[-1]
    p_rows = p_ref[...]
    eye = (lax.broadcasted_iota(I32, (p_rows.shape[0], LANES), 0)
           == lax.broadcasted_iota(I32, (p_rows.shape[0], LANES), 1)).astype(F32)
    p = lax.dot_general(p_rows, eye, (((0,), (0,)), ((), ())), preferred_element_type=F32,
                        precision=HIGHEST)
    y = None
    for kk in range(TOP_K):
        yk = p[:, kk:kk + 1] * _unpack_row_chunks(_load_row_chunks(ya_ref, (kk,)))
        y = yk if y is None else y + yk
    gt = m_ref[0, 5:6, :]
    o_ref[...] = _layer_norm(alpha * x_ref[...] + (1.0 + gt) * y, lg_ref[...], lb_ref[...])


def _moe_layer(x, m, layer, router_w, router_b, w_gu, b_gu, w_down, b_down, ln_g, ln_b, alpha, cfg):
    b, s, d = x.shape
    n_tok = b * s
    ne = router_w.shape[-1]
    f2 = w_gu.shape[-1]
    ff = f2 // 2
    assert d % (2 * LANES) == 0 and w_down.shape[-2:] == (ff, d)
    rc = d // (2 * LANES)
    xf = x.reshape(n_tok, d)
    const2 = lambda i: (0, 0)

    tt = cfg["router_tt"]
    assert s % tt == 0
    tri = jnp.triu(jnp.ones((tt, tt), BF16), k=1)
    hp, idx, probs, rank, cnt = pl.pallas_call(
        functools.partial(_router_kernel, tt=tt),
        grid=(n_tok // tt,),
        in_specs=[pl.BlockSpec((tt, d), lambda i: (i, 0)),
                  pl.BlockSpec((1, N_MOD, d), lambda i: (i * tt // s, 0, 0)),
                  pl.BlockSpec((ne, d), const2),
                  pl.BlockSpec((ne, 1), const2),
                  pl.BlockSpec((tt, tt), const2)],
        out_specs=[pl.BlockSpec((rc, tt, LANES), lambda i: (0, i, 0)),
                   pl.BlockSpec((TOP_K, tt), lambda i: (0, i)),
                   pl.BlockSpec((SUBLANES, tt), lambda i: (0, i)),
                   pl.BlockSpec((TOP_K, tt), lambda i: (0, i)),
                   pl.BlockSpec((ne, LANES), const2)],
        out_shape=[jax.ShapeDtypeStruct((rc, n_tok, LANES), U32),
                   jax.ShapeDtypeStruct((TOP_K, n_tok), I32),
                   jax.ShapeDtypeStruct((SUBLANES, n_tok), F32),
                   jax.ShapeDtypeStruct((TOP_K, n_tok), I32),
                   jax.ShapeDtypeStruct((ne, LANES), F32)],
        scratch_shapes=[pltpu.VMEM((ne, 1), F32)],
        compiler_params=_cparams(1),
        name="moe_router",
    )(xf, m, router_w[layer].T, router_b[layer].reshape(ne, 1), tri)

    bm = cfg["expert_bm"]
    cap = n_tok * TOP_K + ne * bm
    nb = cap // bm
    counts = cnt[:, 0].astype(I32)
    padded = (counts + bm - 1) // bm * bm
    pad_end = jnp.cumsum(padded)
    pad_start = pad_end - padded
    eids = jnp.arange(ne, dtype=I32)[:, None, None]
    dest = rank + jnp.sum(jnp.where(idx[None] == eids, pad_start[:, None, None], 0), axis=0)
    block_start = jnp.arange(nb, dtype=I32) * bm
    block_expert = jnp.minimum(jnp.sum(block_start[None, :] >= pad_end[:, None], axis=0), ne - 1).astype(I32)
    n_used = (pad_end[-1:] // bm).astype(I32)

    plane_offset = (jnp.arange(rc * n_tok, dtype=I32) // n_tok * cap)[None, :]
    slot_rows = [jnp.tile(dest[kk:kk + 1], (1, rc)) + plane_offset for kk in range(TOP_K)]

    xs = _sc_scatter_rows(hp.reshape(rc * n_tok, LANES), slot_rows, rc * cap).reshape(rc, cap, LANES)

    last = lambda j, be, nu: jnp.minimum(j, nu[0] - 1)
    y_rows = pl.pallas_call(
        _experts_kernel,
        grid_spec=pltpu.PrefetchScalarGridSpec(
            num_scalar_prefetch=2,
            grid=(nb,),
            in_specs=[pl.BlockSpec((rc, bm, LANES), lambda j, be, nu: (0, last(j, be, nu), 0)),
                      pl.BlockSpec((1, 1, d, f2), lambda j, be, nu: (layer, be[last(j, be, nu)], 0, 0)),
                      pl.BlockSpec((1, 1, 1, f2), lambda j, be, nu: (layer, be[last(j, be, nu)], 0, 0)),
                      pl.BlockSpec((1, 1, ff, d), lambda j, be, nu: (layer, be[last(j, be, nu)], 0, 0)),
                      pl.BlockSpec((1, 1, 1, d), lambda j, be, nu: (layer, be[last(j, be, nu)], 0, 0))],
            out_specs=pl.BlockSpec((rc, bm, LANES), lambda j, be, nu: (0, j, 0)),
            scratch_shapes=[pltpu.VMEM((d, f2), BF16), pltpu.VMEM((ff, d), BF16)]),
        out_shape=jax.ShapeDtypeStruct((rc, cap, LANES), U32),
        compiler_params=_cparams(1),
        name="moe_experts",
    )(block_expert, n_used, xs, w_gu, b_gu.reshape(*b_gu.shape[:2], 1, f2),
      w_down, b_down.reshape(*b_down.shape[:2], 1, d))

    tt = cfg["combine_tt"]
    n_groups = cfg["combine_groups"] if n_tok % (cfg["combine_groups"] * max(tt, SC_WINDOW * SC_WORKERS)) == 0 else 1
    n_grp = n_tok // n_groups
    tiles = n_grp // tt
    assert s % tt == 0 and n_grp % tt == 0
    y_table = y_rows.reshape(rc * cap, LANES)
    grp_plane_offset = (jnp.arange(rc * n_grp, dtype=I32) // n_grp * cap)[None, :]
    out = None
    for gi in range(n_groups):
        dest_g = dest[:, gi * n_grp:(gi + 1) * n_grp]
        rows_g = jnp.concatenate([jnp.tile(dest_g[kk:kk + 1], (1, rc)) + grp_plane_offset
                                  for kk in range(TOP_K)], axis=1)
        y_assign = _sc_gather_rows(y_table, rows_g).reshape(TOP_K, rc, n_grp, LANES)
        first = gi * tiles
        operands = [y_assign, xf, m, probs, ln_g.reshape(1, d), ln_b.reshape(1, d)]
        in_specs = [pl.BlockSpec((TOP_K, rc, tt, LANES), lambda i: (0, 0, i, 0)),
                    pl.BlockSpec((tt, d), lambda i, first=first: (first + i, 0)),
                    pl.BlockSpec((1, N_MOD, d), lambda i, first=first: ((first + i) * tt // s, 0, 0)),
                    pl.BlockSpec((SUBLANES, tt), lambda i, first=first: (0, first + i)),
                    pl.BlockSpec((1, d), const2),
                    pl.BlockSpec((1, d), const2)]
        aliases = {}
        if out is not None:
            operands.append(out)
            in_specs.append(pl.BlockSpec(memory_space=pl.ANY))
            aliases = {len(operands) - 1: 0}
        out = pl.pallas_call(
            functools.partial(_combine_kernel, alpha=alpha),
            grid=(tiles,),
            in_specs=in_specs,
            out_specs=pl.BlockSpec((tt, d), lambda i, first=first: (first + i, 0)),
            out_shape=jax.ShapeDtypeStruct((n_tok, d), F32),
            input_output_aliases=aliases,
            compiler_params=_cparams(1),
            name="moe_combine",
        )(*operands)
    return out.reshape(b, s, d)


def kernel(x, c, ada_w, ada_b, ln_g, ln_b, pool_w, pool_scale, dn_w_in, dn_conv_w, dn_a_log, dn_dt_bias,
           dn_norm_w, dn_w_out, router_w, router_b, exp_w_gu, exp_b_gu, exp_w_down, exp_b_down):
    b, s, d = x.shape
    depth = ada_w.shape[0]
    alpha = (2 * depth) ** 0.25
    cfg = _tile_config(s, b * s)
    mods = _mods(c, ada_w, ada_b)
    n_mixers = 2
    for i in range(depth):
        m = mods[i]
        j = i // n_mixers
        if i % n_mixers == 0:
            x = _pool_layer(x, m, pool_w[j], pool_scale[j], ln_g[i, 0], ln_b[i, 0], alpha, cfg)
        else:
            x = _deltanet_layer(x, m, dn_w_in[j], dn_conv_w[j], dn_a_log[j], dn_dt_bias[j], dn_norm_w[j],
                                dn_w_out[j], ln_g[i, 0], ln_b[i, 0], alpha, cfg)
        x = _moe_layer(x, m, i, router_w, router_b, exp_w_gu, exp_b_gu, exp_w_down, exp_b_down,
                       ln_g[i, 1], ln_b[i, 1], alpha, cfg)
    return x
```

```python
import functools

import jax
import jax.numpy as jnp
from jax import lax
from jax.experimental import pallas as pl
from jax.experimental.pallas import tpu as pltpu
from jax.experimental.pallas import tpu_sc as plsc

F32 = jnp.float32
BF16 = jnp.bfloat16
I32 = jnp.int32
U32 = jnp.uint32

N_MOD = 6
POOL_WINDOWS = (2, 4, 8, 16)
DN_HEADS = 8
DN_HEAD_DIM = 128
DN_CONV = 4
TOP_K = 4
SWIGLU_LIMIT = 7.0
SWIGLU_ALPHA = 1.702
LN_EPS = 1e-5
RMS_EPS = 1e-6

LANES = 128
SUBLANES = 8
VMEM_LIMIT_BYTES = 56 * 1024 * 1024
SC_WORKERS = 32
SC_WINDOW = 128

DN_CHUNK = LANES
DN_HEAD_GROUP = 8
POOL_HALO = 32
CONV_HALO = SUBLANES

HIGHEST = lax.Precision.HIGHEST


def _tile_config(seq, n_tok):
    return dict(
        pool_ts=min(512, seq),
        dn_in_ts=min(512, seq),
        dn_in_sub=128,
        dn_core_ts=min(512, seq),
        dn_out_ts=min(512, seq),
        router_tt=min(512, seq),
        expert_bm=512,
        combine_tt=min(256, seq),
        combine_groups=4,
    )


def _cparams(n_axes):
    return pltpu.CompilerParams(dimension_semantics=("arbitrary",) * n_axes,
                                vmem_limit_bytes=VMEM_LIMIT_BYTES)


def _layer_norm(v, g, b):
    mu = jnp.mean(v, axis=-1, keepdims=True)
    d = v - mu
    var = jnp.mean(d * d, axis=-1, keepdims=True)
    return d * lax.rsqrt(var + LN_EPS) * g + b


def _dot(a, b):
    return jnp.dot(a, b, preferred_element_type=F32)


def _dot_nt(a, b):
    return lax.dot_general(a, b, (((1,), (1,)), ((), ())), preferred_element_type=F32)


def _dot_tn(a, b):
    return lax.dot_general(a, b, (((0,), (0,)), ((), ())), preferred_element_type=F32)


def _bdot(a, b):
    return lax.dot_general(a, b, (((2,), (1,)), ((0,), (0,))), preferred_element_type=F32)


def _bdot_nt(a, b):
    return lax.dot_general(a, b, (((2,), (2,)), ((0,), (0,))), preferred_element_type=F32)


def _mods_kernel(c_ref, w_ref, b_ref, o_ref):
    c = c_ref[...]
    c_act = c * jax.nn.sigmoid(c)
    o_ref[0] = jnp.dot(c_act, w_ref[0], preferred_element_type=F32, precision=HIGHEST) + b_ref[0]


def _mods(c, ada_w, ada_b):
    depth, d, n = ada_w.shape
    b = c.shape[0]
    tn = 2048 if n % 2048 == 0 else n
    out = pl.pallas_call(
        _mods_kernel,
        grid=(depth, n // tn),
        in_specs=[pl.BlockSpec((b, d), lambda i, j: (0, 0)),
                  pl.BlockSpec((1, d, tn), lambda i, j: (i, 0, j)),
                  pl.BlockSpec((1, 1, tn), lambda i, j: (i, 0, j))],
        out_specs=pl.BlockSpec((1, b, tn), lambda i, j: (i, 0, j)),
        out_shape=jax.ShapeDtypeStruct((depth, b, n), F32),
        compiler_params=_cparams(2),
        name="adaln_mods",
    )(c, ada_w, ada_b.reshape(depth, 1, n))
    return out.reshape(depth, b, N_MOD, d)


def _pool_kernel(x_ref, m_ref, pw_ref, ps_ref, lg_ref, lb_ref, o_ref, e1, ea, eb, *, ts, alpha):
    s = pl.program_id(1)
    d = x_ref.shape[-1]
    gw = d // len(POOL_WINDOWS)
    halo = POOL_HALO
    rows = halo + ts
    x = x_ref[0]
    sh, sc, gt = m_ref[0, 0:1, :], m_ref[0, 1:2, :], m_ref[0, 2:3, :]
    h = x * (1.0 + sc) + sh

    @pl.when(s == 0)
    def _():
        e1[0:halo, :] = jnp.zeros((halo, d), F32)

    e1[halo:rows, :] = h
    ea[8:rows, :] = e1[8:rows, :] + e1[7:rows - 1, :]
    eb[16:rows, gw:] = ea[16:rows, gw:] + ea[14:rows - 2, gw:]
    ea[24:rows, 2 * gw:] = eb[24:rows, 2 * gw:] + eb[20:rows - 4, 2 * gw:]
    eb[32:rows, 3 * gw:] = ea[32:rows, 3 * gw:] + ea[24:rows - 8, 3 * gw:]

    pos = s * ts + lax.broadcasted_iota(I32, (ts, 1), 0)
    outs = []
    for g, win in enumerate(POOL_WINDOWS):
        src = ea if g % 2 == 0 else eb
        cols = slice(g * gw, (g + 1) * gw)
        cnt = jnp.minimum(pos + 1, win).astype(F32)
        pooled = src[halo:rows, cols] / cnt - h[:, cols]
        outs.append(_dot(pooled.astype(BF16), pw_ref[g]))
    y = jnp.concatenate(outs, axis=1) * ps_ref[...]
    o_ref[0] = _layer_norm(alpha * x + (1.0 + gt) * y, lg_ref[...], lb_ref[...])
    e1[0:halo, :] = e1[ts:rows, :]


def _pool_layer(x, m, pool_w, pool_scale, ln_g, ln_b, alpha, cfg):
    b, s, d = x.shape
    ts = cfg["pool_ts"]
    g, gw, _ = pool_w.shape
    assert POOL_WINDOWS == (2, 4, 8, 16) and g == len(POOL_WINDOWS) and s % ts == 0 and ts >= POOL_HALO
    row = lambda v: v.reshape(1, d)
    return pl.pallas_call(
        functools.partial(_pool_kernel, ts=ts, alpha=alpha),
        grid=(b, s // ts),
        in_specs=[pl.BlockSpec((1, ts, d), lambda i, j: (i, j, 0)),
                  pl.BlockSpec((1, N_MOD, d), lambda i, j: (i, 0, 0)),
                  pl.BlockSpec((g, gw, gw), lambda i, j: (0, 0, 0)),
                  pl.BlockSpec((1, d), lambda i, j: (0, 0)),
                  pl.BlockSpec((1, d), lambda i, j: (0, 0)),
                  pl.BlockSpec((1, d), lambda i, j: (0, 0))],
        out_specs=pl.BlockSpec((1, ts, d), lambda i, j: (i, j, 0)),
        out_shape=jax.ShapeDtypeStruct((b, s, d), F32),
        scratch_shapes=[pltpu.VMEM((POOL_HALO + ts, d), F32)] * 3,
        compiler_params=_cparams(2),
        name="pool_layer",
    )(x, m, pool_w.astype(BF16), row(pool_scale), row(ln_g), row(ln_b))


def _dn_in_kernel(x_ref, m_ref, wm_ref, ws_ref, cw_ref, av_ref, dv_ref,
                  q_ref, k_ref, v_ref, z_ref, bg_ref, ext, *, ts, sr):
    s = pl.program_id(1)
    nh, dh = DN_HEADS, DN_HEAD_DIM
    w = nh * dh
    halo = CONV_HALO
    sh, sc = m_ref[0, 0:1, :], m_ref[0, 1:2, :]

    @pl.when(s == 0)
    def _():
        ext[0:halo, :] = jnp.zeros((halo, 3 * w), F32)

    def project(i):
        rows = slice(i * sr, (i + 1) * sr)
        h = (x_ref[0, rows, :] * (1.0 + sc) + sh).astype(BF16)
        proj = _dot(h, wm_ref[...])
        ext[halo + i * sr:halo + (i + 1) * sr, :] = proj[:, :3 * w]
        z_ref[0, rows, :] = proj[:, 3 * w:]
        small = _dot(h, ws_ref[...])
        lane = lax.broadcasted_iota(I32, small.shape, 1)
        beta = jax.nn.sigmoid(small)
        g = -jnp.exp(av_ref[...]) * jax.nn.softplus(small + dv_ref[...])
        bg_ref[0, rows, :] = jnp.where(lane < nh, beta, jnp.where(lane < 2 * nh, g, 0.0))

    def mix(i):
        rows = slice(i * sr, (i + 1) * sr)
        base = halo - (DN_CONV - 1) + i * sr
        conv = cw_ref[0:1, :] * ext[base:base + sr, :]
        for j in range(1, DN_CONV):
            conv = conv + cw_ref[j:j + 1, :] * ext[base + j:base + j + sr, :]
        act = conv * jax.nn.sigmoid(conv)
        for hd in range(nh):
            qh = act[:, hd * dh:(hd + 1) * dh]
            kh = act[:, w + hd * dh:w + (hd + 1) * dh]
            q_ref[0, hd, rows, :] = (qh * lax.rsqrt(jnp.sum(qh * qh, axis=-1, keepdims=True) + RMS_EPS)
                                     * (dh ** -0.5))
            k_ref[0, hd, rows, :] = kh * lax.rsqrt(jnp.sum(kh * kh, axis=-1, keepdims=True) + RMS_EPS)
            v_ref[0, hd, rows, :] = act[:, 2 * w + hd * dh:2 * w + (hd + 1) * dh]

    n_sub = ts // sr
    project(0)
    for i in range(1, n_sub):
        project(i)
        mix(i - 1)
    mix(n_sub - 1)
    ext[0:halo, :] = ext[ts:ts + halo, :]


def _dn_core_kernel(q_ref, k_ref, v_ref, bg_ref, o_ref,
                    st, bb, gcb, gt, gl_s, u_s, wq_s, ai_s, kd_s, *, ts):
    s = pl.program_id(1)
    nh, dh, c = DN_HEADS, DN_HEAD_DIM, DN_CHUNK
    nc = ts // c

    @pl.when(s == 0)
    def _():
        st[...] = jnp.zeros(st.shape, F32)

    bgv = bg_ref[0]
    lane = lax.broadcasted_iota(I32, bgv.shape, 1)
    rowc = lax.broadcasted_iota(I32, bgv.shape, 0) % c
    gc = jnp.where(lane >= nh, bgv, 0.0)
    shift = 1
    while shift < c:
        gc = gc + jnp.where(rowc >= shift, pltpu.roll(gc, shift, 0), 0.0)
        shift *= 2
    gc_t = gc.T
    for hd in range(nh):
        for ci in range(nc):
            gt[hd * nc + ci] = jnp.broadcast_to(gc_t[nh + hd:nh + hd + 1, ci * c:(ci + 1) * c], (SUBLANES, c))
        bb[hd] = jnp.broadcast_to(bgv[:, hd:hd + 1], (ts, LANES))
        gcb[hd] = jnp.broadcast_to(gc[:, nh + hd:nh + hd + 1], (ts, LANES))

    ri = lax.broadcasted_iota(I32, (c, c), 0)
    ci_ = lax.broadcasted_iota(I32, (c, c), 1)
    tril = ri >= ci_
    eye = (ri == ci_).astype(F32)
    off_masks = []
    blk = 1
    while blk < c:
        off_masks.append((ri // blk != ci_ // blk) & (ri // (2 * blk) == ci_ // (2 * blk)) & (ri > ci_))
        blk *= 2

    hg = DN_HEAD_GROUP
    nb = hg * nc
    n_pairs = nb // 2
    two = lambda msk: jnp.concatenate([msk, msk], axis=1)
    tril2, eye2, off2 = two(tril), two(eye), [two(msk) for msk in off_masks]

    def pair(a):
        a = a.reshape(n_pairs, 2, a.shape[1], a.shape[2])
        return jnp.concatenate([a[:, 0], a[:, 1]], axis=-1)

    def unpair(a):
        w = a.shape[-1] // 2
        return jnp.stack([a[..., :w], a[..., w:]], axis=1).reshape(nb, a.shape[1], w)

    def block_diag(a):
        left = lax.broadcasted_iota(I32, a.shape[1:], 1) < a.shape[-1] // 2
        zero = jnp.zeros_like(a)
        return jnp.concatenate([jnp.where(left, a, zero), jnp.where(left, zero, a)], axis=1)

    def group_body(gi, carry):
        heads = pl.ds(gi * hg, hg)
        chunks = lambda a: a.reshape(nb, c, a.shape[-1])
        per_head = lambda a: a.reshape(hg, ts, a.shape[-1])
        q, k, v = chunks(q_ref[0, heads]), chunks(k_ref[0, heads]), chunks(v_ref[0, heads])
        beta, gch = chunks(bb[heads]), chunks(gcb[heads])
        egh = jnp.exp(gch)
        g_last = jnp.broadcast_to(gch[:, c - 1:c, :], gch.shape)
        kb = k * beta
        vb = v * beta
        kd_s[heads] = per_head((k * jnp.exp(g_last - gch)).astype(BF16))
        gl_s[heads] = jnp.broadcast_to(egh[:, c - 1:c, :], (nb, SUBLANES, LANES)).reshape(hg, nc * SUBLANES, LANES)
        grow = gt[pl.ds(gi * nb, nb), 0:1, :]
        dec = jnp.where(tril2, jnp.exp(jnp.where(tril2, pair(gch) - pair(grow), 0.0)), 0.0)
        kq = jnp.concatenate([pair(kb), pair(q)], axis=1).astype(BF16)
        a_all = _bdot_nt(kq, block_diag(pair(k).astype(BF16)))
        ai_s[heads] = per_head(unpair((a_all[:, c:] * dec).astype(BF16)))
        l_mat = a_all[:, :c] * dec
        t_inv = eye2 - jnp.where(off2[0], l_mat, 0.0)
        for off in off2[1:]:
            tb = t_inv.astype(BF16)
            l_off = jnp.where(off, l_mat, 0.0).astype(BF16)
            t_inv = t_inv - _bdot(tb, block_diag(_bdot(l_off, block_diag(tb)).astype(BF16)))
        rhs = jnp.concatenate([vb, kb * egh], axis=2).astype(BF16)
        uw = unpair(_bdot(t_inv.astype(BF16), block_diag(pair(rhs))))
        u_s[heads] = per_head(uw[:, :, :dh])
        wq = jnp.concatenate([uw[:, :, dh:], q * egh], axis=1).astype(BF16)
        wq_s[heads] = wq.reshape(hg, 2 * ts, dh)
        return carry

    lax.fori_loop(0, nh // hg, group_body, 0)

    def chunk_body(ci, carry):
        rows = pl.ds(pl.multiple_of(ci * c, c), c)
        rows2 = pl.ds(pl.multiple_of(ci * 2 * c, 2 * c), 2 * c)
        heads = range(nh)
        states = [st[hd] for hd in heads]
        ws_qs = [_dot(wq_s[hd, rows2, :], states[hd].astype(BF16)) for hd in heads]
        v_nb = [(u_s[hd, rows, :] - ws_qs[hd][:c]).astype(BF16) for hd in heads]
        outs = [ws_qs[hd][c:] + _dot(ai_s[hd, rows, :], v_nb[hd]) for hd in heads]
        new_states = [states[hd] * gl_s[hd, pl.ds(pl.multiple_of(ci * SUBLANES, SUBLANES), 1), :]
                      + _dot_tn(kd_s[hd, rows, :], v_nb[hd]) for hd in heads]
        for hd in heads:
            o_ref[0, hd, rows, :] = outs[hd]
            st[hd] = new_states[hd]
        return carry

    lax.fori_loop(0, nc, chunk_body, 0)


def _dn_out_kernel(o_ref, z_ref, x_ref, m_ref, nw_ref, wo_ref, lg_ref, lb_ref, out_ref, *, alpha):
    nh, dh = DN_HEADS, DN_HEAD_DIM
    z = z_ref[0]
    parts = []
    for hd in range(nh):
        oh = o_ref[0, hd]
        on = oh * lax.rsqrt(jnp.mean(oh * oh, axis=-1, keepdims=True) + RMS_EPS) * nw_ref[...]
        zh = z[:, hd * dh:(hd + 1) * dh]
        parts.append((on * (zh * jax.nn.sigmoid(zh))).astype(BF16))
    y = _dot(jnp.concatenate(parts, axis=1), wo_ref[...])
    x = x_ref[0]
    gt = m_ref[0, 2:3, :]
    out_ref[0] = _layer_norm(alpha * x + (1.0 + gt) * y, lg_ref[...], lb_ref[...])


def _deltanet_layer(x, m, w_in, conv_w, a_log, dt_bias, norm_w, w_out, ln_g, ln_b, alpha, cfg):
    b, s, d = x.shape
    nh, dh = DN_HEADS, DN_HEAD_DIM
    w = nh * dh
    assert w_in.shape == (d, 4 * w + 2 * nh) and conv_w.shape == (DN_CONV, 3 * w) and 2 * nh <= LANES
    w_main = w_in[:, :4 * w].astype(BF16)
    w_small = jnp.pad(w_in[:, 4 * w:], ((0, 0), (0, LANES - 2 * nh))).astype(BF16)
    avec = jnp.zeros((1, LANES), F32).at[0, nh:2 * nh].set(a_log.astype(F32))
    dvec = jnp.zeros((1, LANES), F32).at[0, nh:2 * nh].set(dt_bias.astype(F32))
    row = lambda v: v.reshape(1, -1)
    const2 = lambda i, j: (0, 0)

    ts = cfg["dn_in_ts"]
    assert s % ts == 0 and ts >= CONV_HALO
    head_major = jax.ShapeDtypeStruct((b, nh, s, dh), F32)
    hm_spec = lambda t: pl.BlockSpec((1, nh, t, dh), lambda i, j: (i, 0, j, 0))
    q, k, v, z, bg = pl.pallas_call(
        functools.partial(_dn_in_kernel, ts=ts, sr=min(cfg["dn_in_sub"], ts)),
        grid=(b, s // ts),
        in_specs=[pl.BlockSpec((1, ts, d), lambda i, j: (i, j, 0)),
                  pl.BlockSpec((1, N_MOD, d), lambda i, j: (i, 0, 0)),
                  pl.BlockSpec((d, 4 * w), const2),
                  pl.BlockSpec((d, LANES), const2),
                  pl.BlockSpec((DN_CONV, 3 * w), const2),
                  pl.BlockSpec((1, LANES), const2),
                  pl.BlockSpec((1, LANES), const2)],
        out_specs=[hm_spec(ts), hm_spec(ts), hm_spec(ts),
                   pl.BlockSpec((1, ts, w), lambda i, j: (i, j, 0)),
                   pl.BlockSpec((1, ts, LANES), lambda i, j: (i, j, 0))],
        out_shape=[head_major, head_major, head_major,
                   jax.ShapeDtypeStruct((b, s, w), F32),
                   jax.ShapeDtypeStruct((b, s, LANES), F32)],
        scratch_shapes=[pltpu.VMEM((CONV_HALO + ts, 3 * w), F32)],
        compiler_params=_cparams(2),
        name="deltanet_in",
    )(x, m, w_main, w_small, conv_w, avec, dvec)

    ts = cfg["dn_core_ts"]
    assert s % ts == 0 and ts % DN_CHUNK == 0 and dh == DN_CHUNK
    per_head = pltpu.VMEM((nh, ts, dh), F32)
    o = pl.pallas_call(
        functools.partial(_dn_core_kernel, ts=ts),
        grid=(b, s // ts),
        in_specs=[hm_spec(ts), hm_spec(ts), hm_spec(ts),
                  pl.BlockSpec((1, ts, LANES), lambda i, j: (i, j, 0))],
        out_specs=hm_spec(ts),
        out_shape=head_major,
        scratch_shapes=[pltpu.VMEM((nh, dh, dh), F32),
                        per_head, per_head,
                        pltpu.VMEM((nh * (ts // DN_CHUNK), SUBLANES, DN_CHUNK), F32),
                        pltpu.VMEM((nh, ts // DN_CHUNK * SUBLANES, LANES), F32),
                        per_head,
                        pltpu.VMEM((nh, 2 * ts, dh), BF16),
                        pltpu.VMEM((nh, ts, DN_CHUNK), BF16),
                        pltpu.VMEM((nh, ts, dh), BF16)],
        compiler_params=_cparams(2),
        name="deltanet_core",
    )(q, k, v, bg)

    ts = cfg["dn_out_ts"]
    assert s % ts == 0
    return pl.pallas_call(
        functools.partial(_dn_out_kernel, alpha=alpha),
        grid=(b, s // ts),
        in_specs=[hm_spec(ts),
                  pl.BlockSpec((1, ts, w), lambda i, j: (i, j, 0)),
                  pl.BlockSpec((1, ts, d), lambda i, j: (i, j, 0)),
                  pl.BlockSpec((1, N_MOD, d), lambda i, j: (i, 0, 0)),
                  pl.BlockSpec((1, dh), const2),
                  pl.BlockSpec((w, d), const2),
                  pl.BlockSpec((1, d), const2),
                  pl.BlockSpec((1, d), const2)],
        out_specs=pl.BlockSpec((1, ts, d), lambda i, j: (i, j, 0)),
        out_shape=jax.ShapeDtypeStruct((b, s, d), F32),
        compiler_params=_cparams(2),
        name="deltanet_out",
    )(o, z, x, m, row(norm_w), w_out.astype(BF16), row(ln_g), row(ln_b))


def _pack_row_chunks(v):
    half = v.shape[-1] // 2
    bits = pltpu.bitcast(v.astype(BF16).astype(F32), U32)
    packed = (bits[:, :half] >> 16) | (bits[:, half:] & jnp.uint32(0xFFFF0000))
    return [packed[:, i * LANES:(i + 1) * LANES] for i in range(half // LANES)]


def _unpack_row_chunks(chunks):
    lo = [pltpu.bitcast(c << 16, F32) for c in chunks]
    hi = [pltpu.bitcast(c & jnp.uint32(0xFFFF0000), F32) for c in chunks]
    return jnp.concatenate(lo + hi, axis=1)


def _store_row_chunks(ref, chunks):
    for i, ch in enumerate(chunks):
        ref[i] = ch


def _load_row_chunks(ref, lead=()):
    return [ref[(*lead, i)] for i in range(ref.shape[len(lead)])]


def _sc_mesh():
    return plsc.VectorSubcoreMesh(core_axis_name="c", subcore_axis_name="s")


def _sc_gather_rows(table, idx):
    n = idx.shape[1]
    assert table.shape[1] == LANES and idx.shape[0] == 1 and n % (SC_WINDOW * SC_WORKERS) == 0

    @functools.partial(pl.kernel, out_type=jax.ShapeDtypeStruct((n, LANES), table.dtype), mesh=_sc_mesh(),
                       name="sc_gather_rows")
    def gather(table_hbm, idx_hbm, out_hbm):
        def body(idx_vmem, out_vmem):
            pltpu.sync_copy(table_hbm.at[idx_vmem.at[0]], out_vmem)

        pltpu.emit_pipeline(
            body,
            grid=(n // SC_WINDOW,),
            in_specs=[pl.BlockSpec((1, SC_WINDOW), lambda i: (0, i))],
            out_specs=[pl.BlockSpec((SC_WINDOW, LANES), lambda i: (i, 0))],
            core_axis_name=("c", "s"),
            dimension_semantics=(pltpu.PARALLEL,),
        )(idx_hbm, out_hbm)

    return gather(table, idx)


def _sc_scatter_rows(x, idx_list, n_out):
    n = x.shape[0]
    assert x.shape[1] == LANES and n % (SC_WINDOW * SC_WORKERS) == 0
    assert all(idx.shape == (1, n) for idx in idx_list)

    @functools.partial(pl.kernel, out_type=jax.ShapeDtypeStruct((n_out, LANES), x.dtype), mesh=_sc_mesh(),
                       name="sc_scatter_rows")
    def scatter(x_hbm, *refs):
        idx_hbms, out_hbm = refs[:-1], refs[-1]

        def body(x_vmem, *idx_vmems):
            for idx_vmem in idx_vmems:
                pltpu.sync_copy(x_vmem, out_hbm.at[idx_vmem.at[0]])

        pltpu.emit_pipeline(
            body,
            grid=(n // SC_WINDOW,),
            in_specs=[pl.BlockSpec((SC_WINDOW, LANES), lambda i: (i, 0))]
                     + [pl.BlockSpec((1, SC_WINDOW), lambda i: (0, i))] * len(idx_list),
            out_specs=[],
            core_axis_name=("c", "s"),
            dimension_semantics=(pltpu.PARALLEL,),
        )(x_hbm, *idx_hbms)

    return scatter(x, *idx_list)


def _router_kernel(x_ref, m_ref, rwt_ref, rb_ref, tri_ref,
                   hp_ref, idx_ref, p_ref, rank_ref, cnt_ref, run, *, tt):
    i = pl.program_id(0)

    @pl.when(i == 0)
    def _():
        run[...] = jnp.zeros(run.shape, F32)

    x = x_ref[...]
    d = x.shape[-1]
    sh, sc = m_ref[0, 3:4, :], m_ref[0, 4:5, :]
    h = x * (1.0 + sc) + sh
    _store_row_chunks(hp_ref, _pack_row_chunks(h))

    h_hi = h.astype(BF16)
    h_lo = (h - h_hi.astype(F32)).astype(BF16)
    rw = rwt_ref[...]
    w_hi = rw.astype(BF16)
    w_lo = (rw - w_hi.astype(F32)).astype(BF16)
    ne = rw.shape[0]
    hi_terms = _dot_nt(jnp.concatenate([w_hi, w_lo], axis=0), h_hi)
    logits = hi_terms[:ne] + hi_terms[ne:] + _dot_nt(w_hi, h_lo) + rb_ref[...]
    eio = lax.broadcasted_iota(I32, logits.shape, 0).astype(F32)
    vals, idxs, sels = [], [], []
    for _ in range(TOP_K):
        mx = jnp.max(logits, axis=0, keepdims=True)
        ix = jnp.min(jnp.where(logits == mx, eio, float(ne)), axis=0, keepdims=True)
        sel = eio == ix
        logits = jnp.where(sel, -jnp.inf, logits)
        vals.append(mx)
        idxs.append(ix)
        sels.append(sel)
    exps = [jnp.exp(v - vals[0]) for v in vals]
    den = functools.reduce(lambda a, b_: a + b_, exps)
    chosen = functools.reduce(jnp.logical_or, sels)
    onehot = jnp.where(chosen, 1.0, 0.0)
    before = _dot(onehot.astype(BF16), tri_ref[...]) + run[...]
    ranks = [jnp.sum(jnp.where(sel, before, 0.0), axis=0, keepdims=True) for sel in sels]
    run[...] = run[...] + jnp.sum(onehot, axis=1, keepdims=True)
    idx_ref[...] = jnp.concatenate(idxs, axis=0).astype(I32)
    p_ref[...] = jnp.concatenate([e / den for e in exps]
                                 + [jnp.zeros((p_ref.shape[0] - TOP_K, den.shape[1]), F32)], axis=0)
    rank_ref[...] = jnp.concatenate(ranks, axis=0).astype(I32)
    cnt_ref[...] = jnp.broadcast_to(run[...], cnt_ref.shape)


def _experts_kernel(be_ref, nu_ref, nx_ref, par_ref, xs_ref, wgu_hbm, bgu_ref, wd_hbm, bd_ref, y_ref,
                    wgu_f, wd_f, sems, wgu_b, wd_b, *, layer):
    j = pl.program_id(0)

    def weight_copies(expert, slot):
        return (pltpu.make_async_copy(wgu_hbm.at[layer, expert], wgu_f.at[slot], sems.at[0, slot]),
                pltpu.make_async_copy(wd_hbm.at[layer, expert], wd_f.at[slot], sems.at[1, slot]))

    @pl.when(j >= nu_ref[0])
    def _():
        y_ref[...] = jnp.zeros(y_ref.shape, U32)

    @pl.when(j < nu_ref[0])
    def _():
        expert, slot = be_ref[j], par_ref[j]
        new_expert = jnp.logical_or(j == 0, expert != be_ref[jnp.maximum(j - 1, 0)])

        @pl.when(j == 0)
        def _():
            for cp in weight_copies(expert, slot):
                cp.start()

        @pl.when(new_expert)
        def _():
            for cp in weight_copies(expert, slot):
                cp.wait()

            @pl.when(nx_ref[j] >= 0)
            def _():
                for cp in weight_copies(nx_ref[j], 1 - slot):
                    cp.start()

            wgu_b[...] = wgu_f[slot].astype(BF16)
            wd_b[...] = wd_f[slot].astype(BF16)

        xb = _unpack_row_chunks(_load_row_chunks(xs_ref)).astype(BF16)
        gu = _dot(xb, wgu_b[...]) + bgu_ref[0, 0]
        f = gu.shape[1] // 2
        glu = jnp.minimum(gu[:, :f], SWIGLU_LIMIT)
        lin = jnp.clip(gu[:, f:], -SWIGLU_LIMIT, SWIGLU_LIMIT)
        act = glu * jax.nn.sigmoid(SWIGLU_ALPHA * glu) * (lin + 1.0)
        y = _dot(act.astype(BF16), wd_b[...]) + bd_ref[0, 0]
        _store_row_chunks(y_ref, _pack_row_chunks(y))


def _combine_kernel(ya_ref, x_ref, m_ref, p_ref, lg_ref, lb_ref, *rest, alpha):
    o_ref = rest[-1]
    p_rows = p_ref[...]
    eye = (lax.broadcasted_iota(I32, (p_rows.shape[0], LANES), 0)
           == lax.broadcasted_iota(I32, (p_rows.shape[0], LANES), 1)).astype(F32)
    p = lax.dot_general(p_rows, eye, (((0,), (0,)), ((), ())), preferred_element_type=F32,
                        precision=HIGHEST)
    y = None
    for kk in range(TOP_K):
        yk = p[:, kk:kk + 1] * _unpack_row_chunks(_load_row_chunks(ya_ref, (kk,)))
        y = yk if y is None else y + yk
    gt = m_ref[0, 5:6, :]
    o_ref[...] = _layer_norm(alpha * x_ref[...] + (1.0 + gt) * y, lg_ref[...], lb_ref[...])


def _moe_layer(x, m, layer, router_w, router_b, w_gu, b_gu, w_down, b_down, ln_g, ln_b, alpha, cfg):
    b, s, d = x.shape
    n_tok = b * s
    ne = router_w.shape[-1]
    f2 = w_gu.shape[-1]
    ff = f2 // 2
    assert d % (2 * LANES) == 0 and w_down.shape[-2:] == (ff, d)
    rc = d // (2 * LANES)
    xf = x.reshape(n_tok, d)
    const2 = lambda i: (0, 0)

    tt = cfg["router_tt"]
    assert s % tt == 0
    tri = jnp.triu(jnp.ones((tt, tt), BF16), k=1)
    hp, idx, probs, rank, cnt = pl.pallas_call(
        functools.partial(_router_kernel, tt=tt),
        grid=(n_tok // tt,),
        in_specs=[pl.BlockSpec((tt, d), lambda i: (i, 0)),
                  pl.BlockSpec((1, N_MOD, d), lambda i: (i * tt // s, 0, 0)),
                  pl.BlockSpec((ne, d), const2),
                  pl.BlockSpec((ne, 1), const2),
                  pl.BlockSpec((tt, tt), const2)],
        out_specs=[pl.BlockSpec((rc, tt, LANES), lambda i: (0, i, 0)),
                   pl.BlockSpec((TOP_K, tt), lambda i: (0, i)),
                   pl.BlockSpec((SUBLANES, tt), lambda i: (0, i)),
                   pl.BlockSpec((TOP_K, tt), lambda i: (0, i)),
                   pl.BlockSpec((ne, LANES), const2)],
        out_shape=[jax.ShapeDtypeStruct((rc, n_tok, LANES), U32),
                   jax.ShapeDtypeStruct((TOP_K, n_tok), I32),
                   jax.ShapeDtypeStruct((SUBLANES, n_tok), F32),
                   jax.ShapeDtypeStruct((TOP_K, n_tok), I32),
                   jax.ShapeDtypeStruct((ne, LANES), F32)],
        scratch_shapes=[pltpu.VMEM((ne, 1), F32)],
        compiler_params=_cparams(1),
        name="moe_router",
    )(xf, m, router_w[layer].T, router_b[layer].reshape(ne, 1), tri)

    bm = cfg["expert_bm"]
    cap = n_tok * TOP_K + ne * bm
    nb = cap // bm
    counts = cnt[:, 0].astype(I32)
    padded = (counts + bm - 1) // bm * bm
    pad_end = jnp.cumsum(padded)
    pad_start = pad_end - padded
    eids = jnp.arange(ne, dtype=I32)[:, None, None]
    dest = rank + jnp.sum(jnp.where(idx[None] == eids, pad_start[:, None, None], 0), axis=0)
    block_start = jnp.arange(nb, dtype=I32) * bm
    block_expert = jnp.minimum(jnp.sum(block_start[None, :] >= pad_end[:, None], axis=0), ne - 1).astype(I32)
    n_used = (pad_end[-1:] // bm).astype(I32)
    first_of_expert = jnp.concatenate([jnp.ones((1,), bool), block_expert[1:] != block_expert[:-1]])
    weight_slot = ((jnp.cumsum(first_of_expert.astype(I32)) - 1) % 2).astype(I32)
    next_block = pad_end[block_expert] // bm
    next_expert = jnp.where(next_block < n_used[0], block_expert[jnp.minimum(next_block, nb - 1)], -1).astype(I32)

    plane_offset = (jnp.arange(rc * n_tok, dtype=I32) // n_tok * cap)[None, :]
    slot_rows = [jnp.tile(dest[kk:kk + 1], (1, rc)) + plane_offset for kk in range(TOP_K)]

    xs = _sc_scatter_rows(hp.reshape(rc * n_tok, LANES), slot_rows, rc * cap).reshape(rc, cap, LANES)

    last = lambda j, be, nu, nx, par: jnp.minimum(j, nu[0] - 1)
    bias_map = lambda j, be, nu, nx, par: (layer, be[last(j, be, nu, nx, par)], 0, 0)
    y_rows = pl.pallas_call(
        functools.partial(_experts_kernel, layer=layer),
        grid_spec=pltpu.PrefetchScalarGridSpec(
            num_scalar_prefetch=4,
            grid=(nb,),
            in_specs=[pl.BlockSpec((rc, bm, LANES), lambda j, be, nu, nx, par: (0, last(j, be, nu, nx, par), 0)),
                      pl.BlockSpec(memory_space=pl.ANY),
                      pl.BlockSpec((1, 1, 1, f2), bias_map),
                      pl.BlockSpec(memory_space=pl.ANY),
                      pl.BlockSpec((1, 1, 1, d), bias_map)],
            out_specs=pl.BlockSpec((rc, bm, LANES), lambda j, be, nu, nx, par: (0, j, 0)),
            scratch_shapes=[pltpu.VMEM((2, d, f2), F32), pltpu.VMEM((2, ff, d), F32),
                            pltpu.SemaphoreType.DMA((2, 2)),
                            pltpu.VMEM((d, f2), BF16), pltpu.VMEM((ff, d), BF16)]),
        out_shape=jax.ShapeDtypeStruct((rc, cap, LANES), U32),
        compiler_params=_cparams(1),
        name="moe_experts",
    )(block_expert, n_used, next_expert, weight_slot, xs, w_gu, b_gu.reshape(*b_gu.shape[:2], 1, f2),
      w_down, b_down.reshape(*b_down.shape[:2], 1, d))

    tt = cfg["combine_tt"]
    n_groups = cfg["combine_groups"] if n_tok % (cfg["combine_groups"] * max(tt, SC_WINDOW * SC_WORKERS)) == 0 else 1
    n_grp = n_tok // n_groups
    tiles = n_grp // tt
    assert s % tt == 0 and n_grp % tt == 0
    y_table = y_rows.reshape(rc * cap, LANES)
    grp_rows = (dest.reshape(TOP_K, n_groups, n_grp).transpose(1, 0, 2)[:, :, None, :]
                + (jnp.arange(rc, dtype=I32) * cap)[None, None, :, None])
    out = None
    for gi in range(n_groups):
        rows_g = grp_rows[gi].reshape(1, TOP_K * rc * n_grp)
        y_assign = _sc_gather_rows(y_table, rows_g).reshape(TOP_K, rc, n_grp, LANES)
        first = gi * tiles
        operands = [y_assign, xf, m, probs, ln_g.reshape(1, d), ln_b.reshape(1, d)]
        in_specs = [pl.BlockSpec((TOP_K, rc, tt, LANES), lambda i: (0, 0, i, 0)),
                    pl.BlockSpec((tt, d), lambda i, first=first: (first + i, 0)),
                    pl.BlockSpec((1, N_MOD, d), lambda i, first=first: ((first + i) * tt // s, 0, 0)),
                    pl.BlockSpec((SUBLANES, tt), lambda i, first=first: (0, first + i)),
                    pl.BlockSpec((1, d), const2),
                    pl.BlockSpec((1, d), const2)]
        aliases = {}
        if out is not None:
            operands.append(out)
            in_specs.append(pl.BlockSpec(memory_space=pl.ANY))
            aliases = {len(operands) - 1: 0}
        out = pl.pallas_call(
            functools.partial(_combine_kernel, alpha=alpha),
            grid=(tiles,),
            in_specs=in_specs,
            out_specs=pl.BlockSpec((tt, d), lambda i, first=first: (first + i, 0)),
            out_shape=jax.ShapeDtypeStruct((n_tok, d), F32),
            input_output_aliases=aliases,
            compiler_params=_cparams(1),
            name="moe_combine",
        )(*operands)
    return out.reshape(b, s, d)


def kernel(x, c, ada_w, ada_b, ln_g, ln_b, pool_w, pool_scale, dn_w_in, dn_conv_w, dn_a_log, dn_dt_bias,
           dn_norm_w, dn_w_out, router_w, router_b, exp_w_gu, exp_b_gu, exp_w_down, exp_b_down):
    b, s, d = x.shape
    depth = ada_w.shape[0]
    alpha = (2 * depth) ** 0.25
    cfg = _tile_config(s, b * s)
    mods = _mods(c, ada_w, ada_b)
    n_mixers = 2
    for i in range(depth):
        m = mods[i]
        j = i // n_mixers
        if i % n_mixers == 0:
            x = _pool_layer(x, m, pool_w[j], pool_scale[j], ln_g[i, 0], ln_b[i, 0], alpha, cfg)
        else:
            x = _deltanet_layer(x, m, dn_w_in[j], dn_conv_w[j], dn_a_log[j], dn_dt_bias[j], dn_norm_w[j],
                                dn_w_out[j], ln_g[i, 0], ln_b[i, 0], alpha, cfg)
        x = _moe_layer(x, m, i, router_w, router_b, exp_w_gu, exp_b_gu, exp_w_down, exp_b_down,
                       ln_g[i, 1], ln_b[i, 1], alpha, cfg)
    return x
```

```python
import functools

import jax
import jax.numpy as jnp
from jax import lax
from jax.experimental import pallas as pl
from jax.experimental.pallas import tpu as pltpu
from jax.experimental.pallas import tpu_sc as plsc

F32 = jnp.float32
BF16 = jnp.bfloat16
I32 = jnp.int32
U32 = jnp.uint32

N_MOD = 6
POOL_WINDOWS = (2, 4, 8, 16)
DN_HEADS = 8
DN_HEAD_DIM = 128
DN_CONV = 4
TOP_K = 4
SWIGLU_LIMIT = 7.0
SWIGLU_ALPHA = 1.702
LN_EPS = 1e-5
RMS_EPS = 1e-6

LANES = 128
SUBLANES = 8
VMEM_LIMIT_BYTES = 56 * 1024 * 1024
SC_WORKERS = 32
SC_WINDOW = 128

DN_CHUNK = LANES
DN_HEAD_GROUP = 8
POOL_HALO = 32
CONV_HALO = SUBLANES

HIGHEST = lax.Precision.HIGHEST


def _tile_config(seq, n_tok):
    return dict(
        pool_ts=min(512, seq),
        dn_in_ts=min(512, seq),
        dn_in_sub=128,
        dn_core_ts=min(512, seq),
        dn_out_ts=min(512, seq),
        router_tt=min(512, seq),
        expert_bm=512,
        combine_tt=min(512, seq),
        combine_groups=8,
    )


def _cparams(n_axes):
    return pltpu.CompilerParams(dimension_semantics=("arbitrary",) * n_axes,
                                vmem_limit_bytes=VMEM_LIMIT_BYTES)


def _layer_norm(v, g, b):
    mu = jnp.mean(v, axis=-1, keepdims=True)
    d = v - mu
    var = jnp.mean(d * d, axis=-1, keepdims=True)
    return d * lax.rsqrt(var + LN_EPS) * g + b


def _dot(a, b):
    return jnp.dot(a, b, preferred_element_type=F32)


def _dot_nt(a, b):
    return lax.dot_general(a, b, (((1,), (1,)), ((), ())), preferred_element_type=F32)


def _dot_tn(a, b):
    return lax.dot_general(a, b, (((0,), (0,)), ((), ())), preferred_element_type=F32)


def _bdot(a, b):
    return lax.dot_general(a, b, (((2,), (1,)), ((0,), (0,))), preferred_element_type=F32)


def _bdot_nt(a, b):
    return lax.dot_general(a, b, (((2,), (2,)), ((0,), (0,))), preferred_element_type=F32)


def _mods_kernel(c_ref, w_ref, b_ref, o_ref):
    c = c_ref[...]
    c_act = c * jax.nn.sigmoid(c)
    o_ref[0] = jnp.dot(c_act, w_ref[0], preferred_element_type=F32, precision=HIGHEST) + b_ref[0]


def _mods(c, ada_w, ada_b):
    depth, d, n = ada_w.shape
    b = c.shape[0]
    tn = 2048 if n % 2048 == 0 else n
    out = pl.pallas_call(
        _mods_kernel,
        grid=(depth, n // tn),
        in_specs=[pl.BlockSpec((b, d), lambda i, j: (0, 0)),
                  pl.BlockSpec((1, d, tn), lambda i, j: (i, 0, j)),
                  pl.BlockSpec((1, 1, tn), lambda i, j: (i, 0, j))],
        out_specs=pl.BlockSpec((1, b, tn), lambda i, j: (i, 0, j)),
        out_shape=jax.ShapeDtypeStruct((depth, b, n), F32),
        compiler_params=_cparams(2),
        name="adaln_mods",
    )(c, ada_w, ada_b.reshape(depth, 1, n))
    return out.reshape(depth, b, N_MOD, d)


def _pool_kernel(x_ref, m_ref, pw_ref, ps_ref, lg_ref, lb_ref, o_ref, e1, ea, eb, *, ts, alpha):
    s = pl.program_id(1)
    d = x_ref.shape[-1]
    gw = d // len(POOL_WINDOWS)
    halo = POOL_HALO
    rows = halo + ts
    x = x_ref[0]
    sh, sc, gt = m_ref[0, 0:1, :], m_ref[0, 1:2, :], m_ref[0, 2:3, :]
    h = x * (1.0 + sc) + sh

    @pl.when(s == 0)
    def _():
        e1[0:halo, :] = jnp.zeros((halo, d), F32)

    e1[halo:rows, :] = h
    ea[8:rows, :] = e1[8:rows, :] + e1[7:rows - 1, :]
    eb[16:rows, gw:] = ea[16:rows, gw:] + ea[14:rows - 2, gw:]
    ea[24:rows, 2 * gw:] = eb[24:rows, 2 * gw:] + eb[20:rows - 4, 2 * gw:]
    eb[32:rows, 3 * gw:] = ea[32:rows, 3 * gw:] + ea[24:rows - 8, 3 * gw:]

    pos = s * ts + lax.broadcasted_iota(I32, (ts, 1), 0)
    outs = []
    for g, win in enumerate(POOL_WINDOWS):
        src = ea if g % 2 == 0 else eb
        cols = slice(g * gw, (g + 1) * gw)
        cnt = jnp.minimum(pos + 1, win).astype(F32)
        pooled = src[halo:rows, cols] / cnt - h[:, cols]
        outs.append(_dot(pooled.astype(BF16), pw_ref[g]))
    y = jnp.concatenate(outs, axis=1) * ps_ref[...]
    o_ref[0] = _layer_norm(alpha * x + (1.0 + gt) * y, lg_ref[...], lb_ref[...])
    e1[0:halo, :] = e1[ts:rows, :]


def _pool_layer(x, m, pool_w, pool_scale, ln_g, ln_b, alpha, cfg):
    b, s, d = x.shape
    ts = cfg["pool_ts"]
    g, gw, _ = pool_w.shape
    assert POOL_WINDOWS == (2, 4, 8, 16) and g == len(POOL_WINDOWS) and s % ts == 0 and ts >= POOL_HALO
    row = lambda v: v.reshape(1, d)
    return pl.pallas_call(
        functools.partial(_pool_kernel, ts=ts, alpha=alpha),
        grid=(b, s // ts),
        in_specs=[pl.BlockSpec((1, ts, d), lambda i, j: (i, j, 0)),
                  pl.BlockSpec((1, N_MOD, d), lambda i, j: (i, 0, 0)),
                  pl.BlockSpec((g, gw, gw), lambda i, j: (0, 0, 0)),
                  pl.BlockSpec((1, d), lambda i, j: (0, 0)),
                  pl.BlockSpec((1, d), lambda i, j: (0, 0)),
                  pl.BlockSpec((1, d), lambda i, j: (0, 0))],
        out_specs=pl.BlockSpec((1, ts, d), lambda i, j: (i, j, 0)),
        out_shape=jax.ShapeDtypeStruct((b, s, d), F32),
        scratch_shapes=[pltpu.VMEM((POOL_HALO + ts, d), F32)] * 3,
        compiler_params=_cparams(2),
        name="pool_layer",
    )(x, m, pool_w.astype(BF16), row(pool_scale), row(ln_g), row(ln_b))


def _dn_in_kernel(x_ref, m_ref, wm_ref, ws_ref, cw_ref, av_ref, dv_ref,
                  q_ref, k_ref, v_ref, z_ref, bg_ref, ext, *, ts, sr):
    s = pl.program_id(1)
    nh, dh = DN_HEADS, DN_HEAD_DIM
    w = nh * dh
    halo = CONV_HALO
    sh, sc = m_ref[0, 0:1, :], m_ref[0, 1:2, :]

    @pl.when(s == 0)
    def _():
        ext[0:halo, :] = jnp.zeros((halo, 3 * w), F32)

    def project(i):
        rows = slice(i * sr, (i + 1) * sr)
        h = (x_ref[0, rows, :] * (1.0 + sc) + sh).astype(BF16)
        proj = _dot(h, wm_ref[...])
        ext[halo + i * sr:halo + (i + 1) * sr, :] = proj[:, :3 * w]
        z_ref[0, rows, :] = proj[:, 3 * w:]
        small = _dot(h, ws_ref[...])
        lane = lax.broadcasted_iota(I32, small.shape, 1)
        beta = jax.nn.sigmoid(small)
        g = -jnp.exp(av_ref[...]) * jax.nn.softplus(small + dv_ref[...])
        bg_ref[0, rows, :] = jnp.where(lane < nh, beta, jnp.where(lane < 2 * nh, g, 0.0))

    def mix(i):
        rows = slice(i * sr, (i + 1) * sr)
        base = halo - (DN_CONV - 1) + i * sr
        conv = cw_ref[0:1, :] * ext[base:base + sr, :]
        for j in range(1, DN_CONV):
            conv = conv + cw_ref[j:j + 1, :] * ext[base + j:base + j + sr, :]
        act = conv * jax.nn.sigmoid(conv)
        for hd in range(nh):
            qh = act[:, hd * dh:(hd + 1) * dh]
            kh = act[:, w + hd * dh:w + (hd + 1) * dh]
            q_ref[0, hd, rows, :] = (qh * lax.rsqrt(jnp.sum(qh * qh, axis=-1, keepdims=True) + RMS_EPS)
                                     * (dh ** -0.5))
            k_ref[0, hd, rows, :] = kh * lax.rsqrt(jnp.sum(kh * kh, axis=-1, keepdims=True) + RMS_EPS)
            v_ref[0, hd, rows, :] = act[:, 2 * w + hd * dh:2 * w + (hd + 1) * dh]

    n_sub = ts // sr
    project(0)
    for i in range(1, n_sub):
        project(i)
        mix(i - 1)
    mix(n_sub - 1)
    ext[0:halo, :] = ext[ts:ts + halo, :]


def _dn_core_kernel(q_ref, k_ref, v_ref, bg_ref, o_ref,
                    st, bb, gcb, gt, gl_s, u_s, wq_s, ai_s, kd_s, *, ts):
    s = pl.program_id(1)
    nh, dh, c = DN_HEADS, DN_HEAD_DIM, DN_CHUNK
    nc = ts // c

    @pl.when(s == 0)
    def _():
        st[...] = jnp.zeros(st.shape, F32)

    bgv = bg_ref[0]
    lane = lax.broadcasted_iota(I32, bgv.shape, 1)
    rowc = lax.broadcasted_iota(I32, bgv.shape, 0) % c
    gc = jnp.where(lane >= nh, bgv, 0.0)
    shift = 1
    while shift < c:
        gc = gc + jnp.where(rowc >= shift, pltpu.roll(gc, shift, 0), 0.0)
        shift *= 2
    gc_t = gc.T
    for hd in range(nh):
        for ci in range(nc):
            gt[hd * nc + ci] = jnp.broadcast_to(gc_t[nh + hd:nh + hd + 1, ci * c:(ci + 1) * c], (SUBLANES, c))
        bb[hd] = jnp.broadcast_to(bgv[:, hd:hd + 1], (ts, LANES))
        gcb[hd] = jnp.broadcast_to(gc[:, nh + hd:nh + hd + 1], (ts, LANES))

    ri = lax.broadcasted_iota(I32, (c, c), 0)
    ci_ = lax.broadcasted_iota(I32, (c, c), 1)
    tril = ri >= ci_
    eye = (ri == ci_).astype(F32)
    off_masks = []
    blk = 1
    while blk < c:
        off_masks.append((ri // blk != ci_ // blk) & (ri // (2 * blk) == ci_ // (2 * blk)) & (ri > ci_))
        blk *= 2

    hg = DN_HEAD_GROUP
    nb = hg * nc
    n_pairs = nb // 2
    two = lambda msk: jnp.concatenate([msk, msk], axis=1)
    tril2, eye2, off2 = two(tril), two(eye), [two(msk) for msk in off_masks]

    def pair(a):
        a = a.reshape(n_pairs, 2, a.shape[1], a.shape[2])
        return jnp.concatenate([a[:, 0], a[:, 1]], axis=-1)

    def unpair(a):
        w = a.shape[-1] // 2
        return jnp.stack([a[..., :w], a[..., w:]], axis=1).reshape(nb, a.shape[1], w)

    def block_diag(a):
        left = lax.broadcasted_iota(I32, a.shape[1:], 1) < a.shape[-1] // 2
        zero = jnp.zeros_like(a)
        return jnp.concatenate([jnp.where(left, a, zero), jnp.where(left, zero, a)], axis=1)

    def group_body(gi, carry):
        heads = pl.ds(gi * hg, hg)
        chunks = lambda a: a.reshape(nb, c, a.shape[-1])
        per_head = lambda a: a.reshape(hg, ts, a.shape[-1])
        q, k, v = chunks(q_ref[0, heads]), chunks(k_ref[0, heads]), chunks(v_ref[0, heads])
        beta, gch = chunks(bb[heads]), chunks(gcb[heads])
        egh = jnp.exp(gch)
        g_last = jnp.broadcast_to(gch[:, c - 1:c, :], gch.shape)
        kb = k * beta
        vb = v * beta
        kd_s[heads] = per_head((k * jnp.exp(g_last - gch)).astype(BF16))
        gl_s[heads] = jnp.broadcast_to(egh[:, c - 1:c, :], (nb, SUBLANES, LANES)).reshape(hg, nc * SUBLANES, LANES)
        grow = gt[pl.ds(gi * nb, nb), 0:1, :]
        dec = jnp.where(tril2, jnp.exp(jnp.where(tril2, pair(gch) - pair(grow), 0.0)), 0.0)
        kq = jnp.concatenate([pair(kb), pair(q)], axis=1).astype(BF16)
        a_all = _bdot_nt(kq, block_diag(pair(k).astype(BF16)))
        ai_s[heads] = per_head(unpair((a_all[:, c:] * dec).astype(BF16)))
        l_mat = a_all[:, :c] * dec
        t_inv = eye2 - jnp.where(off2[0], l_mat, 0.0)
        for off in off2[1:]:
            tb = t_inv.astype(BF16)
            l_off = jnp.where(off, l_mat, 0.0).astype(BF16)
            t_inv = t_inv - _bdot(tb, block_diag(_bdot(l_off, block_diag(tb)).astype(BF16)))
        rhs = jnp.concatenate([vb, kb * egh], axis=2).astype(BF16)
        uw = unpair(_bdot(t_inv.astype(BF16), block_diag(pair(rhs))))
        u_s[heads] = per_head(uw[:, :, :dh])
        wq = jnp.concatenate([uw[:, :, dh:], q * egh], axis=1).astype(BF16)
        wq_s[heads] = wq.reshape(hg, 2 * ts, dh)
        return carry

    lax.fori_loop(0, nh // hg, group_body, 0)

    def chunk_body(ci, carry):
        rows = pl.ds(pl.multiple_of(ci * c, c), c)
        rows2 = pl.ds(pl.multiple_of(ci * 2 * c, 2 * c), 2 * c)
        heads = range(nh)
        states = [st[hd] for hd in heads]
        ws_qs = [_dot(wq_s[hd, rows2, :], states[hd].astype(BF16)) for hd in heads]
        v_nb = [(u_s[hd, rows, :] - ws_qs[hd][:c]).astype(BF16) for hd in heads]
        outs = [ws_qs[hd][c:] + _dot(ai_s[hd, rows, :], v_nb[hd]) for hd in heads]
        new_states = [states[hd] * gl_s[hd, pl.ds(pl.multiple_of(ci * SUBLANES, SUBLANES), 1), :]
                      + _dot_tn(kd_s[hd, rows, :], v_nb[hd]) for hd in heads]
        for hd in heads:
            o_ref[0, hd, rows, :] = outs[hd]
            st[hd] = new_states[hd]
        return carry

    lax.fori_loop(0, nc, chunk_body, 0)


def _dn_out_kernel(o_ref, z_ref, x_ref, m_ref, nw_ref, wo_ref, lg_ref, lb_ref, out_ref, *, alpha):
    nh, dh = DN_HEADS, DN_HEAD_DIM
    z = z_ref[0]
    parts = []
    for hd in range(nh):
        oh = o_ref[0, hd]
        on = oh * lax.rsqrt(jnp.mean(oh * oh, axis=-1, keepdims=True) + RMS_EPS) * nw_ref[...]
        zh = z[:, hd * dh:(hd + 1) * dh]
        parts.append((on * (zh * jax.nn.sigmoid(zh))).astype(BF16))
    y = _dot(jnp.concatenate(parts, axis=1), wo_ref[...])
    x = x_ref[0]
    gt = m_ref[0, 2:3, :]
    out_ref[0] = _layer_norm(alpha * x + (1.0 + gt) * y, lg_ref[...], lb_ref[...])


def _deltanet_layer(x, m, w_in, conv_w, a_log, dt_bias, norm_w, w_out, ln_g, ln_b, alpha, cfg):
    b, s, d = x.shape
    nh, dh = DN_HEADS, DN_HEAD_DIM
    w = nh * dh
    assert w_in.shape == (d, 4 * w + 2 * nh) and conv_w.shape == (DN_CONV, 3 * w) and 2 * nh <= LANES
    w_main = w_in[:, :4 * w].astype(BF16)
    w_small = jnp.pad(w_in[:, 4 * w:], ((0, 0), (0, LANES - 2 * nh))).astype(BF16)
    avec = jnp.zeros((1, LANES), F32).at[0, nh:2 * nh].set(a_log.astype(F32))
    dvec = jnp.zeros((1, LANES), F32).at[0, nh:2 * nh].set(dt_bias.astype(F32))
    row = lambda v: v.reshape(1, -1)
    const2 = lambda i, j: (0, 0)

    ts = cfg["dn_in_ts"]
    assert s % ts == 0 and ts >= CONV_HALO
    head_major = jax.ShapeDtypeStruct((b, nh, s, dh), F32)
    hm_spec = lambda t: pl.BlockSpec((1, nh, t, dh), lambda i, j: (i, 0, j, 0))
    q, k, v, z, bg = pl.pallas_call(
        functools.partial(_dn_in_kernel, ts=ts, sr=min(cfg["dn_in_sub"], ts)),
        grid=(b, s // ts),
        in_specs=[pl.BlockSpec((1, ts, d), lambda i, j: (i, j, 0)),
                  pl.BlockSpec((1, N_MOD, d), lambda i, j: (i, 0, 0)),
                  pl.BlockSpec((d, 4 * w), const2),
                  pl.BlockSpec((d, LANES), const2),
                  pl.BlockSpec((DN_CONV, 3 * w), const2),
                  pl.BlockSpec((1, LANES), const2),
                  pl.BlockSpec((1, LANES), const2)],
        out_specs=[hm_spec(ts), hm_spec(ts), hm_spec(ts),
                   pl.BlockSpec((1, ts, w), lambda i, j: (i, j, 0)),
                   pl.BlockSpec((1, ts, LANES), lambda i, j: (i, j, 0))],
        out_shape=[head_major, head_major, head_major,
                   jax.ShapeDtypeStruct((b, s, w), F32),
                   jax.ShapeDtypeStruct((b, s, LANES), F32)],
        scratch_shapes=[pltpu.VMEM((CONV_HALO + ts, 3 * w), F32)],
        compiler_params=_cparams(2),
        name="deltanet_in",
    )(x, m, w_main, w_small, conv_w, avec, dvec)

    ts = cfg["dn_core_ts"]
    assert s % ts == 0 and ts % DN_CHUNK == 0 and dh == DN_CHUNK
    per_head = pltpu.VMEM((nh, ts, dh), F32)
    o = pl.pallas_call(
        functools.partial(_dn_core_kernel, ts=ts),
        grid=(b, s // ts),
        in_specs=[hm_spec(ts), hm_spec(ts), hm_spec(ts),
                  pl.BlockSpec((1, ts, LANES), lambda i, j: (i, j, 0))],
        out_specs=hm_spec(ts),
        out_shape=head_major,
        scratch_shapes=[pltpu.VMEM((nh, dh, dh), F32),
                        per_head, per_head,
                        pltpu.VMEM((nh * (ts // DN_CHUNK), SUBLANES, DN_CHUNK), F32),
                        pltpu.VMEM((nh, ts // DN_CHUNK * SUBLANES, LANES), F32),
                        per_head,
                        pltpu.VMEM((nh, 2 * ts, dh), BF16),
                        pltpu.VMEM((nh, ts, DN_CHUNK), BF16),
                        pltpu.VMEM((nh, ts, dh), BF16)],
        compiler_params=_cparams(2),
        name="deltanet_core",
    )(q, k, v, bg)

    ts = cfg["dn_out_ts"]
    assert s % ts == 0
    return pl.pallas_call(
        functools.partial(_dn_out_kernel, alpha=alpha),
        grid=(b, s // ts),
        in_specs=[hm_spec(ts),
                  pl.BlockSpec((1, ts, w), lambda i, j: (i, j, 0)),
                  pl.BlockSpec((1, ts, d), lambda i, j: (i, j, 0)),
                  pl.BlockSpec((1, N_MOD, d), lambda i, j: (i, 0, 0)),
                  pl.BlockSpec((1, dh), const2),
                  pl.BlockSpec((w, d), const2),
                  pl.BlockSpec((1, d), const2),
                  pl.BlockSpec((1, d), const2)],
        out_specs=pl.BlockSpec((1, ts, d), lambda i, j: (i, j, 0)),
        out_shape=jax.ShapeDtypeStruct((b, s, d), F32),
        compiler_params=_cparams(2),
        name="deltanet_out",
    )(o, z, x, m, row(norm_w), w_out.astype(BF16), row(ln_g), row(ln_b))


def _pack_row_chunks(v):
    half = v.shape[-1] // 2
    bits = pltpu.bitcast(v.astype(BF16).astype(F32), U32)
    packed = (bits[:, :half] >> 16) | (bits[:, half:] & jnp.uint32(0xFFFF0000))
    return [packed[:, i * LANES:(i + 1) * LANES] for i in range(half // LANES)]


def _unpack_row_chunks(chunks):
    lo = [pltpu.bitcast(c << 16, F32) for c in chunks]
    hi = [pltpu.bitcast(c & jnp.uint32(0xFFFF0000), F32) for c in chunks]
    return jnp.concatenate(lo + hi, axis=1)


def _store_row_chunks(ref, chunks):
    for i, ch in enumerate(chunks):
        ref[i] = ch


def _load_row_chunks(ref, lead=()):
    return [ref[(*lead, i)] for i in range(ref.shape[len(lead)])]


def _sc_mesh():
    return plsc.VectorSubcoreMesh(core_axis_name="c", subcore_axis_name="s")


def _sc_gather_rows(table, idx):
    n = idx.shape[1]
    assert table.shape[1] == LANES and idx.shape[0] == 1 and n % (SC_WINDOW * SC_WORKERS) == 0

    @functools.partial(pl.kernel, out_type=jax.ShapeDtypeStruct((n, LANES), table.dtype), mesh=_sc_mesh(),
                       name="sc_gather_rows")
    def gather(table_hbm, idx_hbm, out_hbm):
        def body(idx_vmem, out_vmem):
            pltpu.sync_copy(table_hbm.at[idx_vmem.at[0]], out_vmem)

        pltpu.emit_pipeline(
            body,
            grid=(n // SC_WINDOW,),
            in_specs=[pl.BlockSpec((1, SC_WINDOW), lambda i: (0, i))],
            out_specs=[pl.BlockSpec((SC_WINDOW, LANES), lambda i: (i, 0))],
            core_axis_name=("c", "s"),
            dimension_semantics=(pltpu.PARALLEL,),
        )(idx_hbm, out_hbm)

    return gather(table, idx)


def _sc_scatter_rows(x, idx_list, n_out):
    n = x.shape[0]
    assert x.shape[1] == LANES and n % (SC_WINDOW * SC_WORKERS) == 0
    assert all(idx.shape == (1, n) for idx in idx_list)

    @functools.partial(pl.kernel, out_type=jax.ShapeDtypeStruct((n_out, LANES), x.dtype), mesh=_sc_mesh(),
                       name="sc_scatter_rows")
    def scatter(x_hbm, *refs):
        idx_hbms, out_hbm = refs[:-1], refs[-1]

        def body(x_vmem, *idx_vmems):
            for idx_vmem in idx_vmems:
                pltpu.sync_copy(x_vmem, out_hbm.at[idx_vmem.at[0]])

        pltpu.emit_pipeline(
            body,
            grid=(n // SC_WINDOW,),
            in_specs=[pl.BlockSpec((SC_WINDOW, LANES), lambda i: (i, 0))]
                     + [pl.BlockSpec((1, SC_WINDOW), lambda i: (0, i))] * len(idx_list),
            out_specs=[],
            core_axis_name=("c", "s"),
            dimension_semantics=(pltpu.PARALLEL,),
        )(x_hbm, *idx_hbms)

    return scatter(x, *idx_list)


def _router_kernel(x_ref, m_ref, rwt_ref, rb_ref, tri_ref,
                   hp_ref, idx_ref, p_ref, rank_ref, cnt_ref, run, *, tt):
    i = pl.program_id(0)

    @pl.when(i == 0)
    def _():
        run[...] = jnp.zeros(run.shape, F32)

    x = x_ref[...]
    d = x.shape[-1]
    sh, sc = m_ref[0, 3:4, :], m_ref[0, 4:5, :]
    h = x * (1.0 + sc) + sh
    _store_row_chunks(hp_ref, _pack_row_chunks(h))

    h_hi = h.astype(BF16)
    h_lo = (h - h_hi.astype(F32)).astype(BF16)
    rw = rwt_ref[...]
    w_hi = rw.astype(BF16)
    w_lo = (rw - w_hi.astype(F32)).astype(BF16)
    ne = rw.shape[0]
    hi_terms = _dot_nt(jnp.concatenate([w_hi, w_lo], axis=0), h_hi)
    logits = hi_terms[:ne] + hi_terms[ne:] + _dot_nt(w_hi, h_lo) + rb_ref[...]
    eio = lax.broadcasted_iota(I32, logits.shape, 0).astype(F32)
    vals, idxs, sels = [], [], []
    for _ in range(TOP_K):
        mx = jnp.max(logits, axis=0, keepdims=True)
        ix = jnp.min(jnp.where(logits == mx, eio, float(ne)), axis=0, keepdims=True)
        sel = eio == ix
        logits = jnp.where(sel, -jnp.inf, logits)
        vals.append(mx)
        idxs.append(ix)
        sels.append(sel)
    exps = [jnp.exp(v - vals[0]) for v in vals]
    den = functools.reduce(lambda a, b_: a + b_, exps)
    chosen = functools.reduce(jnp.logical_or, sels)
    onehot = jnp.where(chosen, 1.0, 0.0)
    before = _dot(onehot.astype(BF16), tri_ref[...]) + run[...]
    ranks = [jnp.sum(jnp.where(sel, before, 0.0), axis=0, keepdims=True) for sel in sels]
    run[...] = run[...] + jnp.sum(onehot, axis=1, keepdims=True)
    idx_ref[...] = jnp.concatenate(idxs, axis=0).astype(I32)
    p_ref[...] = jnp.concatenate([e / den for e in exps]
                                 + [jnp.zeros((p_ref.shape[0] - TOP_K, den.shape[1]), F32)], axis=0)
    rank_ref[...] = jnp.concatenate(ranks, axis=0).astype(I32)
    cnt_ref[...] = jnp.broadcast_to(run[...], cnt_ref.shape)


def _experts_kernel(be_ref, nu_ref, nx_ref, par_ref, xs_ref, wgu_hbm, bgu_ref, wd_hbm, bd_ref, y_ref,
                    wgu_f, wd_f, sems, wgu_b, wd_b, *, layer):
    j = pl.program_id(0)

    def weight_copies(expert, slot):
        return (pltpu.make_async_copy(wgu_hbm.at[layer, expert], wgu_f.at[slot], sems.at[0, slot]),
                pltpu.make_async_copy(wd_hbm.at[layer, expert], wd_f.at[slot], sems.at[1, slot]))

    @pl.when(j >= nu_ref[0])
    def _():
        y_ref[...] = jnp.zeros(y_ref.shape, U32)

    @pl.when(j < nu_ref[0])
    def _():
        expert, slot = be_ref[j], par_ref[j]
        new_expert = jnp.logical_or(j == 0, expert != be_ref[jnp.maximum(j - 1, 0)])

        @pl.when(j == 0)
        def _():
            for cp in weight_copies(expert, slot):
                cp.start()

        @pl.when(new_expert)
        def _():
            for cp in weight_copies(expert, slot):
                cp.wait()

            @pl.when(nx_ref[j] >= 0)
            def _():
                for cp in weight_copies(nx_ref[j], 1 - slot):
                    cp.start()

            wgu_b[...] = wgu_f[slot].astype(BF16)
            wd_b[...] = wd_f[slot].astype(BF16)

        xb = _unpack_row_chunks(_load_row_chunks(xs_ref)).astype(BF16)
        gu = _dot(xb, wgu_b[...]) + bgu_ref[0, 0]
        f = gu.shape[1] // 2
        glu = jnp.minimum(gu[:, :f], SWIGLU_LIMIT)
        lin = jnp.clip(gu[:, f:], -SWIGLU_LIMIT, SWIGLU_LIMIT)
        act = glu * jax.nn.sigmoid(SWIGLU_ALPHA * glu) * (lin + 1.0)
        y = _dot(act.astype(BF16), wd_b[...]) + bd_ref[0, 0]
        _store_row_chunks(y_ref, _pack_row_chunks(y))


def _combine_kernel(ya_ref, x_ref, m_ref, p_ref, lg_ref, lb_ref, *rest, alpha):
    o_ref = rest[-1]
    p_rows = p_ref[...]
    eye = (lax.broadcasted_iota(I32, (p_rows.shape[0], LANES), 0)
           == lax.broadcasted_iota(I32, (p_rows.shape[0], LANES), 1)).astype(F32)
    p = lax.dot_general(p_rows, eye, (((0,), (0,)), ((), ())), preferred_element_type=F32,
                        precision=HIGHEST)
    y = None
    for kk in range(TOP_K):
        yk = p[:, kk:kk + 1] * _unpack_row_chunks(_load_row_chunks(ya_ref, (kk,)))
        y = yk if y is None else y + yk
    gt = m_ref[0, 5:6, :]
    o_ref[...] = _layer_norm(alpha * x_ref[...] + (1.0 + gt) * y, lg_ref[...], lb_ref[...])


def _moe_layer(x, m, layer, router_w, router_b, w_gu, b_gu, w_down, b_down, ln_g, ln_b, alpha, cfg):
    b, s, d = x.shape
    n_tok = b * s
    ne = router_w.shape[-1]
    f2 = w_gu.shape[-1]
    ff = f2 // 2
    assert d % (2 * LANES) == 0 and w_down.shape[-2:] == (ff, d)
    rc = d // (2 * LANES)
    xf = x.reshape(n_tok, d)
    const2 = lambda i: (0, 0)

    tt = cfg["router_tt"]
    assert s % tt == 0
    tri = jnp.triu(jnp.ones((tt, tt), BF16), k=1)
    hp, idx, probs, rank, cnt = pl.pallas_call(
        functools.partial(_router_kernel, tt=tt),
        grid=(n_tok // tt,),
        in_specs=[pl.BlockSpec((tt, d), lambda i: (i, 0)),
                  pl.BlockSpec((1, N_MOD, d), lambda i: (i * tt // s, 0, 0)),
                  pl.BlockSpec((ne, d), const2),
                  pl.BlockSpec((ne, 1), const2),
                  pl.BlockSpec((tt, tt), const2)],
        out_specs=[pl.BlockSpec((rc, tt, LANES), lambda i: (0, i, 0)),
                   pl.BlockSpec((TOP_K, tt), lambda i: (0, i)),
                   pl.BlockSpec((SUBLANES, tt), lambda i: (0, i)),
                   pl.BlockSpec((TOP_K, tt), lambda i: (0, i)),
                   pl.BlockSpec((ne, LANES), const2)],
        out_shape=[jax.ShapeDtypeStruct((rc, n_tok, LANES), U32),
                   jax.ShapeDtypeStruct((TOP_K, n_tok), I32),
                   jax.ShapeDtypeStruct((SUBLANES, n_tok), F32),
                   jax.ShapeDtypeStruct((TOP_K, n_tok), I32),
                   jax.ShapeDtypeStruct((ne, LANES), F32)],
        scratch_shapes=[pltpu.VMEM((ne, 1), F32)],
        compiler_params=_cparams(1),
        name="moe_router",
    )(xf, m, router_w[layer].T, router_b[layer].reshape(ne, 1), tri)

    bm = cfg["expert_bm"]
    cap = n_tok * TOP_K + ne * bm
    nb = cap // bm
    counts = cnt[:, 0].astype(I32)
    padded = (counts + bm - 1) // bm * bm
    pad_end = jnp.cumsum(padded)
    pad_start = pad_end - padded
    eids = jnp.arange(ne, dtype=I32)[:, None, None]
    dest = rank + jnp.sum(jnp.where(idx[None] == eids, pad_start[:, None, None], 0), axis=0)
    block_start = jnp.arange(nb, dtype=I32) * bm
    block_expert = jnp.minimum(jnp.sum(block_start[None, :] >= pad_end[:, None], axis=0), ne - 1).astype(I32)
    n_used = (pad_end[-1:] // bm).astype(I32)
    first_of_expert = jnp.concatenate([jnp.ones((1,), bool), block_expert[1:] != block_expert[:-1]])
    weight_slot = ((jnp.cumsum(first_of_expert.astype(I32)) - 1) % 2).astype(I32)
    next_block = pad_end[block_expert] // bm
    next_expert = jnp.where(next_block < n_used[0], block_expert[jnp.minimum(next_block, nb - 1)], -1).astype(I32)

    plane_offset = (jnp.arange(rc * n_tok, dtype=I32) // n_tok * cap)[None, :]
    slot_rows = [jnp.tile(dest[kk:kk + 1], (1, rc)) + plane_offset for kk in range(TOP_K)]

    xs = _sc_scatter_rows(hp.reshape(rc * n_tok, LANES), slot_rows, rc * cap).reshape(rc, cap, LANES)

    last = lambda j, be, nu, *_: jnp.minimum(j, nu[0] - 1)
    bias_map = lambda j, be, nu, *_: (layer, be[last(j, be, nu)], 0, 0)
    y_rows = pl.pallas_call(
        functools.partial(_experts_kernel, layer=layer),
        grid_spec=pltpu.PrefetchScalarGridSpec(
            num_scalar_prefetch=4,
            grid=(nb,),
            in_specs=[pl.BlockSpec((rc, bm, LANES), lambda j, be, nu, *_: (0, last(j, be, nu), 0)),
                      pl.BlockSpec(memory_space=pl.ANY),
                      pl.BlockSpec((1, 1, 1, f2), bias_map),
                      pl.BlockSpec(memory_space=pl.ANY),
                      pl.BlockSpec((1, 1, 1, d), bias_map)],
            out_specs=pl.BlockSpec((rc, bm, LANES), lambda j, *_: (0, j, 0)),
            scratch_shapes=[pltpu.VMEM((2, d, f2), F32), pltpu.VMEM((2, ff, d), F32),
                            pltpu.SemaphoreType.DMA((2, 2)),
                            pltpu.VMEM((d, f2), BF16), pltpu.VMEM((ff, d), BF16)]),
        out_shape=jax.ShapeDtypeStruct((rc, cap, LANES), U32),
        compiler_params=_cparams(1),
        name="moe_experts",
    )(block_expert, n_used, next_expert, weight_slot, xs, w_gu, b_gu.reshape(*b_gu.shape[:2], 1, f2),
      w_down, b_down.reshape(*b_down.shape[:2], 1, d))

    tt = cfg["combine_tt"]
    n_groups = cfg["combine_groups"] if n_tok % (cfg["combine_groups"] * max(tt, SC_WINDOW * SC_WORKERS)) == 0 else 1
    n_grp = n_tok // n_groups
    tiles = n_grp // tt
    assert s % tt == 0 and n_grp % tt == 0
    y_table = y_rows.reshape(rc * cap, LANES)
    grp_rows = (dest.reshape(TOP_K, n_groups, n_grp).transpose(1, 0, 2)[:, :, None, :]
                + (jnp.arange(rc, dtype=I32) * cap)[None, None, :, None])
    out = None
    for gi in range(n_groups):
        rows_g = grp_rows[gi].reshape(1, TOP_K * rc * n_grp)
        y_assign = _sc_gather_rows(y_table, rows_g).reshape(TOP_K, rc, n_grp, LANES)
        first = gi * tiles
        operands = [y_assign, xf, m, probs, ln_g.reshape(1, d), ln_b.reshape(1, d)]
        in_specs = [pl.BlockSpec((TOP_K, rc, tt, LANES), lambda i: (0, 0, i, 0)),
                    pl.BlockSpec((tt, d), lambda i, first=first: (first + i, 0)),
                    pl.BlockSpec((1, N_MOD, d), lambda i, first=first: ((first + i) * tt // s, 0, 0)),
                    pl.BlockSpec((SUBLANES, tt), lambda i, first=first: (0, first + i)),
                    pl.BlockSpec((1, d), const2),
                    pl.BlockSpec((1, d), const2)]
        aliases = {}
        if out is not None:
            operands.append(out)
            in_specs.append(pl.BlockSpec(memory_space=pl.ANY))
            aliases = {len(operands) - 1: 0}
        out = pl.pallas_call(
            functools.partial(_combine_kernel, alpha=alpha),
            grid=(tiles,),
            in_specs=in_specs,
            out_specs=pl.BlockSpec((tt, d), lambda i, first=first: (first + i, 0)),
            out_shape=jax.ShapeDtypeStruct((n_tok, d), F32),
            input_output_aliases=aliases,
            compiler_params=_cparams(1),
            name="moe_combine",
        )(*operands)
    return out.reshape(b, s, d)


def kernel(x, c, ada_w, ada_b, ln_g, ln_b, pool_w, pool_scale, dn_w_in, dn_conv_w, dn_a_log, dn_dt_bias,
           dn_norm_w, dn_w_out, router_w, router_b, exp_w_gu, exp_b_gu, exp_w_down, exp_b_down):
    b, s, d = x.shape
    depth = ada_w.shape[0]
    alpha = (2 * depth) ** 0.25
    cfg = _tile_config(s, b * s)
    mods = _mods(c, ada_w, ada_b)
    n_mixers = 2
    for i in range(depth):
        m = mods[i]
        j = i // n_mixers
        if i % n_mixers == 0:
            x = _pool_layer(x, m, pool_w[j], pool_scale[j], ln_g[i, 0], ln_b[i, 0], alpha, cfg)
        else:
            x = _deltanet_layer(x, m, dn_w_in[j], dn_conv_w[j], dn_a_log[j], dn_dt_bias[j], dn_norm_w[j],
                                dn_w_out[j], ln_g[i, 0], ln_b[i, 0], alpha, cfg)
        x = _moe_layer(x, m, i, router_w, router_b, exp_w_gu, exp_b_gu, exp_w_down, exp_b_down,
                       ln_g[i, 1], ln_b[i, 1], alpha, cfg)
    return x
```

```python
import functools

import jax
import jax.numpy as jnp
from jax import lax
from jax.experimental import pallas as pl
from jax.experimental.pallas import tpu as pltpu
from jax.experimental.pallas import tpu_sc as plsc

F32 = jnp.float32
BF16 = jnp.bfloat16
I32 = jnp.int32
U32 = jnp.uint32

N_MOD = 6
POOL_WINDOWS = (2, 4, 8, 16)
DN_HEADS = 8
DN_HEAD_DIM = 128
DN_CONV = 4
TOP_K = 4
SWIGLU_LIMIT = 7.0
SWIGLU_ALPHA = 1.702
LN_EPS = 1e-5
RMS_EPS = 1e-6

LANES = 128
SUBLANES = 8
VMEM_LIMIT_BYTES = 56 * 1024 * 1024
SC_WORKERS = 32
SC_WINDOW = 128

DN_CHUNK = LANES
DN_HEAD_GROUP = 8
POOL_HALO = 32
CONV_HALO = SUBLANES

HIGHEST = lax.Precision.HIGHEST


def _tile_config(seq, n_tok):
    return dict(
        pool_ts=min(512, seq),
        dn_in_ts=min(512, seq),
        dn_in_sub=128,
        dn_core_ts=min(512, seq),
        dn_out_ts=min(512, seq),
        router_tt=min(512, seq),
        expert_bm=512,
        combine_tt=min(256, seq),
        combine_groups=4,
    )


def _cparams(n_axes):
    return pltpu.CompilerParams(dimension_semantics=("arbitrary",) * n_axes,
                                vmem_limit_bytes=VMEM_LIMIT_BYTES)


def _layer_norm(v, g, b):
    mu = jnp.mean(v, axis=-1, keepdims=True)
    d = v - mu
    var = jnp.mean(d * d, axis=-1, keepdims=True)
    return d * lax.rsqrt(var + LN_EPS) * g + b


def _dot(a, b):
    return jnp.dot(a, b, preferred_element_type=F32)


def _dot_nt(a, b):
    return lax.dot_general(a, b, (((1,), (1,)), ((), ())), preferred_element_type=F32)


def _dot_tn(a, b):
    return lax.dot_general(a, b, (((0,), (0,)), ((), ())), preferred_element_type=F32)


def _bdot(a, b):
    return lax.dot_general(a, b, (((2,), (1,)), ((0,), (0,))), preferred_element_type=F32)


def _bdot_nt(a, b):
    return lax.dot_general(a, b, (((2,), (2,)), ((0,), (0,))), preferred_element_type=F32)


def _mods_kernel(c_ref, w_ref, b_ref, o_ref):
    c = c_ref[...]
    c_act = c * jax.nn.sigmoid(c)
    o_ref[0] = jnp.dot(c_act, w_ref[0], preferred_element_type=F32, precision=HIGHEST) + b_ref[0]


def _mods(c, ada_w, ada_b):
    depth, d, n = ada_w.shape
    b = c.shape[0]
    tn = 2048 if n % 2048 == 0 else n
    out = pl.pallas_call(
        _mods_kernel,
        grid=(depth, n // tn),
        in_specs=[pl.BlockSpec((b, d), lambda i, j: (0, 0)),
                  pl.BlockSpec((1, d, tn), lambda i, j: (i, 0, j)),
                  pl.BlockSpec((1, 1, tn), lambda i, j: (i, 0, j))],
        out_specs=pl.BlockSpec((1, b, tn), lambda i, j: (i, 0, j)),
        out_shape=jax.ShapeDtypeStruct((depth, b, n), F32),
        compiler_params=_cparams(2),
        name="adaln_mods",
    )(c, ada_w, ada_b.reshape(depth, 1, n))
    return out.reshape(depth, b, N_MOD, d)


def _pool_kernel(x_ref, m_ref, pw_ref, ps_ref, lg_ref, lb_ref, o_ref, e1, ea, eb, *, ts, alpha):
    s = pl.program_id(1)
    d = x_ref.shape[-1]
    gw = d // len(POOL_WINDOWS)
    halo = POOL_HALO
    rows = halo + ts
    x = x_ref[0]
    sh, sc, gt = m_ref[0, 0:1, :], m_ref[0, 1:2, :], m_ref[0, 2:3, :]
    h = x * (1.0 + sc) + sh

    @pl.when(s == 0)
    def _():
        e1[0:halo, :] = jnp.zeros((halo, d), F32)

    e1[halo:rows, :] = h
    ea[8:rows, :] = e1[8:rows, :] + e1[7:rows - 1, :]
    eb[16:rows, gw:] = ea[16:rows, gw:] + ea[14:rows - 2, gw:]
    ea[24:rows, 2 * gw:] = eb[24:rows, 2 * gw:] + eb[20:rows - 4, 2 * gw:]
    eb[32:rows, 3 * gw:] = ea[32:rows, 3 * gw:] + ea[24:rows - 8, 3 * gw:]

    pos = s * ts + lax.broadcasted_iota(I32, (ts, 1), 0)
    outs = []
    for g, win in enumerate(POOL_WINDOWS):
        src = ea if g % 2 == 0 else eb
        cols = slice(g * gw, (g + 1) * gw)
        cnt = jnp.minimum(pos + 1, win).astype(F32)
        pooled = src[halo:rows, cols] / cnt - h[:, cols]
        outs.append(_dot(pooled.astype(BF16), pw_ref[g]))
    y = jnp.concatenate(outs, axis=1) * ps_ref[...]
    o_ref[0] = _layer_norm(alpha * x + (1.0 + gt) * y, lg_ref[...], lb_ref[...])
    e1[0:halo, :] = e1[ts:rows, :]


def _pool_layer(x, m, pool_w, pool_scale, ln_g, ln_b, alpha, cfg):
    b, s, d = x.shape
    ts = cfg["pool_ts"]
    g, gw, _ = pool_w.shape
    assert POOL_WINDOWS == (2, 4, 8, 16) and g == len(POOL_WINDOWS) and s % ts == 0 and ts >= POOL_HALO
    row = lambda v: v.reshape(1, d)
    return pl.pallas_call(
        functools.partial(_pool_kernel, ts=ts, alpha=alpha),
        grid=(b, s // ts),
        in_specs=[pl.BlockSpec((1, ts, d), lambda i, j: (i, j, 0)),
                  pl.BlockSpec((1, N_MOD, d), lambda i, j: (i, 0, 0)),
                  pl.BlockSpec((g, gw, gw), lambda i, j: (0, 0, 0)),
                  pl.BlockSpec((1, d), lambda i, j: (0, 0)),
                  pl.BlockSpec((1, d), lambda i, j: (0, 0)),
                  pl.BlockSpec((1, d), lambda i, j: (0, 0))],
        out_specs=pl.BlockSpec((1, ts, d), lambda i, j: (i, j, 0)),
        out_shape=jax.ShapeDtypeStruct((b, s, d), F32),
        scratch_shapes=[pltpu.VMEM((POOL_HALO + ts, d), F32)] * 3,
        compiler_params=_cparams(2),
        name="pool_layer",
    )(x, m, pool_w.astype(BF16), row(pool_scale), row(ln_g), row(ln_b))


def _dn_in_kernel(x_ref, m_ref, wm_ref, ws_ref, cw_ref, av_ref, dv_ref,
                  q_ref, k_ref, v_ref, z_ref, bg_ref, ext, *, ts, sr):
    s = pl.program_id(1)
    nh, dh = DN_HEADS, DN_HEAD_DIM
    w = nh * dh
    halo = CONV_HALO
    sh, sc = m_ref[0, 0:1, :], m_ref[0, 1:2, :]

    @pl.when(s == 0)
    def _():
        ext[0:halo, :] = jnp.zeros((halo, 3 * w), F32)

    def project(i):
        rows = slice(i * sr, (i + 1) * sr)
        h = (x_ref[0, rows, :] * (1.0 + sc) + sh).astype(BF16)
        proj = _dot(h, wm_ref[...])
        ext[halo + i * sr:halo + (i + 1) * sr, :] = proj[:, :3 * w]
        z_ref[0, rows, :] = proj[:, 3 * w:]
        small = _dot(h, ws_ref[...])
        lane = lax.broadcasted_iota(I32, small.shape, 1)
        beta = jax.nn.sigmoid(small)
        g = -jnp.exp(av_ref[...]) * jax.nn.softplus(small + dv_ref[...])
        bg_ref[0, rows, :] = jnp.where(lane < nh, beta, jnp.where(lane < 2 * nh, g, 0.0))

    def mix(i):
        rows = slice(i * sr, (i + 1) * sr)
        base = halo - (DN_CONV - 1) + i * sr
        conv = cw_ref[0:1, :] * ext[base:base + sr, :]
        for j in range(1, DN_CONV):
            conv = conv + cw_ref[j:j + 1, :] * ext[base + j:base + j + sr, :]
        act = conv * jax.nn.sigmoid(conv)
        for hd in range(nh):
            qh = act[:, hd * dh:(hd + 1) * dh]
            kh = act[:, w + hd * dh:w + (hd + 1) * dh]
            q_ref[0, hd, rows, :] = (qh * lax.rsqrt(jnp.sum(qh * qh, axis=-1, keepdims=True) + RMS_EPS)
                                     * (dh ** -0.5))
            k_ref[0, hd, rows, :] = kh * lax.rsqrt(jnp.sum(kh * kh, axis=-1, keepdims=True) + RMS_EPS)
            v_ref[0, hd, rows, :] = act[:, 2 * w + hd * dh:2 * w + (hd + 1) * dh]

    n_sub = ts // sr
    project(0)
    for i in range(1, n_sub):
        project(i)
        mix(i - 1)
    mix(n_sub - 1)
    ext[0:halo, :] = ext[ts:ts + halo, :]


def _dn_core_kernel(q_ref, k_ref, v_ref, bg_ref, o_ref,
                    st, bb, gcb, gt, gl_s, u_s, wq_s, ai_s, kd_s, *, ts):
    s = pl.program_id(1)
    nh, dh, c = DN_HEADS, DN_HEAD_DIM, DN_CHUNK
    nc = ts // c

    @pl.when(s == 0)
    def _():
        st[...] = jnp.zeros(st.shape, F32)

    bgv = bg_ref[0]
    lane = lax.broadcasted_iota(I32, bgv.shape, 1)
    rowc = lax.broadcasted_iota(I32, bgv.shape, 0) % c
    gc = jnp.where(lane >= nh, bgv, 0.0)
    shift = 1
    while shift < c:
        gc = gc + jnp.where(rowc >= shift, pltpu.roll(gc, shift, 0), 0.0)
        shift *= 2
    gc_t = gc.T
    for hd in range(nh):
        for ci in range(nc):
            gt[hd * nc + ci] = jnp.broadcast_to(gc_t[nh + hd:nh + hd + 1, ci * c:(ci + 1) * c], (SUBLANES, c))
        bb[hd] = jnp.broadcast_to(bgv[:, hd:hd + 1], (ts, LANES))
        gcb[hd] = jnp.broadcast_to(gc[:, nh + hd:nh + hd + 1], (ts, LANES))

    ri = lax.broadcasted_iota(I32, (c, c), 0)
    ci_ = lax.broadcasted_iota(I32, (c, c), 1)
    tril = ri >= ci_
    eye = (ri == ci_).astype(F32)
    off_masks = []
    blk = 1
    while blk < c:
        off_masks.append((ri // blk != ci_ // blk) & (ri // (2 * blk) == ci_ // (2 * blk)) & (ri > ci_))
        blk *= 2

    hg = DN_HEAD_GROUP
    nb = hg * nc
    n_pairs = nb // 2
    two = lambda msk: jnp.concatenate([msk, msk], axis=1)
    tril2, eye2, off2 = two(tril), two(eye), [two(msk) for msk in off_masks]

    def pair(a):
        a = a.reshape(n_pairs, 2, a.shape[1], a.shape[2])
        return jnp.concatenate([a[:, 0], a[:, 1]], axis=-1)

    def unpair(a):
        w = a.shape[-1] // 2
        return jnp.stack([a[..., :w], a[..., w:]], axis=1).reshape(nb, a.shape[1], w)

    def block_diag(a):
        left = lax.broadcasted_iota(I32, a.shape[1:], 1) < a.shape[-1] // 2
        zero = jnp.zeros_like(a)
        return jnp.concatenate([jnp.where(left, a, zero), jnp.where(left, zero, a)], axis=1)

    def group_body(gi, carry):
        heads = pl.ds(gi * hg, hg)
        chunks = lambda a: a.reshape(nb, c, a.shape[-1])
        per_head = lambda a: a.reshape(hg, ts, a.shape[-1])
        q, k, v = chunks(q_ref[0, heads]), chunks(k_ref[0, heads]), chunks(v_ref[0, heads])
        beta, gch = chunks(bb[heads]), chunks(gcb[heads])
        egh = jnp.exp(gch)
        g_last = jnp.broadcast_to(gch[:, c - 1:c, :], gch.shape)
        kb = k * beta
        vb = v * beta
        kd_s[heads] = per_head((k * jnp.exp(g_last - gch)).astype(BF16))
        gl_s[heads] = jnp.broadcast_to(egh[:, c - 1:c, :], (nb, SUBLANES, LANES)).reshape(hg, nc * SUBLANES, LANES)
        grow = gt[pl.ds(gi * nb, nb), 0:1, :]
        dec = jnp.where(tril2, jnp.exp(jnp.where(tril2, pair(gch) - pair(grow), 0.0)), 0.0)
        kq = jnp.concatenate([pair(kb), pair(q)], axis=1).astype(BF16)
        a_all = _bdot_nt(kq, block_diag(pair(k).astype(BF16)))
        ai_s[heads] = per_head(unpair((a_all[:, c:] * dec).astype(BF16)))
        l_mat = a_all[:, :c] * dec
        t_inv = eye2 - jnp.where(off2[0], l_mat, 0.0)
        for off in off2[1:]:
            tb = t_inv.astype(BF16)
            l_off = jnp.where(off, l_mat, 0.0).astype(BF16)
            t_inv = t_inv - _bdot(tb, block_diag(_bdot(l_off, block_diag(tb)).astype(BF16)))
        rhs = jnp.concatenate([vb, kb * egh], axis=2).astype(BF16)
        uw = unpair(_bdot(t_inv.astype(BF16), block_diag(pair(rhs))))
        u_s[heads] = per_head(uw[:, :, :dh])
        wq = jnp.concatenate([uw[:, :, dh:], q * egh], axis=1).astype(BF16)
        wq_s[heads] = wq.reshape(hg, 2 * ts, dh)
        return carry

    lax.fori_loop(0, nh // hg, group_body, 0)

    def chunk_body(ci, carry):
        rows = pl.ds(pl.multiple_of(ci * c, c), c)
        rows2 = pl.ds(pl.multiple_of(ci * 2 * c, 2 * c), 2 * c)
        heads = range(nh)
        states = [st[hd] for hd in heads]
        ws_qs = [_dot(wq_s[hd, rows2, :], states[hd].astype(BF16)) for hd in heads]
        v_nb = [(u_s[hd, rows, :] - ws_qs[hd][:c]).astype(BF16) for hd in heads]
        outs = [ws_qs[hd][c:] + _dot(ai_s[hd, rows, :], v_nb[hd]) for hd in heads]
        new_states = [states[hd] * gl_s[hd, pl.ds(pl.multiple_of(ci * SUBLANES, SUBLANES), 1), :]
                      + _dot_tn(kd_s[hd, rows, :], v_nb[hd]) for hd in heads]
        for hd in heads:
            o_ref[0, hd, rows, :] = outs[hd]
            st[hd] = new_states[hd]
        return carry

    lax.fori_loop(0, nc, chunk_body, 0)


def _dn_out_kernel(o_ref, z_ref, x_ref, m_ref, nw_ref, wo_ref, lg_ref, lb_ref, out_ref, *, alpha):
    nh, dh = DN_HEADS, DN_HEAD_DIM
    z = z_ref[0]
    parts = []
    for hd in range(nh):
        oh = o_ref[0, hd]
        on = oh * lax.rsqrt(jnp.mean(oh * oh, axis=-1, keepdims=True) + RMS_EPS) * nw_ref[...]
        zh = z[:, hd * dh:(hd + 1) * dh]
        parts.append((on * (zh * jax.nn.sigmoid(zh))).astype(BF16))
    y = _dot(jnp.concatenate(parts, axis=1), wo_ref[...])
    x = x_ref[0]
    gt = m_ref[0, 2:3, :]
    out_ref[0] = _layer_norm(alpha * x + (1.0 + gt) * y, lg_ref[...], lb_ref[...])


def _deltanet_layer(x, m, w_in, conv_w, a_log, dt_bias, norm_w, w_out, ln_g, ln_b, alpha, cfg):
    b, s, d = x.shape
    nh, dh = DN_HEADS, DN_HEAD_DIM
    w = nh * dh
    assert w_in.shape == (d, 4 * w + 2 * nh) and conv_w.shape == (DN_CONV, 3 * w) and 2 * nh <= LANES
    w_main = w_in[:, :4 * w].astype(BF16)
    w_small = jnp.pad(w_in[:, 4 * w:], ((0, 0), (0, LANES - 2 * nh))).astype(BF16)
    avec = jnp.zeros((1, LANES), F32).at[0, nh:2 * nh].set(a_log.astype(F32))
    dvec = jnp.zeros((1, LANES), F32).at[0, nh:2 * nh].set(dt_bias.astype(F32))
    row = lambda v: v.reshape(1, -1)
    const2 = lambda i, j: (0, 0)

    ts = cfg["dn_in_ts"]
    assert s % ts == 0 and ts >= CONV_HALO
    head_major = jax.ShapeDtypeStruct((b, nh, s, dh), F32)
    hm_spec = lambda t: pl.BlockSpec((1, nh, t, dh), lambda i, j: (i, 0, j, 0))
    q, k, v, z, bg = pl.pallas_call(
        functools.partial(_dn_in_kernel, ts=ts, sr=min(cfg["dn_in_sub"], ts)),
        grid=(b, s // ts),
        in_specs=[pl.BlockSpec((1, ts, d), lambda i, j: (i, j, 0)),
                  pl.BlockSpec((1, N_MOD, d), lambda i, j: (i, 0, 0)),
                  pl.BlockSpec((d, 4 * w), const2),
                  pl.BlockSpec((d, LANES), const2),
                  pl.BlockSpec((DN_CONV, 3 * w), const2),
                  pl.BlockSpec((1, LANES), const2),
                  pl.BlockSpec((1, LANES), const2)],
        out_specs=[hm_spec(ts), hm_spec(ts), hm_spec(ts),
                   pl.BlockSpec((1, ts, w), lambda i, j: (i, j, 0)),
                   pl.BlockSpec((1, ts, LANES), lambda i, j: (i, j, 0))],
        out_shape=[head_major, head_major, head_major,
                   jax.ShapeDtypeStruct((b, s, w), F32),
                   jax.ShapeDtypeStruct((b, s, LANES), F32)],
        scratch_shapes=[pltpu.VMEM((CONV_HALO + ts, 3 * w), F32)],
        compiler_params=_cparams(2),
        name="deltanet_in",
    )(x, m, w_main, w_small, conv_w, avec, dvec)

    ts = cfg["dn_core_ts"]
    assert s % ts == 0 and ts % DN_CHUNK == 0 and dh == DN_CHUNK
    per_head = pltpu.VMEM((nh, ts, dh), F32)
    o = pl.pallas_call(
        functools.partial(_dn_core_kernel, ts=ts),
        grid=(b, s // ts),
        in_specs=[hm_spec(ts), hm_spec(ts), hm_spec(ts),
                  pl.BlockSpec((1, ts, LANES), lambda i, j: (i, j, 0))],
        out_specs=hm_spec(ts),
        out_shape=head_major,
        scratch_shapes=[pltpu.VMEM((nh, dh, dh), F32),
                        per_head, per_head,
                        pltpu.VMEM((nh * (ts // DN_CHUNK), SUBLANES, DN_CHUNK), F32),
                        pltpu.VMEM((nh, ts // DN_CHUNK * SUBLANES, LANES), F32),
                        per_head,
                        pltpu.VMEM((nh, 2 * ts, dh), BF16),
                        pltpu.VMEM((nh, ts, DN_CHUNK), BF16),
                        pltpu.VMEM((nh, ts, dh), BF16)],
        compiler_params=_cparams(2),
        name="deltanet_core",
    )(q, k, v, bg)

    ts = cfg["dn_out_ts"]
    assert s % ts == 0
    return pl.pallas_call(
        functools.partial(_dn_out_kernel, alpha=alpha),
        grid=(b, s // ts),
        in_specs=[hm_spec(ts),
                  pl.BlockSpec((1, ts, w), lambda i, j: (i, j, 0)),
                  pl.BlockSpec((1, ts, d), lambda i, j: (i, j, 0)),
                  pl.BlockSpec((1, N_MOD, d), lambda i, j: (i, 0, 0)),
                  pl.BlockSpec((1, dh), const2),
                  pl.BlockSpec((w, d), const2),
                  pl.BlockSpec((1, d), const2),
                  pl.BlockSpec((1, d), const2)],
        out_specs=pl.BlockSpec((1, ts, d), lambda i, j: (i, j, 0)),
        out_shape=jax.ShapeDtypeStruct((b, s, d), F32),
        compiler_params=_cparams(2),
        name="deltanet_out",
    )(o, z, x, m, row(norm_w), w_out.astype(BF16), row(ln_g), row(ln_b))


def _pack_row_chunks(v):
    half = v.shape[-1] // 2
    bits = pltpu.bitcast(v.astype(BF16).astype(F32), U32)
    packed = (bits[:, :half] >> 16) | (bits[:, half:] & jnp.uint32(0xFFFF0000))
    return [packed[:, i * LANES:(i + 1) * LANES] for i in range(half // LANES)]


def _unpack_row_chunks(chunks):
    lo = [pltpu.bitcast(c << 16, F32) for c in chunks]
    hi = [pltpu.bitcast(c & jnp.uint32(0xFFFF0000), F32) for c in chunks]
    return jnp.concatenate(lo + hi, axis=1)


def _store_row_chunks(ref, chunks):
    for i, ch in enumerate(chunks):
        ref[i] = ch


def _load_row_chunks(ref, lead=()):
    return [ref[(*lead, i)] for i in range(ref.shape[len(lead)])]


def _sc_mesh():
    return plsc.VectorSubcoreMesh(core_axis_name="c", subcore_axis_name="s")


def _sc_gather_rows(table, idx):
    n = idx.shape[1]
    assert table.shape[1] == LANES and idx.shape[0] == 1 and n % (SC_WINDOW * SC_WORKERS) == 0

    @functools.partial(pl.kernel, out_type=jax.ShapeDtypeStruct((n, LANES), table.dtype), mesh=_sc_mesh(),
                       name="sc_gather_rows")
    def gather(table_hbm, idx_hbm, out_hbm):
        def body(idx_vmem, out_vmem):
            pltpu.sync_copy(table_hbm.at[idx_vmem.at[0]], out_vmem)

        pltpu.emit_pipeline(
            body,
            grid=(n // SC_WINDOW,),
            in_specs=[pl.BlockSpec((1, SC_WINDOW), lambda i: (0, i))],
            out_specs=[pl.BlockSpec((SC_WINDOW, LANES), lambda i: (i, 0))],
            core_axis_name=("c", "s"),
            dimension_semantics=(pltpu.PARALLEL,),
        )(idx_hbm, out_hbm)

    return gather(table, idx)


def _sc_scatter_rows(x, idx_list, n_out):
    n = x.shape[0]
    assert x.shape[1] == LANES and n % (SC_WINDOW * SC_WORKERS) == 0
    assert all(idx.shape == (1, n) for idx in idx_list)

    @functools.partial(pl.kernel, out_type=jax.ShapeDtypeStruct((n_out, LANES), x.dtype), mesh=_sc_mesh(),
                       name="sc_scatter_rows")
    def scatter(x_hbm, *refs):
        idx_hbms, out_hbm = refs[:-1], refs[-1]

        def body(x_vmem, *idx_vmems):
            for idx_vmem in idx_vmems:
                pltpu.sync_copy(x_vmem, out_hbm.at[idx_vmem.at[0]])

        pltpu.emit_pipeline(
            body,
            grid=(n // SC_WINDOW,),
            in_specs=[pl.BlockSpec((SC_WINDOW, LANES), lambda i: (i, 0))]
                     + [pl.BlockSpec((1, SC_WINDOW), lambda i: (0, i))] * len(idx_list),
            out_specs=[],
            core_axis_name=("c", "s"),
            dimension_semantics=(pltpu.PARALLEL,),
        )(x_hbm, *idx_hbms)

    return scatter(x, *idx_list)


def _router_kernel(x_ref, m_ref, rwt_ref, rb_ref, tri_ref,
                   hp_ref, idx_ref, p_ref, rank_ref, cnt_ref, run):
    i = pl.program_id(0)

    @pl.when(i == 0)
    def _():
        run[...] = jnp.zeros(run.shape, F32)

    x = x_ref[...]
    d = x.shape[-1]
    sh, sc = m_ref[0, 3:4, :], m_ref[0, 4:5, :]
    h = x * (1.0 + sc) + sh
    _store_row_chunks(hp_ref, _pack_row_chunks(h))

    h_hi = h.astype(BF16)
    h_lo = (h - h_hi.astype(F32)).astype(BF16)
    rw = rwt_ref[...]
    w_hi = rw.astype(BF16)
    w_lo = (rw - w_hi.astype(F32)).astype(BF16)
    ne = rw.shape[0]
    hi_terms = _dot_nt(jnp.concatenate([w_hi, w_lo], axis=0), h_hi)
    logits = hi_terms[:ne] + hi_terms[ne:] + _dot_nt(w_hi, h_lo) + rb_ref[...]
    eio = lax.broadcasted_iota(I32, logits.shape, 0).astype(F32)
    vals, idxs, sels = [], [], []
    for _ in range(TOP_K):
        mx = jnp.max(logits, axis=0, keepdims=True)
        ix = jnp.min(jnp.where(logits == mx, eio, float(ne)), axis=0, keepdims=True)
        sel = eio == ix
        logits = jnp.where(sel, -jnp.inf, logits)
        vals.append(mx)
        idxs.append(ix)
        sels.append(sel)
    exps = [jnp.exp(v - vals[0]) for v in vals]
    den = functools.reduce(lambda a, b_: a + b_, exps)
    chosen = functools.reduce(jnp.logical_or, sels)
    onehot = jnp.where(chosen, 1.0, 0.0)
    before = _dot(onehot.astype(BF16), tri_ref[...]) + run[...]
    ranks = [jnp.sum(jnp.where(sel, before, 0.0), axis=0, keepdims=True) for sel in sels]
    run[...] = run[...] + jnp.sum(onehot, axis=1, keepdims=True)
    idx_ref[...] = jnp.concatenate(idxs, axis=0).astype(I32)
    p_ref[...] = jnp.concatenate([e / den for e in exps]
                                 + [jnp.zeros((p_ref.shape[0] - TOP_K, den.shape[1]), F32)], axis=0)
    rank_ref[...] = jnp.concatenate(ranks, axis=0).astype(I32)
    cnt_ref[...] = jnp.broadcast_to(run[...], cnt_ref.shape)


def _experts_kernel(be_ref, nu_ref, nx_ref, par_ref, nv_ref, xs_ref, wgu_hbm, bgu_ref, wd_hbm, bd_ref, y_ref,
                    wgu_f, wd_f, sems, wgu_b, wd_b, *, layer):
    j = pl.program_id(0)

    def weight_copies(expert, slot):
        return (pltpu.make_async_copy(wgu_hbm.at[layer, expert], wgu_f.at[slot], sems.at[0, slot]),
                pltpu.make_async_copy(wd_hbm.at[layer, expert], wd_f.at[slot], sems.at[1, slot]))

    @pl.when(j >= nu_ref[0])
    def _():
        y_ref[...] = jnp.zeros(y_ref.shape, U32)

    @pl.when(j < nu_ref[0])
    def _():
        expert, slot = be_ref[j], par_ref[j]
        new_expert = jnp.logical_or(j == 0, expert != be_ref[jnp.maximum(j - 1, 0)])

        @pl.when(j == 0)
        def _():
            for cp in weight_copies(expert, slot):
                cp.start()

        @pl.when(new_expert)
        def _():
            for cp in weight_copies(expert, slot):
                cp.wait()

            @pl.when(nx_ref[j] >= 0)
            def _():
                for cp in weight_copies(nx_ref[j], 1 - slot):
                    cp.start()

            wgu_b[...] = wgu_f[slot].astype(BF16)
            wd_b[...] = wd_f[slot].astype(BF16)

        is_token = lax.broadcasted_iota(I32, xs_ref.shape[1:], 0) < nv_ref[j]
        chunks = [jnp.where(is_token, ch, jnp.uint32(0)) for ch in _load_row_chunks(xs_ref)]
        xb = _unpack_row_chunks(chunks).astype(BF16)
        gu = _dot(xb, wgu_b[...]) + bgu_ref[0, 0]
        f = gu.shape[1] // 2
        glu = jnp.minimum(gu[:, :f], SWIGLU_LIMIT)
        lin = jnp.clip(gu[:, f:], -SWIGLU_LIMIT, SWIGLU_LIMIT)
        act = glu * jax.nn.sigmoid(SWIGLU_ALPHA * glu) * (lin + 1.0)
        y = _dot(act.astype(BF16), wd_b[...]) + bd_ref[0, 0]
        _store_row_chunks(y_ref, _pack_row_chunks(y))


def _combine_kernel(ya_ref, x_ref, m_ref, p_ref, lg_ref, lb_ref, *rest, alpha):
    o_ref = rest[-1]
    p_rows = p_ref[...]
    eye = (lax.broadcasted_iota(I32, (p_rows.shape[0], LANES), 0)
           == lax.broadcasted_iota(I32, (p_rows.shape[0], LANES), 1)).astype(F32)
    p = lax.dot_general(p_rows, eye, (((0,), (0,)), ((), ())), preferred_element_type=F32,
                        precision=HIGHEST)
    y = None
    for kk in range(TOP_K):
        yk = p[:, kk:kk + 1] * _unpack_row_chunks(_load_row_chunks(ya_ref, (kk,)))
        y = yk if y is None else y + yk
    gt = m_ref[0, 5:6, :]
    o_ref[...] = _layer_norm(alpha * x_ref[...] + (1.0 + gt) * y, lg_ref[...], lb_ref[...])


def _moe_layer(x, m, layer, router_w, router_b, w_gu, b_gu, w_down, b_down, ln_g, ln_b, alpha, cfg):
    b, s, d = x.shape
    n_tok = b * s
    ne = router_w.shape[-1]
    f2 = w_gu.shape[-1]
    ff = f2 // 2
    assert d % (2 * LANES) == 0 and w_down.shape[-2:] == (ff, d)
    rc = d // (2 * LANES)
    xf = x.reshape(n_tok, d)
    const2 = lambda i: (0, 0)

    tt = cfg["router_tt"]
    assert s % tt == 0
    tri = jnp.triu(jnp.ones((tt, tt), BF16), k=1)
    hp, idx, probs, rank, cnt = pl.pallas_call(
        _router_kernel,
        grid=(n_tok // tt,),
        in_specs=[pl.BlockSpec((tt, d), lambda i: (i, 0)),
                  pl.BlockSpec((1, N_MOD, d), lambda i: (i * tt // s, 0, 0)),
                  pl.BlockSpec((ne, d), const2),
                  pl.BlockSpec((ne, 1), const2),
                  pl.BlockSpec((tt, tt), const2)],
        out_specs=[pl.BlockSpec((rc, tt, LANES), lambda i: (0, i, 0)),
                   pl.BlockSpec((TOP_K, tt), lambda i: (0, i)),
                   pl.BlockSpec((SUBLANES, tt), lambda i: (0, i)),
                   pl.BlockSpec((TOP_K, tt), lambda i: (0, i)),
                   pl.BlockSpec((ne, LANES), const2)],
        out_shape=[jax.ShapeDtypeStruct((rc, n_tok, LANES), U32),
                   jax.ShapeDtypeStruct((TOP_K, n_tok), I32),
                   jax.ShapeDtypeStruct((SUBLANES, n_tok), F32),
                   jax.ShapeDtypeStruct((TOP_K, n_tok), I32),
                   jax.ShapeDtypeStruct((ne, LANES), F32)],
        scratch_shapes=[pltpu.VMEM((ne, 1), F32)],
        compiler_params=_cparams(1),
        name="moe_router",
    )(xf, m, router_w[layer].T, router_b[layer].reshape(ne, 1), tri)

    bm = cfg["expert_bm"]
    cap = n_tok * TOP_K + ne * bm
    nb = cap // bm
    counts = cnt[:, 0].astype(I32)
    padded = (counts + bm - 1) // bm * bm
    pad_end = jnp.cumsum(padded)
    pad_start = pad_end - padded
    eids = jnp.arange(ne, dtype=I32)[:, None, None]
    dest = rank + jnp.sum(jnp.where(idx[None] == eids, pad_start[:, None, None], 0), axis=0)
    block_start = jnp.arange(nb, dtype=I32) * bm
    block_expert = jnp.minimum(jnp.sum(block_start[None, :] >= pad_end[:, None], axis=0), ne - 1).astype(I32)
    n_used = (pad_end[-1:] // bm).astype(I32)
    first_of_expert = jnp.concatenate([jnp.ones((1,), bool), block_expert[1:] != block_expert[:-1]])
    weight_slot = ((jnp.cumsum(first_of_expert.astype(I32)) - 1) % 2).astype(I32)
    next_block = pad_end[block_expert] // bm
    next_expert = jnp.where(next_block < n_used[0], block_expert[jnp.minimum(next_block, nb - 1)], -1).astype(I32)

    plane_offset = (jnp.arange(rc * n_tok, dtype=I32) // n_tok * cap)[None, :]
    slot_rows = [jnp.tile(dest[kk:kk + 1], (1, rc)) + plane_offset for kk in range(TOP_K)]

    xs = _sc_scatter_rows(hp.reshape(rc * n_tok, LANES), slot_rows, rc * cap).reshape(rc, cap, LANES)

    last = lambda j, be, nu, *_: jnp.minimum(j, nu[0] - 1)
    bias_map = lambda j, be, nu, *_: (layer, be[last(j, be, nu)], 0, 0)
    block_tokens = jnp.clip(pad_start[block_expert] + counts[block_expert] - block_start, 0, bm).astype(I32)
    y_rows = pl.pallas_call(
        functools.partial(_experts_kernel, layer=layer),
        grid_spec=pltpu.PrefetchScalarGridSpec(
            num_scalar_prefetch=5,
            grid=(nb,),
            in_specs=[pl.BlockSpec((rc, bm, LANES), lambda j, be, nu, *_: (0, last(j, be, nu), 0)),
                      pl.BlockSpec(memory_space=pl.ANY),
                      pl.BlockSpec((1, 1, 1, f2), bias_map),
                      pl.BlockSpec(memory_space=pl.ANY),
                      pl.BlockSpec((1, 1, 1, d), bias_map)],
            out_specs=pl.BlockSpec((rc, bm, LANES), lambda j, *_: (0, j, 0)),
            scratch_shapes=[pltpu.VMEM((2, d, f2), F32), pltpu.VMEM((2, ff, d), F32),
                            pltpu.SemaphoreType.DMA((2, 2)),
                            pltpu.VMEM((d, f2), BF16), pltpu.VMEM((ff, d), BF16)]),
        out_shape=jax.ShapeDtypeStruct((rc, cap, LANES), U32),
        compiler_params=_cparams(1),
        name="moe_experts",
    )(block_expert, n_used, next_expert, weight_slot, block_tokens, xs, w_gu, b_gu.reshape(*b_gu.shape[:2], 1, f2),
      w_down, b_down.reshape(*b_down.shape[:2], 1, d))

    tt = cfg["combine_tt"]
    n_groups = cfg["combine_groups"] if n_tok % (cfg["combine_groups"] * max(tt, SC_WINDOW * SC_WORKERS)) == 0 else 1
    n_grp = n_tok // n_groups
    tiles = n_grp // tt
    assert s % tt == 0 and n_grp % tt == 0
    y_table = y_rows.reshape(rc * cap, LANES)
    grp_rows = (dest.reshape(TOP_K, n_groups, n_grp).transpose(1, 0, 2)[:, :, None, :]
                + (jnp.arange(rc, dtype=I32) * cap)[None, None, :, None])
    out = None
    for gi in range(n_groups):
        rows_g = grp_rows[gi].reshape(1, TOP_K * rc * n_grp)
        y_assign = _sc_gather_rows(y_table, rows_g).reshape(TOP_K, rc, n_grp, LANES)
        first = gi * tiles
        operands = [y_assign, xf, m, probs, ln_g.reshape(1, d), ln_b.reshape(1, d)]
        in_specs = [pl.BlockSpec((TOP_K, rc, tt, LANES), lambda i: (0, 0, i, 0)),
                    pl.BlockSpec((tt, d), lambda i, first=first: (first + i, 0)),
                    pl.BlockSpec((1, N_MOD, d), lambda i, first=first: ((first + i) * tt // s, 0, 0)),
                    pl.BlockSpec((SUBLANES, tt), lambda i, first=first: (0, first + i)),
                    pl.BlockSpec((1, d), const2),
                    pl.BlockSpec((1, d), const2)]
        aliases = {}
        if out is not None:
            operands.append(out)
            in_specs.append(pl.BlockSpec(memory_space=pl.ANY))
            aliases = {len(operands) - 1: 0}
        out = pl.pallas_call(
            functools.partial(_combine_kernel, alpha=alpha),
            grid=(tiles,),
            in_specs=in_specs,
            out_specs=pl.BlockSpec((tt, d), lambda i, first=first: (first + i, 0)),
            out_shape=jax.ShapeDtypeStruct((n_tok, d), F32),
            input_output_aliases=aliases,
            compiler_params=_cparams(1),
            name="moe_combine",
        )(*operands)
    return out.reshape(b, s, d)


def kernel(x, c, ada_w, ada_b, ln_g, ln_b, pool_w, pool_scale, dn_w_in, dn_conv_w, dn_a_log, dn_dt_bias,
           dn_norm_w, dn_w_out, router_w, router_b, exp_w_gu, exp_b_gu, exp_w_down, exp_b_down):
    b, s, d = x.shape
    depth = ada_w.shape[0]
    alpha = (2 * depth) ** 0.25
    cfg = _tile_config(s, b * s)
    mods = _mods(c, ada_w, ada_b)
    n_mixers = 2
    for i in range(depth):
        m = mods[i]
        j = i // n_mixers
        if i % n_mixers == 0:
            x = _pool_layer(x, m, pool_w[j], pool_scale[j], ln_g[i, 0], ln_b[i, 0], alpha, cfg)
        else:
            x = _deltanet_layer(x, m, dn_w_in[j], dn_conv_w[j], dn_a_log[j], dn_dt_bias[j], dn_norm_w[j],
                                dn_w_out[j], ln_g[i, 0], ln_b[i, 0], alpha, cfg)
        x = _moe_layer(x, m, i, router_w, router_b, exp_w_gu, exp_b_gu, exp_w_down, exp_b_down,
                       ln_g[i, 1], ln_b[i, 1], alpha, cfg)
    return x
```

```python
import functools

import jax
import jax.numpy as jnp
from jax import lax
from jax.experimental import pallas as pl
from jax.experimental.pallas import tpu as pltpu
from jax.experimental.pallas import tpu_sc as plsc

F32 = jnp.float32
BF16 = jnp.bfloat16
I32 = jnp.int32
U32 = jnp.uint32

N_MOD = 6
POOL_WINDOWS = (2, 4, 8, 16)
DN_HEADS = 8
DN_HEAD_DIM = 128
DN_CONV = 4
TOP_K = 4
SWIGLU_LIMIT = 7.0
SWIGLU_ALPHA = 1.702
LN_EPS = 1e-5
RMS_EPS = 1e-6

LANES = 128
SUBLANES = 8
VMEM_LIMIT_BYTES = 56 * 1024 * 1024
SC_WORKERS = 32
SC_WINDOW = 128

DN_CHUNK = LANES
DN_HEAD_GROUP = 8
POOL_HALO = 32
CONV_HALO = SUBLANES

HIGHEST = lax.Precision.HIGHEST


def _tile_config(seq, n_tok):
    return dict(
        pool_ts=min(512, seq),
        dn_in_ts=min(512, seq),
        dn_in_sub=128,
        dn_core_ts=min(512, seq),
        dn_out_ts=min(512, seq),
        router_tt=min(512, seq),
        expert_bm=512,
        combine_tt=min(256, seq),
        combine_groups=4,
    )


def _cparams(n_axes):
    return pltpu.CompilerParams(dimension_semantics=("arbitrary",) * n_axes,
                                vmem_limit_bytes=VMEM_LIMIT_BYTES)


def _layer_norm(v, g, b):
    mu = jnp.mean(v, axis=-1, keepdims=True)
    d = v - mu
    var = jnp.mean(d * d, axis=-1, keepdims=True)
    return d * lax.rsqrt(var + LN_EPS) * g + b


def _dot(a, b):
    return jnp.dot(a, b, preferred_element_type=F32)


def _dot_nt(a, b):
    return lax.dot_general(a, b, (((1,), (1,)), ((), ())), preferred_element_type=F32)


def _dot_tn(a, b):
    return lax.dot_general(a, b, (((0,), (0,)), ((), ())), preferred_element_type=F32)


def _bdot(a, b):
    return lax.dot_general(a, b, (((2,), (1,)), ((0,), (0,))), preferred_element_type=F32)


def _bdot_nt(a, b):
    return lax.dot_general(a, b, (((2,), (2,)), ((0,), (0,))), preferred_element_type=F32)


def _mods_kernel(c_ref, w_ref, b_ref, o_ref):
    c = c_ref[...]
    c_act = c * jax.nn.sigmoid(c)
    o_ref[0] = jnp.dot(c_act, w_ref[0], preferred_element_type=F32, precision=HIGHEST) + b_ref[0]


def _mods(c, ada_w, ada_b):
    depth, d, n = ada_w.shape
    b = c.shape[0]
    tn = 2048 if n % 2048 == 0 else n
    out = pl.pallas_call(
        _mods_kernel,
        grid=(depth, n // tn),
        in_specs=[pl.BlockSpec((b, d), lambda i, j: (0, 0)),
                  pl.BlockSpec((1, d, tn), lambda i, j: (i, 0, j)),
                  pl.BlockSpec((1, 1, tn), lambda i, j: (i, 0, j))],
        out_specs=pl.BlockSpec((1, b, tn), lambda i, j: (i, 0, j)),
        out_shape=jax.ShapeDtypeStruct((depth, b, n), F32),
        compiler_params=_cparams(2),
        name="adaln_mods",
    )(c, ada_w, ada_b.reshape(depth, 1, n))
    return out.reshape(depth, b, N_MOD, d)


def _pool_kernel(x_ref, m_ref, pw_ref, ps_ref, lg_ref, lb_ref, o_ref, e1, ea, eb, *, ts, alpha):
    s = pl.program_id(1)
    d = x_ref.shape[-1]
    gw = d // len(POOL_WINDOWS)
    halo = POOL_HALO
    rows = halo + ts
    x = x_ref[0]
    sh, sc, gt = m_ref[0, 0:1, :], m_ref[0, 1:2, :], m_ref[0, 2:3, :]
    h = x * (1.0 + sc) + sh

    @pl.when(s == 0)
    def _():
        e1[0:halo, :] = jnp.zeros((halo, d), F32)

    e1[halo:rows, :] = h
    ea[8:rows, :] = e1[8:rows, :] + e1[7:rows - 1, :]
    eb[16:rows, gw:] = ea[16:rows, gw:] + ea[14:rows - 2, gw:]
    ea[24:rows, 2 * gw:] = eb[24:rows, 2 * gw:] + eb[20:rows - 4, 2 * gw:]
    eb[32:rows, 3 * gw:] = ea[32:rows, 3 * gw:] + ea[24:rows - 8, 3 * gw:]

    pos = s * ts + lax.broadcasted_iota(I32, (ts, 1), 0)
    outs = []
    for g, win in enumerate(POOL_WINDOWS):
        src = ea if g % 2 == 0 else eb
        cols = slice(g * gw, (g + 1) * gw)
        cnt = jnp.minimum(pos + 1, win).astype(F32)
        pooled = src[halo:rows, cols] / cnt - h[:, cols]
        outs.append(_dot(pooled.astype(BF16), pw_ref[g]))
    y = jnp.concatenate(outs, axis=1) * ps_ref[...]
    o_ref[0] = _layer_norm(alpha * x + (1.0 + gt) * y, lg_ref[...], lb_ref[...])
    e1[0:halo, :] = e1[ts:rows, :]


def _pool_layer(x, m, pool_w, pool_scale, ln_g, ln_b, alpha, cfg):
    b, s, d = x.shape
    ts = cfg["pool_ts"]
    g, gw, _ = pool_w.shape
    assert POOL_WINDOWS == (2, 4, 8, 16) and g == len(POOL_WINDOWS) and s % ts == 0 and ts >= POOL_HALO
    row = lambda v: v.reshape(1, d)
    return pl.pallas_call(
        functools.partial(_pool_kernel, ts=ts, alpha=alpha),
        grid=(b, s // ts),
        in_specs=[pl.BlockSpec((1, ts, d), lambda i, j: (i, j, 0)),
                  pl.BlockSpec((1, N_MOD, d), lambda i, j: (i, 0, 0)),
                  pl.BlockSpec((g, gw, gw), lambda i, j: (0, 0, 0)),
                  pl.BlockSpec((1, d), lambda i, j: (0, 0)),
                  pl.BlockSpec((1, d), lambda i, j: (0, 0)),
                  pl.BlockSpec((1, d), lambda i, j: (0, 0))],
        out_specs=pl.BlockSpec((1, ts, d), lambda i, j: (i, j, 0)),
        out_shape=jax.ShapeDtypeStruct((b, s, d), F32),
        scratch_shapes=[pltpu.VMEM((POOL_HALO + ts, d), F32)] * 3,
        compiler_params=_cparams(2),
        name="pool_layer",
    )(x, m, pool_w.astype(BF16), row(pool_scale), row(ln_g), row(ln_b))


def _dn_in_kernel(x_ref, m_ref, wm_ref, ws_ref, cw_ref, av_ref, dv_ref,
                  q_ref, k_ref, v_ref, z_ref, bg_ref, ext, *, ts, sr):
    s = pl.program_id(1)
    nh, dh = DN_HEADS, DN_HEAD_DIM
    w = nh * dh
    halo = CONV_HALO
    sh, sc = m_ref[0, 0:1, :], m_ref[0, 1:2, :]

    @pl.when(s == 0)
    def _():
        ext[0:halo, :] = jnp.zeros((halo, 3 * w), F32)

    def project(i):
        rows = slice(i * sr, (i + 1) * sr)
        h = (x_ref[0, rows, :] * (1.0 + sc) + sh).astype(BF16)
        proj = _dot(h, wm_ref[...])
        ext[halo + i * sr:halo + (i + 1) * sr, :] = proj[:, :3 * w]
        z_ref[0, rows, :] = proj[:, 3 * w:]
        small = _dot(h, ws_ref[...])
        lane = lax.broadcasted_iota(I32, small.shape, 1)
        beta = jax.nn.sigmoid(small)
        g = -jnp.exp(av_ref[...]) * jax.nn.softplus(small + dv_ref[...])
        bg_ref[0, rows, :] = jnp.where(lane < nh, beta, jnp.where(lane < 2 * nh, g, 0.0))

    def mix(i):
        rows = slice(i * sr, (i + 1) * sr)
        base = halo - (DN_CONV - 1) + i * sr
        conv = cw_ref[0:1, :] * ext[base:base + sr, :]
        for j in range(1, DN_CONV):
            conv = conv + cw_ref[j:j + 1, :] * ext[base + j:base + j + sr, :]
        act = conv * jax.nn.sigmoid(conv)
        for hd in range(nh):
            qh = act[:, hd * dh:(hd + 1) * dh]
            kh = act[:, w + hd * dh:w + (hd + 1) * dh]
            q_ref[0, hd, rows, :] = (qh * lax.rsqrt(jnp.sum(qh * qh, axis=-1, keepdims=True) + RMS_EPS)
                                     * (dh ** -0.5))
            k_ref[0, hd, rows, :] = kh * lax.rsqrt(jnp.sum(kh * kh, axis=-1, keepdims=True) + RMS_EPS)
            v_ref[0, hd, rows, :] = act[:, 2 * w + hd * dh:2 * w + (hd + 1) * dh]

    n_sub = ts // sr
    project(0)
    for i in range(1, n_sub):
        project(i)
        mix(i - 1)
    mix(n_sub - 1)
    ext[0:halo, :] = ext[ts:ts + halo, :]


def _dn_core_kernel(q_ref, k_ref, v_ref, bg_ref, o_ref,
                    st, bb, gcb, gt, gl_s, u_s, wq_s, ai_s, kd_s, *, ts):
    s = pl.program_id(1)
    nh, dh, c = DN_HEADS, DN_HEAD_DIM, DN_CHUNK
    nc = ts // c

    @pl.when(s == 0)
    def _():
        st[...] = jnp.zeros(st.shape, F32)

    bgv = bg_ref[0]
    lane = lax.broadcasted_iota(I32, bgv.shape, 1)
    rowc = lax.broadcasted_iota(I32, bgv.shape, 0) % c
    gc = jnp.where(lane >= nh, bgv, 0.0)
    shift = 1
    while shift < c:
        gc = gc + jnp.where(rowc >= shift, pltpu.roll(gc, shift, 0), 0.0)
        shift *= 2
    gc_t = gc.T
    for hd in range(nh):
        for ci in range(nc):
            gt[hd * nc + ci] = jnp.broadcast_to(gc_t[nh + hd:nh + hd + 1, ci * c:(ci + 1) * c], (SUBLANES, c))
        bb[hd] = jnp.broadcast_to(bgv[:, hd:hd + 1], (ts, LANES))
        gcb[hd] = jnp.broadcast_to(gc[:, nh + hd:nh + hd + 1], (ts, LANES))

    ri = lax.broadcasted_iota(I32, (c, c), 0)
    ci_ = lax.broadcasted_iota(I32, (c, c), 1)
    tril = ri >= ci_
    eye = (ri == ci_).astype(F32)
    off_masks = []
    blk = 1
    while blk < c:
        off_masks.append((ri // blk != ci_ // blk) & (ri // (2 * blk) == ci_ // (2 * blk)) & (ri > ci_))
        blk *= 2

    hg = DN_HEAD_GROUP
    nb = hg * nc
    n_pairs = nb // 2
    two = lambda msk: jnp.concatenate([msk, msk], axis=1)
    tril2, eye2, off2 = two(tril), two(eye), [two(msk) for msk in off_masks]

    def pair(a):
        a = a.reshape(n_pairs, 2, a.shape[1], a.shape[2])
        return jnp.concatenate([a[:, 0], a[:, 1]], axis=-1)

    def unpair(a):
        w = a.shape[-1] // 2
        return jnp.stack([a[..., :w], a[..., w:]], axis=1).reshape(nb, a.shape[1], w)

    def block_diag(a):
        left = lax.broadcasted_iota(I32, a.shape[1:], 1) < a.shape[-1] // 2
        zero = jnp.zeros_like(a)
        return jnp.concatenate([jnp.where(left, a, zero), jnp.where(left, zero, a)], axis=1)

    def group_body(gi, carry):
        heads = pl.ds(gi * hg, hg)
        chunks = lambda a: a.reshape(nb, c, a.shape[-1])
        per_head = lambda a: a.reshape(hg, ts, a.shape[-1])
        q, k, v = chunks(q_ref[0, heads]), chunks(k_ref[0, heads]), chunks(v_ref[0, heads])
        beta, gch = chunks(bb[heads]), chunks(gcb[heads])
        egh = jnp.exp(gch)
        g_last = jnp.broadcast_to(gch[:, c - 1:c, :], gch.shape)
        kb = k * beta
        vb = v * beta
        kd_s[heads] = per_head((k * jnp.exp(g_last - gch)).astype(BF16))
        gl_s[heads] = jnp.broadcast_to(egh[:, c - 1:c, :], (nb, SUBLANES, LANES)).reshape(hg, nc * SUBLANES, LANES)
        grow = gt[pl.ds(gi * nb, nb), 0:1, :]
        dec = jnp.where(tril2, jnp.exp(jnp.where(tril2, pair(gch) - pair(grow), 0.0)), 0.0)
        kq = jnp.concatenate([pair(kb), pair(q)], axis=1).astype(BF16)
        a_all = _bdot_nt(kq, block_diag(pair(k).astype(BF16)))
        ai_s[heads] = per_head(unpair((a_all[:, c:] * dec).astype(BF16)))
        l_mat = a_all[:, :c] * dec
        t_inv = eye2 - jnp.where(off2[0], l_mat, 0.0)
        for off in off2[1:]:
            tb = t_inv.astype(BF16)
            l_off = jnp.where(off, l_mat, 0.0).astype(BF16)
            t_inv = t_inv - _bdot(tb, block_diag(_bdot(l_off, block_diag(tb)).astype(BF16)))
        rhs = jnp.concatenate([vb, kb * egh], axis=2).astype(BF16)
        uw = unpair(_bdot(t_inv.astype(BF16), block_diag(pair(rhs))))
        u_s[heads] = per_head(uw[:, :, :dh])
        wq = jnp.concatenate([uw[:, :, dh:], q * egh], axis=1).astype(BF16)
        wq_s[heads] = wq.reshape(hg, 2 * ts, dh)
        return carry

    lax.fori_loop(0, nh // hg, group_body, 0)

    def chunk_body(ci, carry):
        rows = pl.ds(pl.multiple_of(ci * c, c), c)
        rows2 = pl.ds(pl.multiple_of(ci * 2 * c, 2 * c), 2 * c)
        heads = range(nh)
        states = [st[hd] for hd in heads]
        ws_qs = [_dot(wq_s[hd, rows2, :], states[hd].astype(BF16)) for hd in heads]
        v_nb = [(u_s[hd, rows, :] - ws_qs[hd][:c]).astype(BF16) for hd in heads]
        outs = [ws_qs[hd][c:] + _dot(ai_s[hd, rows, :], v_nb[hd]) for hd in heads]
        new_states = [states[hd] * gl_s[hd, pl.ds(pl.multiple_of(ci * SUBLANES, SUBLANES), 1), :]
                      + _dot_tn(kd_s[hd, rows, :], v_nb[hd]) for hd in heads]
        for hd in heads:
            o_ref[0, hd, rows, :] = outs[hd]
            st[hd] = new_states[hd]
        return carry

    lax.fori_loop(0, nc, chunk_body, 0)


def _dn_out_kernel(o_ref, z_ref, x_ref, m_ref, nw_ref, wo_ref, lg_ref, lb_ref, out_ref, *, alpha):
    nh, dh = DN_HEADS, DN_HEAD_DIM
    z = z_ref[0]
    parts = []
    for hd in range(nh):
        oh = o_ref[0, hd]
        on = oh * lax.rsqrt(jnp.mean(oh * oh, axis=-1, keepdims=True) + RMS_EPS) * nw_ref[...]
        zh = z[:, hd * dh:(hd + 1) * dh]
        parts.append((on * (zh * jax.nn.sigmoid(zh))).astype(BF16))
    y = _dot(jnp.concatenate(parts, axis=1), wo_ref[...])
    x = x_ref[0]
    gt = m_ref[0, 2:3, :]
    out_ref[0] = _layer_norm(alpha * x + (1.0 + gt) * y, lg_ref[...], lb_ref[...])


def _deltanet_layer(x, m, w_in, conv_w, a_log, dt_bias, norm_w, w_out, ln_g, ln_b, alpha, cfg):
    b, s, d = x.shape
    nh, dh = DN_HEADS, DN_HEAD_DIM
    w = nh * dh
    assert w_in.shape == (d, 4 * w + 2 * nh) and conv_w.shape == (DN_CONV, 3 * w) and 2 * nh <= LANES
    w_main = w_in[:, :4 * w].astype(BF16)
    w_small = jnp.pad(w_in[:, 4 * w:], ((0, 0), (0, LANES - 2 * nh))).astype(BF16)
    avec = jnp.zeros((1, LANES), F32).at[0, nh:2 * nh].set(a_log.astype(F32))
    dvec = jnp.zeros((1, LANES), F32).at[0, nh:2 * nh].set(dt_bias.astype(F32))
    row = lambda v: v.reshape(1, -1)
    const2 = lambda i, j: (0, 0)

    ts = cfg["dn_in_ts"]
    assert s % ts == 0 and ts >= CONV_HALO
    head_major = jax.ShapeDtypeStruct((b, nh, s, dh), F32)
    hm_spec = lambda t: pl.BlockSpec((1, nh, t, dh), lambda i, j: (i, 0, j, 0))
    q, k, v, z, bg = pl.pallas_call(
        functools.partial(_dn_in_kernel, ts=ts, sr=min(cfg["dn_in_sub"], ts)),
        grid=(b, s // ts),
        in_specs=[pl.BlockSpec((1, ts, d), lambda i, j: (i, j, 0)),
                  pl.BlockSpec((1, N_MOD, d), lambda i, j: (i, 0, 0)),
                  pl.BlockSpec((d, 4 * w), const2),
                  pl.BlockSpec((d, LANES), const2),
                  pl.BlockSpec((DN_CONV, 3 * w), const2),
                  pl.BlockSpec((1, LANES), const2),
                  pl.BlockSpec((1, LANES), const2)],
        out_specs=[hm_spec(ts), hm_spec(ts), hm_spec(ts),
                   pl.BlockSpec((1, ts, w), lambda i, j: (i, j, 0)),
                   pl.BlockSpec((1, ts, LANES), lambda i, j: (i, j, 0))],
        out_shape=[head_major, head_major, head_major,
                   jax.ShapeDtypeStruct((b, s, w), F32),
                   jax.ShapeDtypeStruct((b, s, LANES), F32)],
        scratch_shapes=[pltpu.VMEM((CONV_HALO + ts, 3 * w), F32)],
        compiler_params=_cparams(2),
        name="deltanet_in",
    )(x, m, w_main, w_small, conv_w, avec, dvec)

    ts = cfg["dn_core_ts"]
    assert s % ts == 0 and ts % DN_CHUNK == 0 and dh == DN_CHUNK
    per_head = pltpu.VMEM((nh, ts, dh), F32)
    o = pl.pallas_call(
        functools.partial(_dn_core_kernel, ts=ts),
        grid=(b, s // ts),
        in_specs=[hm_spec(ts), hm_spec(ts), hm_spec(ts),
                  pl.BlockSpec((1, ts, LANES), lambda i, j: (i, j, 0))],
        out_specs=hm_spec(ts),
        out_shape=head_major,
        scratch_shapes=[pltpu.VMEM((nh, dh, dh), F32),
                        per_head, per_head,
                        pltpu.VMEM((nh * (ts // DN_CHUNK), SUBLANES, DN_CHUNK), F32),
                        pltpu.VMEM((nh, ts // DN_CHUNK * SUBLANES, LANES), F32),
                        per_head,
                        pltpu.VMEM((nh, 2 * ts, dh), BF16),
                        pltpu.VMEM((nh, ts, DN_CHUNK), BF16),
                        pltpu.VMEM((nh, ts, dh), BF16)],
        compiler_params=_cparams(2),
        name="deltanet_core",
    )(q, k, v, bg)

    ts = cfg["dn_out_ts"]
    assert s % ts == 0
    return pl.pallas_call(
        functools.partial(_dn_out_kernel, alpha=alpha),
        grid=(b, s // ts),
        in_specs=[hm_spec(ts),
                  pl.BlockSpec((1, ts, w), lambda i, j: (i, j, 0)),
                  pl.BlockSpec((1, ts, d), lambda i, j: (i, j, 0)),
                  pl.BlockSpec((1, N_MOD, d), lambda i, j: (i, 0, 0)),
                  pl.BlockSpec((1, dh), const2),
                  pl.BlockSpec((w, d), const2),
                  pl.BlockSpec((1, d), const2),
                  pl.BlockSpec((1, d), const2)],
        out_specs=pl.BlockSpec((1, ts, d), lambda i, j: (i, j, 0)),
        out_shape=jax.ShapeDtypeStruct((b, s, d), F32),
        compiler_params=_cparams(2),
        name="deltanet_out",
    )(o, z, x, m, row(norm_w), w_out.astype(BF16), row(ln_g), row(ln_b))


def _pack_row_chunks(v):
    half = v.shape[-1] // 2
    bits = pltpu.bitcast(v.astype(BF16).astype(F32), U32)
    packed = (bits[:, :half] >> 16) | (bits[:, half:] & jnp.uint32(0xFFFF0000))
    return [packed[:, i * LANES:(i + 1) * LANES] for i in range(half // LANES)]


def _unpack_row_chunks(chunks):
    lo = [pltpu.bitcast(c << 16, F32) for c in chunks]
    hi = [pltpu.bitcast(c & jnp.uint32(0xFFFF0000), F32) for c in chunks]
    return jnp.concatenate(lo + hi, axis=1)


def _store_row_chunks(ref, chunks):
    for i, ch in enumerate(chunks):
        ref[i] = ch


def _load_row_chunks(ref, lead=()):
    return [ref[(*lead, i)] for i in range(ref.shape[len(lead)])]


def _sc_mesh():
    return plsc.VectorSubcoreMesh(core_axis_name="c", subcore_axis_name="s")


def _sc_gather_rows(table, idx):
    n = idx.shape[1]
    assert table.shape[1] == LANES and idx.shape[0] == 1 and n % (SC_WINDOW * SC_WORKERS) == 0

    @functools.partial(pl.kernel, out_type=jax.ShapeDtypeStruct((n, LANES), table.dtype), mesh=_sc_mesh(),
                       name="sc_gather_rows")
    def gather(table_hbm, idx_hbm, out_hbm):
        def body(idx_vmem, out_vmem):
            pltpu.sync_copy(table_hbm.at[idx_vmem.at[0]], out_vmem)

        pltpu.emit_pipeline(
            body,
            grid=(n // SC_WINDOW,),
            in_specs=[pl.BlockSpec((1, SC_WINDOW), lambda i: (0, i))],
            out_specs=[pl.BlockSpec((SC_WINDOW, LANES), lambda i: (i, 0))],
            core_axis_name=("c", "s"),
            dimension_semantics=(pltpu.PARALLEL,),
        )(idx_hbm, out_hbm)

    return gather(table, idx)


def _sc_scatter_rows(x, idx_list, n_out):
    n = x.shape[0]
    assert x.shape[1] == LANES and n % (SC_WINDOW * SC_WORKERS) == 0
    assert all(idx.shape == (1, n) for idx in idx_list)

    @functools.partial(pl.kernel, out_type=jax.ShapeDtypeStruct((n_out, LANES), x.dtype), mesh=_sc_mesh(),
                       name="sc_scatter_rows")
    def scatter(x_hbm, *refs):
        idx_hbms, out_hbm = refs[:-1], refs[-1]

        def body(x_vmem, *idx_vmems):
            for idx_vmem in idx_vmems:
                pltpu.sync_copy(x_vmem, out_hbm.at[idx_vmem.at[0]])

        pltpu.emit_pipeline(
            body,
            grid=(n // SC_WINDOW,),
            in_specs=[pl.BlockSpec((SC_WINDOW, LANES), lambda i: (i, 0))]
                     + [pl.BlockSpec((1, SC_WINDOW), lambda i: (0, i))] * len(idx_list),
            out_specs=[],
            core_axis_name=("c", "s"),
            dimension_semantics=(pltpu.PARALLEL,),
        )(x_hbm, *idx_hbms)

    return scatter(x, *idx_list)


def _router_kernel(x_ref, m_ref, rwt_ref, rb_ref, tri_ref,
                   hp_ref, idx_ref, p_ref, rank_ref, cnt_ref, run):
    i = pl.program_id(0)

    @pl.when(i == 0)
    def _():
        run[...] = jnp.zeros(run.shape, F32)

    x = x_ref[...]
    d = x.shape[-1]
    sh, sc = m_ref[0, 3:4, :], m_ref[0, 4:5, :]
    h = x * (1.0 + sc) + sh
    _store_row_chunks(hp_ref, _pack_row_chunks(h))

    h_hi = h.astype(BF16)
    h_lo = (h - h_hi.astype(F32)).astype(BF16)
    rw = rwt_ref[...]
    w_hi = rw.astype(BF16)
    w_lo = (rw - w_hi.astype(F32)).astype(BF16)
    ne = rw.shape[0]
    hi_terms = _dot_nt(jnp.concatenate([w_hi, w_lo], axis=0), h_hi)
    logits = hi_terms[:ne] + hi_terms[ne:] + _dot_nt(w_hi, h_lo) + rb_ref[...]
    eio = lax.broadcasted_iota(I32, logits.shape, 0).astype(F32)
    vals, idxs, sels = [], [], []
    for _ in range(TOP_K):
        mx = jnp.max(logits, axis=0, keepdims=True)
        ix = jnp.min(jnp.where(logits == mx, eio, float(ne)), axis=0, keepdims=True)
        sel = eio == ix
        logits = jnp.where(sel, -jnp.inf, logits)
        vals.append(mx)
        idxs.append(ix)
        sels.append(sel)
    exps = [jnp.exp(v - vals[0]) for v in vals]
    den = functools.reduce(lambda a, b_: a + b_, exps)
    chosen = functools.reduce(jnp.logical_or, sels)
    onehot = jnp.where(chosen, 1.0, 0.0)
    before = _dot(onehot.astype(BF16), tri_ref[...]) + run[...]
    ranks = [jnp.sum(jnp.where(sel, before, 0.0), axis=0, keepdims=True) for sel in sels]
    run[...] = run[...] + jnp.sum(onehot, axis=1, keepdims=True)
    idx_ref[...] = jnp.concatenate(idxs, axis=0).astype(I32)
    p_ref[...] = jnp.concatenate([e / den for e in exps]
                                 + [jnp.zeros((p_ref.shape[0] - TOP_K, den.shape[1]), F32)], axis=0)
    rank_ref[...] = jnp.concatenate(ranks, axis=0).astype(I32)
    cnt_ref[...] = jnp.broadcast_to(run[...], cnt_ref.shape)


def _experts_kernel(be_ref, nu_ref, nx_ref, par_ref, nv_ref, xs_ref, wgu_hbm, bgu_ref, wd_hbm, bd_ref, y_ref,
                    wgu_f, wd_f, sems, wgu_b, wd_b, *, layer):
    j = pl.program_id(0)

    def weight_copies(expert, slot):
        return (pltpu.make_async_copy(wgu_hbm.at[layer, expert], wgu_f.at[slot], sems.at[0, slot]),
                pltpu.make_async_copy(wd_hbm.at[layer, expert], wd_f.at[slot], sems.at[1, slot]))

    @pl.when(j >= nu_ref[0])
    def _():
        y_ref[...] = jnp.zeros(y_ref.shape, U32)

    @pl.when(j < nu_ref[0])
    def _():
        expert, slot = be_ref[j], par_ref[j]
        new_expert = jnp.logical_or(j == 0, expert != be_ref[jnp.maximum(j - 1, 0)])

        @pl.when(j == 0)
        def _():
            for cp in weight_copies(expert, slot):
                cp.start()

        @pl.when(new_expert)
        def _():
            for cp in weight_copies(expert, slot):
                cp.wait()

            @pl.when(nx_ref[j] >= 0)
            def _():
                for cp in weight_copies(nx_ref[j], 1 - slot):
                    cp.start()

            wgu_b[...] = wgu_f[slot].astype(BF16)
            wd_b[...] = wd_f[slot].astype(BF16)

        def mlp(chunks):
            xb = _unpack_row_chunks(chunks).astype(BF16)
            gu = _dot(xb, wgu_b[...]) + bgu_ref[0, 0]
            f = gu.shape[1] // 2
            glu = jnp.minimum(gu[:, :f], SWIGLU_LIMIT)
            lin = jnp.clip(gu[:, f:], -SWIGLU_LIMIT, SWIGLU_LIMIT)
            act = glu * jax.nn.sigmoid(SWIGLU_ALPHA * glu) * (lin + 1.0)
            y = _dot(act.astype(BF16), wd_b[...]) + bd_ref[0, 0]
            _store_row_chunks(y_ref, _pack_row_chunks(y))

        bm = xs_ref.shape[1]
        n_rows = nv_ref[j]

        @pl.when(n_rows == bm)
        def _():
            mlp(_load_row_chunks(xs_ref))

        @pl.when(n_rows < bm)
        def _():
            is_token = lax.broadcasted_iota(I32, xs_ref.shape[1:], 0) < n_rows
            mlp([jnp.where(is_token, ch, jnp.uint32(0)) for ch in _load_row_chunks(xs_ref)])


def _combine_kernel(ya_ref, x_ref, m_ref, p_ref, lg_ref, lb_ref, *rest, alpha):
    o_ref = rest[-1]
    p_rows = p_ref[...]
    eye = (lax.broadcasted_iota(I32, (p_rows.shape[0], LANES), 0)
           == lax.broadcasted_iota(I32, (p_rows.shape[0], LANES), 1)).astype(F32)
    p = lax.dot_general(p_rows, eye, (((0,), (0,)), ((), ())), preferred_element_type=F32,
                        precision=HIGHEST)
    y = None
    for kk in range(TOP_K):
        yk = p[:, kk:kk + 1] * _unpack_row_chunks(_load_row_chunks(ya_ref, (kk,)))
        y = yk if y is None else y + yk
    gt = m_ref[0, 5:6, :]
    o_ref[...] = _layer_norm(alpha * x_ref[...] + (1.0 + gt) * y, lg_ref[...], lb_ref[...])


def _moe_layer(x, m, layer, router_w, router_b, w_gu, b_gu, w_down, b_down, ln_g, ln_b, alpha, cfg):
    b, s, d = x.shape
    n_tok = b * s
    ne = router_w.shape[-1]
    f2 = w_gu.shape[-1]
    ff = f2 // 2
    assert d % (2 * LANES) == 0 and w_down.shape[-2:] == (ff, d)
    rc = d // (2 * LANES)
    xf = x.reshape(n_tok, d)
    const2 = lambda i: (0, 0)

    tt = cfg["router_tt"]
    assert s % tt == 0
    tri = jnp.triu(jnp.ones((tt, tt), BF16), k=1)
    hp, idx, probs, rank, cnt = pl.pallas_call(
        _router_kernel,
        grid=(n_tok // tt,),
        in_specs=[pl.BlockSpec((tt, d), lambda i: (i, 0)),
                  pl.BlockSpec((1, N_MOD, d), lambda i: (i * tt // s, 0, 0)),
                  pl.BlockSpec((ne, d), const2),
                  pl.BlockSpec((ne, 1), const2),
                  pl.BlockSpec((tt, tt), const2)],
        out_specs=[pl.BlockSpec((rc, tt, LANES), lambda i: (0, i, 0)),
                   pl.BlockSpec((TOP_K, tt), lambda i: (0, i)),
                   pl.BlockSpec((SUBLANES, tt), lambda i: (0, i)),
                   pl.BlockSpec((TOP_K, tt), lambda i: (0, i)),
                   pl.BlockSpec((ne, LANES), const2)],
        out_shape=[jax.ShapeDtypeStruct((rc, n_tok, LANES), U32),
                   jax.ShapeDtypeStruct((TOP_K, n_tok), I32),
                   jax.ShapeDtypeStruct((SUBLANES, n_tok), F32),
                   jax.ShapeDtypeStruct((TOP_K, n_tok), I32),
                   jax.ShapeDtypeStruct((ne, LANES), F32)],
        scratch_shapes=[pltpu.VMEM((ne, 1), F32)],
        compiler_params=_cparams(1),
        name="moe_router",
    )(xf, m, router_w[layer].T, router_b[layer].reshape(ne, 1), tri)

    bm = cfg["expert_bm"]
    cap = n_tok * TOP_K + ne * bm
    nb = cap // bm
    counts = cnt[:, 0].astype(I32)
    padded = (counts + bm - 1) // bm * bm
    pad_end = jnp.cumsum(padded)
    pad_start = pad_end - padded
    eids = jnp.arange(ne, dtype=I32)[:, None, None]
    dest = rank + jnp.sum(jnp.where(idx[None] == eids, pad_start[:, None, None], 0), axis=0)
    block_start = jnp.arange(nb, dtype=I32) * bm
    block_expert = jnp.minimum(jnp.sum(block_start[None, :] >= pad_end[:, None], axis=0), ne - 1).astype(I32)
    n_used = (pad_end[-1:] // bm).astype(I32)
    first_of_expert = jnp.concatenate([jnp.ones((1,), bool), block_expert[1:] != block_expert[:-1]])
    weight_slot = ((jnp.cumsum(first_of_expert.astype(I32)) - 1) % 2).astype(I32)
    next_block = pad_end[block_expert] // bm
    next_expert = jnp.where(next_block < n_used[0], block_expert[jnp.minimum(next_block, nb - 1)], -1).astype(I32)

    plane_offset = (jnp.arange(rc * n_tok, dtype=I32) // n_tok * cap)[None, :]
    slot_rows = [jnp.tile(dest[kk:kk + 1], (1, rc)) + plane_offset for kk in range(TOP_K)]

    xs = _sc_scatter_rows(hp.reshape(rc * n_tok, LANES), slot_rows, rc * cap).reshape(rc, cap, LANES)

    last = lambda j, be, nu, *_: jnp.minimum(j, nu[0] - 1)
    bias_map = lambda j, be, nu, *_: (layer, be[last(j, be, nu)], 0, 0)
    block_tokens = jnp.clip(pad_start[block_expert] + counts[block_expert] - block_start, 0, bm).astype(I32)
    y_rows = pl.pallas_call(
        functools.partial(_experts_kernel, layer=layer),
        grid_spec=pltpu.PrefetchScalarGridSpec(
            num_scalar_prefetch=5,
            grid=(nb,),
            in_specs=[pl.BlockSpec((rc, bm, LANES), lambda j, be, nu, *_: (0, last(j, be, nu), 0)),
                      pl.BlockSpec(memory_space=pl.ANY),
                      pl.BlockSpec((1, 1, 1, f2), bias_map),
                      pl.BlockSpec(memory_space=pl.ANY),
                      pl.BlockSpec((1, 1, 1, d), bias_map)],
            out_specs=pl.BlockSpec((rc, bm, LANES), lambda j, *_: (0, j, 0)),
            scratch_shapes=[pltpu.VMEM((2, d, f2), F32), pltpu.VMEM((2, ff, d), F32),
                            pltpu.SemaphoreType.DMA((2, 2)),
                            pltpu.VMEM((d, f2), BF16), pltpu.VMEM((ff, d), BF16)]),
        out_shape=jax.ShapeDtypeStruct((rc, cap, LANES), U32),
        compiler_params=_cparams(1),
        name="moe_experts",
    )(block_expert, n_used, next_expert, weight_slot, block_tokens, xs, w_gu, b_gu.reshape(*b_gu.shape[:2], 1, f2),
      w_down, b_down.reshape(*b_down.shape[:2], 1, d))

    tt = cfg["combine_tt"]
    n_groups = cfg["combine_groups"] if n_tok % (cfg["combine_groups"] * max(tt, SC_WINDOW * SC_WORKERS)) == 0 else 1
    n_grp = n_tok // n_groups
    tiles = n_grp // tt
    assert s % tt == 0 and n_grp % tt == 0
    y_table = y_rows.reshape(rc * cap, LANES)
    grp_rows = (dest.reshape(TOP_K, n_groups, n_grp).transpose(1, 0, 2)[:, :, None, :]
                + (jnp.arange(rc, dtype=I32) * cap)[None, None, :, None])
    out = None
    for gi in range(n_groups):
        rows_g = grp_rows[gi].reshape(1, TOP_K * rc * n_grp)
        y_assign = _sc_gather_rows(y_table, rows_g).reshape(TOP_K, rc, n_grp, LANES)
        first = gi * tiles
        operands = [y_assign, xf, m, probs, ln_g.reshape(1, d), ln_b.reshape(1, d)]
        in_specs = [pl.BlockSpec((TOP_K, rc, tt, LANES), lambda i: (0, 0, i, 0)),
                    pl.BlockSpec((tt, d), lambda i, first=first: (first + i, 0)),
                    pl.BlockSpec((1, N_MOD, d), lambda i, first=first: ((first + i) * tt // s, 0, 0)),
                    pl.BlockSpec((SUBLANES, tt), lambda i, first=first: (0, first + i)),
                    pl.BlockSpec((1, d), const2),
                    pl.BlockSpec((1, d), const2)]
        aliases = {}
        if out is not None:
            operands.append(out)
            in_specs.append(pl.BlockSpec(memory_space=pl.ANY))
            aliases = {len(operands) - 1: 0}
        out = pl.pallas_call(
            functools.partial(_combine_kernel, alpha=alpha),
            grid=(tiles,),
            in_specs=in_specs,
            out_specs=pl.BlockSpec((tt, d), lambda i, first=first: (first + i, 0)),
            out_shape=jax.ShapeDtypeStruct((n_tok, d), F32),
            input_output_aliases=aliases,
            compiler_params=_cparams(1),
            name="moe_combine",
        )(*operands)
    return out.reshape(b, s, d)


def kernel(x, c, ada_w, ada_b, ln_g, ln_b, pool_w, pool_scale, dn_w_in, dn_conv_w, dn_a_log, dn_dt_bias,
           dn_norm_w, dn_w_out, router_w, router_b, exp_w_gu, exp_b_gu, exp_w_down, exp_b_down):
    b, s, d = x.shape
    depth = ada_w.shape[0]
    alpha = (2 * depth) ** 0.25
    cfg = _tile_config(s, b * s)
    mods = _mods(c, ada_w, ada_b)
    n_mixers = 2
    for i in range(depth):
        m = mods[i]
        j = i // n_mixers
        if i % n_mixers == 0:
            x = _pool_layer(x, m, pool_w[j], pool_scale[j], ln_g[i, 0], ln_b[i, 0], alpha, cfg)
        else:
            x = _deltanet_layer(x, m, dn_w_in[j], dn_conv_w[j], dn_a_log[j], dn_dt_bias[j], dn_norm_w[j],
                                dn_w_out[j], ln_g[i, 0], ln_b[i, 0], alpha, cfg)
        x = _moe_layer(x, m, i, router_w, router_b, exp_w_gu, exp_b_gu, exp_w_down, exp_b_down,
                       ln_g[i, 1], ln_b[i, 1], alpha, cfg)
    return x
```

```python
import functools

import jax
import jax.numpy as jnp
from jax import lax
from jax.experimental import pallas as pl
from jax.experimental.pallas import tpu as pltpu
from jax.experimental.pallas import tpu_sc as plsc

F32 = jnp.float32
BF16 = jnp.bfloat16
I32 = jnp.int32
U32 = jnp.uint32

N_MOD = 6
POOL_WINDOWS = (2, 4, 8, 16)
DN_HEADS = 8
DN_HEAD_DIM = 128
DN_CONV = 4
TOP_K = 4
SWIGLU_LIMIT = 7.0
SWIGLU_ALPHA = 1.702
LN_EPS = 1e-5
RMS_EPS = 1e-6

LANES = 128
SUBLANES = 8
VMEM_LIMIT_BYTES = 56 * 1024 * 1024
SC_WORKERS = 32
SC_WINDOW = 128

DN_CHUNK = LANES
DN_HEAD_GROUP = 8
POOL_HALO = 32
CONV_HALO = SUBLANES

HIGHEST = lax.Precision.HIGHEST


def _tile_config(seq, n_tok):
    return dict(
        pool_ts=min(512, seq),
        dn_in_ts=min(512, seq),
        dn_in_sub=256,
        dn_core_ts=min(512, seq),
        dn_out_ts=min(512, seq),
        router_tt=min(1024, seq),
        expert_bm=512,
        combine_tt=min(256, seq),
        combine_groups=4,
    )


def _cparams(n_axes):
    return pltpu.CompilerParams(dimension_semantics=("arbitrary",) * n_axes,
                                vmem_limit_bytes=VMEM_LIMIT_BYTES)


def _layer_norm(v, g, b):
    mu = jnp.mean(v, axis=-1, keepdims=True)
    d = v - mu
    var = jnp.mean(d * d, axis=-1, keepdims=True)
    return d * lax.rsqrt(var + LN_EPS) * g + b


def _dot(a, b):
    return jnp.dot(a, b, preferred_element_type=F32)


def _dot_nt(a, b):
    return lax.dot_general(a, b, (((1,), (1,)), ((), ())), preferred_element_type=F32)


def _dot_tn(a, b):
    return lax.dot_general(a, b, (((0,), (0,)), ((), ())), preferred_element_type=F32)


def _bdot(a, b):
    return lax.dot_general(a, b, (((2,), (1,)), ((0,), (0,))), preferred_element_type=F32)


def _bdot_nt(a, b):
    return lax.dot_general(a, b, (((2,), (2,)), ((0,), (0,))), preferred_element_type=F32)


def _mods_kernel(c_ref, w_ref, b_ref, o_ref):
    c = c_ref[...]
    c_act = c * jax.nn.sigmoid(c)
    o_ref[0] = jnp.dot(c_act, w_ref[0], preferred_element_type=F32, precision=HIGHEST) + b_ref[0]


def _mods(c, ada_w, ada_b):
    depth, d, n = ada_w.shape
    b = c.shape[0]
    tn = 2048 if n % 2048 == 0 else n
    out = pl.pallas_call(
        _mods_kernel,
        grid=(depth, n // tn),
        in_specs=[pl.BlockSpec((b, d), lambda i, j: (0, 0)),
                  pl.BlockSpec((1, d, tn), lambda i, j: (i, 0, j)),
                  pl.BlockSpec((1, 1, tn), lambda i, j: (i, 0, j))],
        out_specs=pl.BlockSpec((1, b, tn), lambda i, j: (i, 0, j)),
        out_shape=jax.ShapeDtypeStruct((depth, b, n), F32),
        compiler_params=_cparams(2),
        name="adaln_mods",
    )(c, ada_w, ada_b.reshape(depth, 1, n))
    return out.reshape(depth, b, N_MOD, d)


def _pool_kernel(x_ref, m_ref, pw_ref, ps_ref, lg_ref, lb_ref, o_ref, e1, ea, eb, *, ts, alpha):
    s = pl.program_id(1)
    d = x_ref.shape[-1]
    gw = d // len(POOL_WINDOWS)
    halo = POOL_HALO
    rows = halo + ts
    x = x_ref[0]
    sh, sc, gt = m_ref[0, 0:1, :], m_ref[0, 1:2, :], m_ref[0, 2:3, :]
    h = x * (1.0 + sc) + sh

    @pl.when(s == 0)
    def _():
        e1[0:halo, :] = jnp.zeros((halo, d), F32)

    e1[halo:rows, :] = h
    ea[8:rows, :] = e1[8:rows, :] + e1[7:rows - 1, :]
    eb[16:rows, gw:] = ea[16:rows, gw:] + ea[14:rows - 2, gw:]
    ea[24:rows, 2 * gw:] = eb[24:rows, 2 * gw:] + eb[20:rows - 4, 2 * gw:]
    eb[32:rows, 3 * gw:] = ea[32:rows, 3 * gw:] + ea[24:rows - 8, 3 * gw:]

    pos = s * ts + lax.broadcasted_iota(I32, (ts, 1), 0)
    outs = []
    for g, win in enumerate(POOL_WINDOWS):
        src = ea if g % 2 == 0 else eb
        cols = slice(g * gw, (g + 1) * gw)
        cnt = jnp.minimum(pos + 1, win).astype(F32)
        pooled = src[halo:rows, cols] / cnt - h[:, cols]
        outs.append(_dot(pooled.astype(BF16), pw_ref[g]))
    y = jnp.concatenate(outs, axis=1) * ps_ref[...]
    o_ref[0] = _layer_norm(alpha * x + (1.0 + gt) * y, lg_ref[...], lb_ref[...])
    e1[0:halo, :] = e1[ts:rows, :]


def _pool_layer(x, m, pool_w, pool_scale, ln_g, ln_b, alpha, cfg):
    b, s, d = x.shape
    ts = cfg["pool_ts"]
    g, gw, _ = pool_w.shape
    assert POOL_WINDOWS == (2, 4, 8, 16) and g == len(POOL_WINDOWS) and s % ts == 0 and ts >= POOL_HALO
    row = lambda v: v.reshape(1, d)
    return pl.pallas_call(
        functools.partial(_pool_kernel, ts=ts, alpha=alpha),
        grid=(b, s // ts),
        in_specs=[pl.BlockSpec((1, ts, d), lambda i, j: (i, j, 0)),
                  pl.BlockSpec((1, N_MOD, d), lambda i, j: (i, 0, 0)),
                  pl.BlockSpec((g, gw, gw), lambda i, j: (0, 0, 0)),
                  pl.BlockSpec((1, d), lambda i, j: (0, 0)),
                  pl.BlockSpec((1, d), lambda i, j: (0, 0)),
                  pl.BlockSpec((1, d), lambda i, j: (0, 0))],
        out_specs=pl.BlockSpec((1, ts, d), lambda i, j: (i, j, 0)),
        out_shape=jax.ShapeDtypeStruct((b, s, d), F32),
        scratch_shapes=[pltpu.VMEM((POOL_HALO + ts, d), F32)] * 3,
        compiler_params=_cparams(2),
        name="pool_layer",
    )(x, m, pool_w.astype(BF16), row(pool_scale), row(ln_g), row(ln_b))


def _dn_in_kernel(x_ref, m_ref, wm_ref, ws_ref, cw_ref, av_ref, dv_ref,
                  q_ref, k_ref, v_ref, z_ref, bg_ref, ext, *, ts, sr):
    s = pl.program_id(1)
    nh, dh = DN_HEADS, DN_HEAD_DIM
    w = nh * dh
    halo = CONV_HALO
    sh, sc = m_ref[0, 0:1, :], m_ref[0, 1:2, :]

    @pl.when(s == 0)
    def _():
        ext[0:halo, :] = jnp.zeros((halo, 3 * w), F32)

    def project(i):
        rows = slice(i * sr, (i + 1) * sr)
        h = (x_ref[0, rows, :] * (1.0 + sc) + sh).astype(BF16)
        proj = _dot(h, wm_ref[...])
        ext[halo + i * sr:halo + (i + 1) * sr, :] = proj[:, :3 * w]
        z_ref[0, rows, :] = proj[:, 3 * w:]
        small = _dot(h, ws_ref[...])
        lane = lax.broadcasted_iota(I32, small.shape, 1)
        beta = jax.nn.sigmoid(small)
        g = -jnp.exp(av_ref[...]) * jax.nn.softplus(small + dv_ref[...])
        bg_ref[0, rows, :] = jnp.where(lane < nh, beta, jnp.where(lane < 2 * nh, g, 0.0))

    def mix(i):
        rows = slice(i * sr, (i + 1) * sr)
        base = halo - (DN_CONV - 1) + i * sr
        conv = cw_ref[0:1, :] * ext[base:base + sr, :]
        for j in range(1, DN_CONV):
            conv = conv + cw_ref[j:j + 1, :] * ext[base + j:base + j + sr, :]
        act = conv * jax.nn.sigmoid(conv)
        for hd in range(nh):
            qh = act[:, hd * dh:(hd + 1) * dh]
            kh = act[:, w + hd * dh:w + (hd + 1) * dh]
            q_ref[0, hd, rows, :] = (qh * lax.rsqrt(jnp.sum(qh * qh, axis=-1, keepdims=True) + RMS_EPS)
                                     * (dh ** -0.5))
            k_ref[0, hd, rows, :] = kh * lax.rsqrt(jnp.sum(kh * kh, axis=-1, keepdims=True) + RMS_EPS)
            v_ref[0, hd, rows, :] = act[:, 2 * w + hd * dh:2 * w + (hd + 1) * dh]

    n_sub = ts // sr
    project(0)
    for i in range(1, n_sub):
        project(i)
        mix(i - 1)
    mix(n_sub - 1)
    ext[0:halo, :] = ext[ts:ts + halo, :]


def _dn_core_kernel(q_ref, k_ref, v_ref, bg_ref, o_ref,
                    st, bb, gcb, gt, gl_s, u_s, wq_s, ai_s, kd_s, *, ts):
    s = pl.program_id(1)
    nh, dh, c = DN_HEADS, DN_HEAD_DIM, DN_CHUNK
    nc = ts // c

    @pl.when(s == 0)
    def _():
        st[...] = jnp.zeros(st.shape, F32)

    bgv = bg_ref[0]
    lane = lax.broadcasted_iota(I32, bgv.shape, 1)
    rowc = lax.broadcasted_iota(I32, bgv.shape, 0) % c
    gc = jnp.where(lane >= nh, bgv, 0.0)
    shift = 1
    while shift < c:
        gc = gc + jnp.where(rowc >= shift, pltpu.roll(gc, shift, 0), 0.0)
        shift *= 2
    gc_t = gc.T
    for hd in range(nh):
        for ci in range(nc):
            gt[hd * nc + ci] = jnp.broadcast_to(gc_t[nh + hd:nh + hd + 1, ci * c:(ci + 1) * c], (SUBLANES, c))
        bb[hd] = jnp.broadcast_to(bgv[:, hd:hd + 1], (ts, LANES))
        gcb[hd] = jnp.broadcast_to(gc[:, nh + hd:nh + hd + 1], (ts, LANES))

    ri = lax.broadcasted_iota(I32, (c, c), 0)
    ci_ = lax.broadcasted_iota(I32, (c, c), 1)
    tril = ri >= ci_
    eye = (ri == ci_).astype(F32)
    off_masks = []
    blk = 1
    while blk < c:
        off_masks.append((ri // blk != ci_ // blk) & (ri // (2 * blk) == ci_ // (2 * blk)) & (ri > ci_))
        blk *= 2

    hg = DN_HEAD_GROUP
    nb = hg * nc
    n_pairs = nb // 2
    two = lambda msk: jnp.concatenate([msk, msk], axis=1)
    tril2, eye2, off2 = two(tril), two(eye), [two(msk) for msk in off_masks]

    def pair(a):
        a = a.reshape(n_pairs, 2, a.shape[1], a.shape[2])
        return jnp.concatenate([a[:, 0], a[:, 1]], axis=-1)

    def unpair(a):
        w = a.shape[-1] // 2
        return jnp.stack([a[..., :w], a[..., w:]], axis=1).reshape(nb, a.shape[1], w)

    def block_diag(a):
        left = lax.broadcasted_iota(I32, a.shape[1:], 1) < a.shape[-1] // 2
        zero = jnp.zeros_like(a)
        return jnp.concatenate([jnp.where(left, a, zero), jnp.where(left, zero, a)], axis=1)

    def group_body(gi, carry):
        heads = pl.ds(gi * hg, hg)
        chunks = lambda a: a.reshape(nb, c, a.shape[-1])
        per_head = lambda a: a.reshape(hg, ts, a.shape[-1])
        q, k, v = chunks(q_ref[0, heads]), chunks(k_ref[0, heads]), chunks(v_ref[0, heads])
        beta, gch = chunks(bb[heads]), chunks(gcb[heads])
        egh = jnp.exp(gch)
        g_last = jnp.broadcast_to(gch[:, c - 1:c, :], gch.shape)
        kb = k * beta
        vb = v * beta
        kd_s[heads] = per_head((k * jnp.exp(g_last - gch)).astype(BF16))
        gl_s[heads] = jnp.broadcast_to(egh[:, c - 1:c, :], (nb, SUBLANES, LANES)).reshape(hg, nc * SUBLANES, LANES)
        grow = gt[pl.ds(gi * nb, nb), 0:1, :]
        dec = jnp.where(tril2, jnp.exp(jnp.where(tril2, pair(gch) - pair(grow), 0.0)), 0.0)
        kq = jnp.concatenate([pair(kb), pair(q)], axis=1).astype(BF16)
        a_all = _bdot_nt(kq, block_diag(pair(k).astype(BF16)))
        ai_s[heads] = per_head(unpair((a_all[:, c:] * dec).astype(BF16)))
        l_mat = a_all[:, :c] * dec
        t_inv = eye2 - jnp.where(off2[0], l_mat, 0.0)
        for off in off2[1:]:
            tb = t_inv.astype(BF16)
            l_off = jnp.where(off, l_mat, 0.0).astype(BF16)
            t_inv = t_inv - _bdot(tb, block_diag(_bdot(l_off, block_diag(tb)).astype(BF16)))
        rhs = jnp.concatenate([vb, kb * egh], axis=2).astype(BF16)
        uw = unpair(_bdot(t_inv.astype(BF16), block_diag(pair(rhs))))
        u_s[heads] = per_head(uw[:, :, :dh])
        wq = jnp.concatenate([uw[:, :, dh:], q * egh], axis=1).astype(BF16)
        wq_s[heads] = wq.reshape(hg, 2 * ts, dh)
        return carry

    lax.fori_loop(0, nh // hg, group_body, 0)

    def chunk_body(ci, carry):
        rows = pl.ds(pl.multiple_of(ci * c, c), c)
        rows2 = pl.ds(pl.multiple_of(ci * 2 * c, 2 * c), 2 * c)
        heads = range(nh)
        states = [st[hd] for hd in heads]
        ws_qs = [_dot(wq_s[hd, rows2, :], states[hd].astype(BF16)) for hd in heads]
        v_nb = [(u_s[hd, rows, :] - ws_qs[hd][:c]).astype(BF16) for hd in heads]
        outs = [ws_qs[hd][c:] + _dot(ai_s[hd, rows, :], v_nb[hd]) for hd in heads]
        new_states = [states[hd] * gl_s[hd, pl.ds(pl.multiple_of(ci * SUBLANES, SUBLANES), 1), :]
                      + _dot_tn(kd_s[hd, rows, :], v_nb[hd]) for hd in heads]
        for hd in heads:
            o_ref[0, hd, rows, :] = outs[hd]
            st[hd] = new_states[hd]
        return carry

    lax.fori_loop(0, nc, chunk_body, 0)


def _dn_out_kernel(o_ref, z_ref, x_ref, m_ref, nw_ref, wo_ref, lg_ref, lb_ref, out_ref, *, alpha):
    nh, dh = DN_HEADS, DN_HEAD_DIM
    z = z_ref[0]
    parts = []
    for hd in range(nh):
        oh = o_ref[0, hd]
        on = oh * lax.rsqrt(jnp.mean(oh * oh, axis=-1, keepdims=True) + RMS_EPS) * nw_ref[...]
        zh = z[:, hd * dh:(hd + 1) * dh]
        parts.append((on * (zh * jax.nn.sigmoid(zh))).astype(BF16))
    y = _dot(jnp.concatenate(parts, axis=1), wo_ref[...])
    x = x_ref[0]
    gt = m_ref[0, 2:3, :]
    out_ref[0] = _layer_norm(alpha * x + (1.0 + gt) * y, lg_ref[...], lb_ref[...])


def _deltanet_layer(x, m, w_in, conv_w, a_log, dt_bias, norm_w, w_out, ln_g, ln_b, alpha, cfg):
    b, s, d = x.shape
    nh, dh = DN_HEADS, DN_HEAD_DIM
    w = nh * dh
    assert w_in.shape == (d, 4 * w + 2 * nh) and conv_w.shape == (DN_CONV, 3 * w) and 2 * nh <= LANES
    w_main = w_in[:, :4 * w].astype(BF16)
    w_small = jnp.pad(w_in[:, 4 * w:], ((0, 0), (0, LANES - 2 * nh))).astype(BF16)
    avec = jnp.zeros((1, LANES), F32).at[0, nh:2 * nh].set(a_log.astype(F32))
    dvec = jnp.zeros((1, LANES), F32).at[0, nh:2 * nh].set(dt_bias.astype(F32))
    row = lambda v: v.reshape(1, -1)
    const2 = lambda i, j: (0, 0)

    ts = cfg["dn_in_ts"]
    assert s % ts == 0 and ts >= CONV_HALO
    head_major = jax.ShapeDtypeStruct((b, nh, s, dh), F32)
    hm_spec = lambda t: pl.BlockSpec((1, nh, t, dh), lambda i, j: (i, 0, j, 0))
    q, k, v, z, bg = pl.pallas_call(
        functools.partial(_dn_in_kernel, ts=ts, sr=min(cfg["dn_in_sub"], ts)),
        grid=(b, s // ts),
        in_specs=[pl.BlockSpec((1, ts, d), lambda i, j: (i, j, 0)),
                  pl.BlockSpec((1, N_MOD, d), lambda i, j: (i, 0, 0)),
                  pl.BlockSpec((d, 4 * w), const2),
                  pl.BlockSpec((d, LANES), const2),
                  pl.BlockSpec((DN_CONV, 3 * w), const2),
                  pl.BlockSpec((1, LANES), const2),
                  pl.BlockSpec((1, LANES), const2)],
        out_specs=[hm_spec(ts), hm_spec(ts), hm_spec(ts),
                   pl.BlockSpec((1, ts, w), lambda i, j: (i, j, 0)),
                   pl.BlockSpec((1, ts, LANES), lambda i, j: (i, j, 0))],
        out_shape=[head_major, head_major, head_major,
                   jax.ShapeDtypeStruct((b, s, w), F32),
                   jax.ShapeDtypeStruct((b, s, LANES), F32)],
        scratch_shapes=[pltpu.VMEM((CONV_HALO + ts, 3 * w), F32)],
        compiler_params=_cparams(2),
        name="deltanet_in",
    )(x, m, w_main, w_small, conv_w, avec, dvec)

    ts = cfg["dn_core_ts"]
    assert s % ts == 0 and ts % DN_CHUNK == 0 and dh == DN_CHUNK
    per_head = pltpu.VMEM((nh, ts, dh), F32)
    o = pl.pallas_call(
        functools.partial(_dn_core_kernel, ts=ts),
        grid=(b, s // ts),
        in_specs=[hm_spec(ts), hm_spec(ts), hm_spec(ts),
                  pl.BlockSpec((1, ts, LANES), lambda i, j: (i, j, 0))],
        out_specs=hm_spec(ts),
        out_shape=head_major,
        scratch_shapes=[pltpu.VMEM((nh, dh, dh), F32),
                        per_head, per_head,
                        pltpu.VMEM((nh * (ts // DN_CHUNK), SUBLANES, DN_CHUNK), F32),
                        pltpu.VMEM((nh, ts // DN_CHUNK * SUBLANES, LANES), F32),
                        per_head,
                        pltpu.VMEM((nh, 2 * ts, dh), BF16),
                        pltpu.VMEM((nh, ts, DN_CHUNK), BF16),
                        pltpu.VMEM((nh, ts, dh), BF16)],
        compiler_params=_cparams(2),
        name="deltanet_core",
    )(q, k, v, bg)

    ts = cfg["dn_out_ts"]
    assert s % ts == 0
    return pl.pallas_call(
        functools.partial(_dn_out_kernel, alpha=alpha),
        grid=(b, s // ts),
        in_specs=[hm_spec(ts),
                  pl.BlockSpec((1, ts, w), lambda i, j: (i, j, 0)),
                  pl.BlockSpec((1, ts, d), lambda i, j: (i, j, 0)),
                  pl.BlockSpec((1, N_MOD, d), lambda i, j: (i, 0, 0)),
                  pl.BlockSpec((1, dh), const2),
                  pl.BlockSpec((w, d), const2),
                  pl.BlockSpec((1, d), const2),
                  pl.BlockSpec((1, d), const2)],
        out_specs=pl.BlockSpec((1, ts, d), lambda i, j: (i, j, 0)),
        out_shape=jax.ShapeDtypeStruct((b, s, d), F32),
        compiler_params=_cparams(2),
        name="deltanet_out",
    )(o, z, x, m, row(norm_w), w_out.astype(BF16), row(ln_g), row(ln_b))


def _pack_row_chunks(v):
    half = v.shape[-1] // 2
    bits = pltpu.bitcast(v.astype(BF16).astype(F32), U32)
    packed = (bits[:, :half] >> 16) | (bits[:, half:] & jnp.uint32(0xFFFF0000))
    return [packed[:, i * LANES:(i + 1) * LANES] for i in range(half // LANES)]


def _unpack_row_chunks(chunks):
    lo = [pltpu.bitcast(c << 16, F32) for c in chunks]
    hi = [pltpu.bitcast(c & jnp.uint32(0xFFFF0000), F32) for c in chunks]
    return jnp.concatenate(lo + hi, axis=1)


def _store_row_chunks(ref, chunks):
    for i, ch in enumerate(chunks):
        ref[i] = ch


def _load_row_chunks(ref, lead=()):
    return [ref[(*lead, i)] for i in range(ref.shape[len(lead)])]


def _sc_mesh():
    return plsc.VectorSubcoreMesh(core_axis_name="c", subcore_axis_name="s")


def _sc_gather_rows(table, idx):
    n = idx.shape[1]
    assert table.shape[1] == LANES and idx.shape[0] == 1 and n % (SC_WINDOW * SC_WORKERS) == 0

    @functools.partial(pl.kernel, out_type=jax.ShapeDtypeStruct((n, LANES), table.dtype), mesh=_sc_mesh(),
                       name="sc_gather_rows")
    def gather(table_hbm, idx_hbm, out_hbm):
        def body(idx_vmem, out_vmem):
            pltpu.sync_copy(table_hbm.at[idx_vmem.at[0]], out_vmem)

        pltpu.emit_pipeline(
            body,
            grid=(n // SC_WINDOW,),
            in_specs=[pl.BlockSpec((1, SC_WINDOW), lambda i: (0, i))],
            out_specs=[pl.BlockSpec((SC_WINDOW, LANES), lambda i: (i, 0))],
            core_axis_name=("c", "s"),
            dimension_semantics=(pltpu.PARALLEL,),
        )(idx_hbm, out_hbm)

    return gather(table, idx)


def _sc_scatter_rows(x, idx_list, n_out):
    n = x.shape[0]
    assert x.shape[1] == LANES and n % (SC_WINDOW * SC_WORKERS) == 0
    assert all(idx.shape == (1, n) for idx in idx_list)

    @functools.partial(pl.kernel, out_type=jax.ShapeDtypeStruct((n_out, LANES), x.dtype), mesh=_sc_mesh(),
                       name="sc_scatter_rows")
    def scatter(x_hbm, *refs):
        idx_hbms, out_hbm = refs[:-1], refs[-1]

        def body(x_vmem, *idx_vmems):
            for idx_vmem in idx_vmems:
                pltpu.sync_copy(x_vmem, out_hbm.at[idx_vmem.at[0]])

        pltpu.emit_pipeline(
            body,
            grid=(n // SC_WINDOW,),
            in_specs=[pl.BlockSpec((SC_WINDOW, LANES), lambda i: (i, 0))]
                     + [pl.BlockSpec((1, SC_WINDOW), lambda i: (0, i))] * len(idx_list),
            out_specs=[],
            core_axis_name=("c", "s"),
            dimension_semantics=(pltpu.PARALLEL,),
        )(x_hbm, *idx_hbms)

    return scatter(x, *idx_list)


def _router_kernel(x_ref, m_ref, rwt_ref, rb_ref, tri_ref,
                   hp_ref, idx_ref, p_ref, rank_ref, cnt_ref, run):
    i = pl.program_id(0)

    @pl.when(i == 0)
    def _():
        run[...] = jnp.zeros(run.shape, F32)

    x = x_ref[...]
    d = x.shape[-1]
    sh, sc = m_ref[0, 3:4, :], m_ref[0, 4:5, :]
    h = x * (1.0 + sc) + sh
    _store_row_chunks(hp_ref, _pack_row_chunks(h))

    h_hi = h.astype(BF16)
    h_lo = (h - h_hi.astype(F32)).astype(BF16)
    rw = rwt_ref[...]
    w_hi = rw.astype(BF16)
    w_lo = (rw - w_hi.astype(F32)).astype(BF16)
    ne = rw.shape[0]
    hi_terms = _dot_nt(jnp.concatenate([w_hi, w_lo], axis=0), h_hi)
    logits = hi_terms[:ne] + hi_terms[ne:] + _dot_nt(w_hi, h_lo) + rb_ref[...]
    eio = lax.broadcasted_iota(I32, logits.shape, 0).astype(F32)
    vals, idxs, sels = [], [], []
    for _ in range(TOP_K):
        mx = jnp.max(logits, axis=0, keepdims=True)
        ix = jnp.min(jnp.where(logits == mx, eio, float(ne)), axis=0, keepdims=True)
        sel = eio == ix
        logits = jnp.where(sel, -jnp.inf, logits)
        vals.append(mx)
        idxs.append(ix)
        sels.append(sel)
    exps = [jnp.exp(v - vals[0]) for v in vals]
    den = functools.reduce(lambda a, b_: a + b_, exps)
    chosen = functools.reduce(jnp.logical_or, sels)
    onehot = jnp.where(chosen, 1.0, 0.0)
    before = _dot(onehot.astype(BF16), tri_ref[...]) + run[...]
    ranks = [jnp.sum(jnp.where(sel, before, 0.0), axis=0, keepdims=True) for sel in sels]
    run[...] = run[...] + jnp.sum(onehot, axis=1, keepdims=True)
    idx_ref[...] = jnp.concatenate(idxs, axis=0).astype(I32)
    p_ref[...] = jnp.concatenate([e / den for e in exps]
                                 + [jnp.zeros((p_ref.shape[0] - TOP_K, den.shape[1]), F32)], axis=0)
    rank_ref[...] = jnp.concatenate(ranks, axis=0).astype(I32)
    cnt_ref[...] = jnp.broadcast_to(run[...], cnt_ref.shape)


def _experts_kernel(be_ref, nu_ref, nx_ref, par_ref, nv_ref, xs_ref, wgu_hbm, bgu_ref, wd_hbm, bd_ref, y_ref,
                    wgu_f, wd_f, sems, wgu_b, wd_b, *, layer):
    j = pl.program_id(0)

    def weight_copies(expert, slot):
        return (pltpu.make_async_copy(wgu_hbm.at[layer, expert], wgu_f.at[slot], sems.at[0, slot]),
                pltpu.make_async_copy(wd_hbm.at[layer, expert], wd_f.at[slot], sems.at[1, slot]))

    @pl.when(j >= nu_ref[0])
    def _():
        y_ref[...] = jnp.zeros(y_ref.shape, U32)

    @pl.when(j < nu_ref[0])
    def _():
        expert, slot = be_ref[j], par_ref[j]
        new_expert = jnp.logical_or(j == 0, expert != be_ref[jnp.maximum(j - 1, 0)])

        @pl.when(j == 0)
        def _():
            for cp in weight_copies(expert, slot):
                cp.start()

        @pl.when(new_expert)
        def _():
            for cp in weight_copies(expert, slot):
                cp.wait()

            @pl.when(nx_ref[j] >= 0)
            def _():
                for cp in weight_copies(nx_ref[j], 1 - slot):
                    cp.start()

            wgu_b[...] = wgu_f[slot].astype(BF16)
            wd_b[...] = wd_f[slot].astype(BF16)

        def mlp(chunks):
            xb = _unpack_row_chunks(chunks).astype(BF16)
            gu = _dot(xb, wgu_b[...]) + bgu_ref[0, 0]
            f = gu.shape[1] // 2
            glu = jnp.minimum(gu[:, :f], SWIGLU_LIMIT)
            lin = jnp.clip(gu[:, f:], -SWIGLU_LIMIT, SWIGLU_LIMIT)
            act = glu * jax.nn.sigmoid(SWIGLU_ALPHA * glu) * (lin + 1.0)
            y = _dot(act.astype(BF16), wd_b[...]) + bd_ref[0, 0]
            _store_row_chunks(y_ref, _pack_row_chunks(y))

        bm = xs_ref.shape[1]
        n_rows = nv_ref[j]

        @pl.when(n_rows == bm)
        def _():
            mlp(_load_row_chunks(xs_ref))

        @pl.when(n_rows < bm)
        def _():
            is_token = lax.broadcasted_iota(I32, xs_ref.shape[1:], 0) < n_rows
            mlp([jnp.where(is_token, ch, jnp.uint32(0)) for ch in _load_row_chunks(xs_ref)])


def _combine_kernel(ya_ref, x_ref, m_ref, p_ref, lg_ref, lb_ref, *rest, alpha):
    o_ref = rest[-1]
    p_rows = p_ref[...]
    eye = (lax.broadcasted_iota(I32, (p_rows.shape[0], LANES), 0)
           == lax.broadcasted_iota(I32, (p_rows.shape[0], LANES), 1)).astype(F32)
    p = lax.dot_general(p_rows, eye, (((0,), (0,)), ((), ())), preferred_element_type=F32,
                        precision=HIGHEST)
    y = None
    for kk in range(TOP_K):
        yk = p[:, kk:kk + 1] * _unpack_row_chunks(_load_row_chunks(ya_ref, (kk,)))
        y = yk if y is None else y + yk
    gt = m_ref[0, 5:6, :]
    o_ref[...] = _layer_norm(alpha * x_ref[...] + (1.0 + gt) * y, lg_ref[...], lb_ref[...])


def _moe_layer(x, m, layer, router_w, router_b, w_gu, b_gu, w_down, b_down, ln_g, ln_b, alpha, cfg):
    b, s, d = x.shape
    n_tok = b * s
    ne = router_w.shape[-1]
    f2 = w_gu.shape[-1]
    ff = f2 // 2
    assert d % (2 * LANES) == 0 and w_down.shape[-2:] == (ff, d)
    rc = d // (2 * LANES)
    xf = x.reshape(n_tok, d)
    const2 = lambda i: (0, 0)

    tt = cfg["router_tt"]
    assert s % tt == 0
    tri = jnp.triu(jnp.ones((tt, tt), BF16), k=1)
    hp, idx, probs, rank, cnt = pl.pallas_call(
        _router_kernel,
        grid=(n_tok // tt,),
        in_specs=[pl.BlockSpec((tt, d), lambda i: (i, 0)),
                  pl.BlockSpec((1, N_MOD, d), lambda i: (i * tt // s, 0, 0)),
                  pl.BlockSpec((ne, d), const2),
                  pl.BlockSpec((ne, 1), const2),
                  pl.BlockSpec((tt, tt), const2)],
        out_specs=[pl.BlockSpec((rc, tt, LANES), lambda i: (0, i, 0)),
                   pl.BlockSpec((TOP_K, tt), lambda i: (0, i)),
                   pl.BlockSpec((SUBLANES, tt), lambda i: (0, i)),
                   pl.BlockSpec((TOP_K, tt), lambda i: (0, i)),
                   pl.BlockSpec((ne, LANES), const2)],
        out_shape=[jax.ShapeDtypeStruct((rc, n_tok, LANES), U32),
                   jax.ShapeDtypeStruct((TOP_K, n_tok), I32),
                   jax.ShapeDtypeStruct((SUBLANES, n_tok), F32),
                   jax.ShapeDtypeStruct((TOP_K, n_tok), I32),
                   jax.ShapeDtypeStruct((ne, LANES), F32)],
        scratch_shapes=[pltpu.VMEM((ne, 1), F32)],
        compiler_params=_cparams(1),
        name="moe_router",
    )(xf, m, router_w[layer].T, router_b[layer].reshape(ne, 1), tri)

    bm = cfg["expert_bm"]
    cap = n_tok * TOP_K + ne * bm
    nb = cap // bm
    counts = cnt[:, 0].astype(I32)
    padded = (counts + bm - 1) // bm * bm
    pad_end = jnp.cumsum(padded)
    pad_start = pad_end - padded
    eids = jnp.arange(ne, dtype=I32)[:, None, None]
    dest = rank + jnp.sum(jnp.where(idx[None] == eids, pad_start[:, None, None], 0), axis=0)
    block_start = jnp.arange(nb, dtype=I32) * bm
    block_expert = jnp.minimum(jnp.sum(block_start[None, :] >= pad_end[:, None], axis=0), ne - 1).astype(I32)
    n_used = (pad_end[-1:] // bm).astype(I32)
    first_of_expert = jnp.concatenate([jnp.ones((1,), bool), block_expert[1:] != block_expert[:-1]])
    weight_slot = ((jnp.cumsum(first_of_expert.astype(I32)) - 1) % 2).astype(I32)
    next_block = pad_end[block_expert] // bm
    next_expert = jnp.where(next_block < n_used[0], block_expert[jnp.minimum(next_block, nb - 1)], -1).astype(I32)

    plane_offset = (jnp.arange(rc * n_tok, dtype=I32) // n_tok * cap)[None, :]
    slot_rows = [jnp.tile(dest[kk:kk + 1], (1, rc)) + plane_offset for kk in range(TOP_K)]

    xs = _sc_scatter_rows(hp.reshape(rc * n_tok, LANES), slot_rows, rc * cap).reshape(rc, cap, LANES)

    last = lambda j, be, nu, *_: jnp.minimum(j, nu[0] - 1)
    bias_map = lambda j, be, nu, *_: (layer, be[last(j, be, nu)], 0, 0)
    block_tokens = jnp.clip(pad_start[block_expert] + counts[block_expert] - block_start, 0, bm).astype(I32)
    y_rows = pl.pallas_call(
        functools.partial(_experts_kernel, layer=layer),
        grid_spec=pltpu.PrefetchScalarGridSpec(
            num_scalar_prefetch=5,
            grid=(nb,),
            in_specs=[pl.BlockSpec((rc, bm, LANES), lambda j, be, nu, *_: (0, last(j, be, nu), 0)),
                      pl.BlockSpec(memory_space=pl.ANY),
                      pl.BlockSpec((1, 1, 1, f2), bias_map),
                      pl.BlockSpec(memory_space=pl.ANY),
                      pl.BlockSpec((1, 1, 1, d), bias_map)],
            out_specs=pl.BlockSpec((rc, bm, LANES), lambda j, *_: (0, j, 0)),
            scratch_shapes=[pltpu.VMEM((2, d, f2), F32), pltpu.VMEM((2, ff, d), F32),
                            pltpu.SemaphoreType.DMA((2, 2)),
                            pltpu.VMEM((d, f2), BF16), pltpu.VMEM((ff, d), BF16)]),
        out_shape=jax.ShapeDtypeStruct((rc, cap, LANES), U32),
        compiler_params=_cparams(1),
        name="moe_experts",
    )(block_expert, n_used, next_expert, weight_slot, block_tokens, xs, w_gu, b_gu.reshape(*b_gu.shape[:2], 1, f2),
      w_down, b_down.reshape(*b_down.shape[:2], 1, d))

    tt = cfg["combine_tt"]
    n_groups = cfg["combine_groups"] if n_tok % (cfg["combine_groups"] * max(tt, SC_WINDOW * SC_WORKERS)) == 0 else 1
    n_grp = n_tok // n_groups
    tiles = n_grp // tt
    assert s % tt == 0 and n_grp % tt == 0
    y_table = y_rows.reshape(rc * cap, LANES)
    grp_rows = (dest.reshape(TOP_K, n_groups, n_grp).transpose(1, 0, 2)[:, :, None, :]
                + (jnp.arange(rc, dtype=I32) * cap)[None, None, :, None])
    out = None
    for gi in range(n_groups):
        rows_g = grp_rows[gi].reshape(1, TOP_K * rc * n_grp)
        y_assign = _sc_gather_rows(y_table, rows_g).reshape(TOP_K, rc, n_grp, LANES)
        first = gi * tiles
        operands = [y_assign, xf, m, probs, ln_g.reshape(1, d), ln_b.reshape(1, d)]
        in_specs = [pl.BlockSpec((TOP_K, rc, tt, LANES), lambda i: (0, 0, i, 0)),
                    pl.BlockSpec((tt, d), lambda i, first=first: (first + i, 0)),
                    pl.BlockSpec((1, N_MOD, d), lambda i, first=first: ((first + i) * tt // s, 0, 0)),
                    pl.BlockSpec((SUBLANES, tt), lambda i, first=first: (0, first + i)),
                    pl.BlockSpec((1, d), const2),
                    pl.BlockSpec((1, d), const2)]
        aliases = {}
        if out is not None:
            operands.append(out)
            in_specs.append(pl.BlockSpec(memory_space=pl.ANY))
            aliases = {len(operands) - 1: 0}
        out = pl.pallas_call(
            functools.partial(_combine_kernel, alpha=alpha),
            grid=(tiles,),
            in_specs=in_specs,
            out_specs=pl.BlockSpec((tt, d), lambda i, first=first: (first + i, 0)),
            out_shape=jax.ShapeDtypeStruct((n_tok, d), F32),
            input_output_aliases=aliases,
            compiler_params=_cparams(1),
            name="moe_combine",
        )(*operands)
    return out.reshape(b, s, d)


def kernel(x, c, ada_w, ada_b, ln_g, ln_b, pool_w, pool_scale, dn_w_in, dn_conv_w, dn_a_log, dn_dt_bias,
           dn_norm_w, dn_w_out, router_w, router_b, exp_w_gu, exp_b_gu, exp_w_down, exp_b_down):
    b, s, d = x.shape
    depth = ada_w.shape[0]
    alpha = (2 * depth) ** 0.25
    cfg = _tile_config(s, b * s)
    mods = _mods(c, ada_w, ada_b)
    n_mixers = 2
    for i in range(depth):
        m = mods[i]
        j = i // n_mixers
        if i % n_mixers == 0:
            x = _pool_layer(x, m, pool_w[j], pool_scale[j], ln_g[i, 0], ln_b[i, 0], alpha, cfg)
        else:
            x = _deltanet_layer(x, m, dn_w_in[j], dn_conv_w[j], dn_a_log[j], dn_dt_bias[j], dn_norm_w[j],
                                dn_w_out[j], ln_g[i, 0], ln_b[i, 0], alpha, cfg)
        x = _moe_layer(x, m, i, router_w, router_b, exp_w_gu, exp_b_gu, exp_w_down, exp_b_down,
                       ln_g[i, 1], ln_b[i, 1], alpha, cfg)
    return x
```

```python
import functools

import jax
import jax.numpy as jnp
from jax import lax
from jax.experimental import pallas as pl
from jax.experimental.pallas import tpu as pltpu
from jax.experimental.pallas import tpu_sc as plsc

F32 = jnp.float32
BF16 = jnp.bfloat16
I32 = jnp.int32
U32 = jnp.uint32

N_MOD = 6
POOL_WINDOWS = (2, 4, 8, 16)
DN_HEADS = 8
DN_HEAD_DIM = 128
DN_CONV = 4
TOP_K = 4
SWIGLU_LIMIT = 7.0
SWIGLU_ALPHA = 1.702
LN_EPS = 1e-5
RMS_EPS = 1e-6

LANES = 128
SUBLANES = 8
VMEM_LIMIT_BYTES = 56 * 1024 * 1024
SC_WORKERS = 32
SC_WINDOW = 128

DN_CHUNK = LANES
DN_HEAD_GROUP = 8
POOL_HALO = 32
CONV_HALO = SUBLANES

HIGHEST = lax.Precision.HIGHEST


def _tile_config(seq, n_tok):
    return dict(
        pool_ts=min(512, seq),
        dn_in_ts=min(512, seq),
        dn_in_sub=256,
        dn_core_ts=min(512, seq),
        dn_out_ts=min(512, seq),
        router_tt=min(1024, seq),
        expert_bm=512,
        combine_tt=min(256, seq),
        combine_group_sixteenths=(1, 3, 4, 4, 4),
    )


def _cparams(n_axes):
    return pltpu.CompilerParams(dimension_semantics=("arbitrary",) * n_axes,
                                vmem_limit_bytes=VMEM_LIMIT_BYTES)


def _layer_norm(v, g, b):
    mu = jnp.mean(v, axis=-1, keepdims=True)
    d = v - mu
    var = jnp.mean(d * d, axis=-1, keepdims=True)
    return d * lax.rsqrt(var + LN_EPS) * g + b


def _dot(a, b):
    return jnp.dot(a, b, preferred_element_type=F32)


def _dot_nt(a, b):
    return lax.dot_general(a, b, (((1,), (1,)), ((), ())), preferred_element_type=F32)


def _dot_tn(a, b):
    return lax.dot_general(a, b, (((0,), (0,)), ((), ())), preferred_element_type=F32)


def _bdot(a, b):
    return lax.dot_general(a, b, (((2,), (1,)), ((0,), (0,))), preferred_element_type=F32)


def _bdot_nt(a, b):
    return lax.dot_general(a, b, (((2,), (2,)), ((0,), (0,))), preferred_element_type=F32)


def _mods_kernel(c_ref, w_ref, b_ref, o_ref):
    c = c_ref[...]
    c_act = c * jax.nn.sigmoid(c)
    o_ref[0] = jnp.dot(c_act, w_ref[0], preferred_element_type=F32, precision=HIGHEST) + b_ref[0]


def _mods(c, ada_w, ada_b):
    depth, d, n = ada_w.shape
    b = c.shape[0]
    tn = 2048 if n % 2048 == 0 else n
    out = pl.pallas_call(
        _mods_kernel,
        grid=(depth, n // tn),
        in_specs=[pl.BlockSpec((b, d), lambda i, j: (0, 0)),
                  pl.BlockSpec((1, d, tn), lambda i, j: (i, 0, j)),
                  pl.BlockSpec((1, 1, tn), lambda i, j: (i, 0, j))],
        out_specs=pl.BlockSpec((1, b, tn), lambda i, j: (i, 0, j)),
        out_shape=jax.ShapeDtypeStruct((depth, b, n), F32),
        compiler_params=_cparams(2),
        name="adaln_mods",
    )(c, ada_w, ada_b.reshape(depth, 1, n))
    return out.reshape(depth, b, N_MOD, d)


def _pool_kernel(x_ref, m_ref, pw_ref, ps_ref, lg_ref, lb_ref, o_ref, e1, ea, eb, *, ts, alpha):
    s = pl.program_id(1)
    d = x_ref.shape[-1]
    gw = d // len(POOL_WINDOWS)
    halo = POOL_HALO
    rows = halo + ts
    x = x_ref[0]
    sh, sc, gt = m_ref[0, 0:1, :], m_ref[0, 1:2, :], m_ref[0, 2:3, :]
    h = x * (1.0 + sc) + sh

    @pl.when(s == 0)
    def _():
        e1[0:halo, :] = jnp.zeros((halo, d), F32)

    e1[halo:rows, :] = h
    ea[8:rows, :] = e1[8:rows, :] + e1[7:rows - 1, :]
    eb[16:rows, gw:] = ea[16:rows, gw:] + ea[14:rows - 2, gw:]
    ea[24:rows, 2 * gw:] = eb[24:rows, 2 * gw:] + eb[20:rows - 4, 2 * gw:]
    eb[32:rows, 3 * gw:] = ea[32:rows, 3 * gw:] + ea[24:rows - 8, 3 * gw:]

    pos = s * ts + lax.broadcasted_iota(I32, (ts, 1), 0)
    outs = []
    for g, win in enumerate(POOL_WINDOWS):
        src = ea if g % 2 == 0 else eb
        cols = slice(g * gw, (g + 1) * gw)
        cnt = jnp.minimum(pos + 1, win).astype(F32)
        pooled = src[halo:rows, cols] / cnt - h[:, cols]
        outs.append(_dot(pooled.astype(BF16), pw_ref[g]))
    y = jnp.concatenate(outs, axis=1) * ps_ref[...]
    o_ref[0] = _layer_norm(alpha * x + (1.0 + gt) * y, lg_ref[...], lb_ref[...])
    e1[0:halo, :] = e1[ts:rows, :]


def _pool_layer(x, m, pool_w, pool_scale, ln_g, ln_b, alpha, cfg):
    b, s, d = x.shape
    ts = cfg["pool_ts"]
    g, gw, _ = pool_w.shape
    assert POOL_WINDOWS == (2, 4, 8, 16) and g == len(POOL_WINDOWS) and s % ts == 0 and ts >= POOL_HALO
    row = lambda v: v.reshape(1, d)
    return pl.pallas_call(
        functools.partial(_pool_kernel, ts=ts, alpha=alpha),
        grid=(b, s // ts),
        in_specs=[pl.BlockSpec((1, ts, d), lambda i, j: (i, j, 0)),
                  pl.BlockSpec((1, N_MOD, d), lambda i, j: (i, 0, 0)),
                  pl.BlockSpec((g, gw, gw), lambda i, j: (0, 0, 0)),
                  pl.BlockSpec((1, d), lambda i, j: (0, 0)),
                  pl.BlockSpec((1, d), lambda i, j: (0, 0)),
                  pl.BlockSpec((1, d), lambda i, j: (0, 0))],
        out_specs=pl.BlockSpec((1, ts, d), lambda i, j: (i, j, 0)),
        out_shape=jax.ShapeDtypeStruct((b, s, d), F32),
        scratch_shapes=[pltpu.VMEM((POOL_HALO + ts, d), F32)] * 3,
        compiler_params=_cparams(2),
        name="pool_layer",
    )(x, m, pool_w.astype(BF16), row(pool_scale), row(ln_g), row(ln_b))


def _dn_in_kernel(x_ref, m_ref, wm_ref, ws_ref, cw_ref, av_ref, dv_ref,
                  q_ref, k_ref, v_ref, z_ref, bg_ref, ext, *, ts, sr):
    s = pl.program_id(1)
    nh, dh = DN_HEADS, DN_HEAD_DIM
    w = nh * dh
    halo = CONV_HALO
    sh, sc = m_ref[0, 0:1, :], m_ref[0, 1:2, :]

    @pl.when(s == 0)
    def _():
        ext[0:halo, :] = jnp.zeros((halo, 3 * w), F32)

    def project(i):
        rows = slice(i * sr, (i + 1) * sr)
        h = (x_ref[0, rows, :] * (1.0 + sc) + sh).astype(BF16)
        proj = _dot(h, wm_ref[...])
        ext[halo + i * sr:halo + (i + 1) * sr, :] = proj[:, :3 * w]
        z_ref[0, rows, :] = proj[:, 3 * w:]
        small = _dot(h, ws_ref[...])
        lane = lax.broadcasted_iota(I32, small.shape, 1)
        beta = jax.nn.sigmoid(small)
        g = -jnp.exp(av_ref[...]) * jax.nn.softplus(small + dv_ref[...])
        bg_ref[0, rows, :] = jnp.where(lane < nh, beta, jnp.where(lane < 2 * nh, g, 0.0))

    def mix(i):
        rows = slice(i * sr, (i + 1) * sr)
        base = halo - (DN_CONV - 1) + i * sr
        conv = cw_ref[0:1, :] * ext[base:base + sr, :]
        for j in range(1, DN_CONV):
            conv = conv + cw_ref[j:j + 1, :] * ext[base + j:base + j + sr, :]
        act = conv * jax.nn.sigmoid(conv)
        for hd in range(nh):
            qh = act[:, hd * dh:(hd + 1) * dh]
            kh = act[:, w + hd * dh:w + (hd + 1) * dh]
            q_ref[0, hd, rows, :] = (qh * lax.rsqrt(jnp.sum(qh * qh, axis=-1, keepdims=True) + RMS_EPS)
                                     * (dh ** -0.5))
            k_ref[0, hd, rows, :] = kh * lax.rsqrt(jnp.sum(kh * kh, axis=-1, keepdims=True) + RMS_EPS)
            v_ref[0, hd, rows, :] = act[:, 2 * w + hd * dh:2 * w + (hd + 1) * dh]

    n_sub = ts // sr
    project(0)
    for i in range(1, n_sub):
        project(i)
        mix(i - 1)
    mix(n_sub - 1)
    ext[0:halo, :] = ext[ts:ts + halo, :]


def _dn_core_kernel(q_ref, k_ref, v_ref, bg_ref, o_ref,
                    st, bb, gcb, gt, gl_s, u_s, wq_s, ai_s, kd_s, *, ts):
    s = pl.program_id(1)
    nh, dh, c = DN_HEADS, DN_HEAD_DIM, DN_CHUNK
    nc = ts // c

    @pl.when(s == 0)
    def _():
        st[...] = jnp.zeros(st.shape, F32)

    bgv = bg_ref[0]
    lane = lax.broadcasted_iota(I32, bgv.shape, 1)
    rowc = lax.broadcasted_iota(I32, bgv.shape, 0) % c
    gc = jnp.where(lane >= nh, bgv, 0.0)
    shift = 1
    while shift < c:
        gc = gc + jnp.where(rowc >= shift, pltpu.roll(gc, shift, 0), 0.0)
        shift *= 2
    gc_t = gc.T
    for hd in range(nh):
        for ci in range(nc):
            gt[hd * nc + ci] = jnp.broadcast_to(gc_t[nh + hd:nh + hd + 1, ci * c:(ci + 1) * c], (SUBLANES, c))
        bb[hd] = jnp.broadcast_to(bgv[:, hd:hd + 1], (ts, LANES))
        gcb[hd] = jnp.broadcast_to(gc[:, nh + hd:nh + hd + 1], (ts, LANES))

    ri = lax.broadcasted_iota(I32, (c, c), 0)
    ci_ = lax.broadcasted_iota(I32, (c, c), 1)
    tril = ri >= ci_
    eye = (ri == ci_).astype(F32)
    off_masks = []
    blk = 1
    while blk < c:
        off_masks.append((ri // blk != ci_ // blk) & (ri // (2 * blk) == ci_ // (2 * blk)) & (ri > ci_))
        blk *= 2

    hg = DN_HEAD_GROUP
    nb = hg * nc
    n_pairs = nb // 2
    two = lambda msk: jnp.concatenate([msk, msk], axis=1)
    tril2, eye2, off2 = two(tril), two(eye), [two(msk) for msk in off_masks]

    def pair(a):
        a = a.reshape(n_pairs, 2, a.shape[1], a.shape[2])
        return jnp.concatenate([a[:, 0], a[:, 1]], axis=-1)

    def unpair(a):
        w = a.shape[-1] // 2
        return jnp.stack([a[..., :w], a[..., w:]], axis=1).reshape(nb, a.shape[1], w)

    def block_diag(a):
        left = lax.broadcasted_iota(I32, a.shape[1:], 1) < a.shape[-1] // 2
        zero = jnp.zeros_like(a)
        return jnp.concatenate([jnp.where(left, a, zero), jnp.where(left, zero, a)], axis=1)

    def group_body(gi, carry):
        heads = pl.ds(gi * hg, hg)
        chunks = lambda a: a.reshape(nb, c, a.shape[-1])
        per_head = lambda a: a.reshape(hg, ts, a.shape[-1])
        q, k, v = chunks(q_ref[0, heads]), chunks(k_ref[0, heads]), chunks(v_ref[0, heads])
        beta, gch = chunks(bb[heads]), chunks(gcb[heads])
        egh = jnp.exp(gch)
        g_last = jnp.broadcast_to(gch[:, c - 1:c, :], gch.shape)
        kb = k * beta
        vb = v * beta
        kd_s[heads] = per_head((k * jnp.exp(g_last - gch)).astype(BF16))
        gl_s[heads] = jnp.broadcast_to(egh[:, c - 1:c, :], (nb, SUBLANES, LANES)).reshape(hg, nc * SUBLANES, LANES)
        grow = gt[pl.ds(gi * nb, nb), 0:1, :]
        dec = jnp.where(tril2, jnp.exp(jnp.where(tril2, pair(gch) - pair(grow), 0.0)), 0.0)
        kq = jnp.concatenate([pair(kb), pair(q)], axis=1).astype(BF16)
        a_all = _bdot_nt(kq, block_diag(pair(k).astype(BF16)))
        ai_s[heads] = per_head(unpair((a_all[:, c:] * dec).astype(BF16)))
        l_mat = a_all[:, :c] * dec
        t_inv = eye2 - jnp.where(off2[0], l_mat, 0.0)
        for off in off2[1:]:
            tb = t_inv.astype(BF16)
            l_off = jnp.where(off, l_mat, 0.0).astype(BF16)
            t_inv = t_inv - _bdot(tb, block_diag(_bdot(l_off, block_diag(tb)).astype(BF16)))
        rhs = jnp.concatenate([vb, kb * egh], axis=2).astype(BF16)
        uw = unpair(_bdot(t_inv.astype(BF16), block_diag(pair(rhs))))
        u_s[heads] = per_head(uw[:, :, :dh])
        wq = jnp.concatenate([uw[:, :, dh:], q * egh], axis=1).astype(BF16)
        wq_s[heads] = wq.reshape(hg, 2 * ts, dh)
        return carry

    lax.fori_loop(0, nh // hg, group_body, 0)

    def chunk_body(ci, carry):
        rows = pl.ds(pl.multiple_of(ci * c, c), c)
        rows2 = pl.ds(pl.multiple_of(ci * 2 * c, 2 * c), 2 * c)
        heads = range(nh)
        states = [st[hd] for hd in heads]
        ws_qs = [_dot(wq_s[hd, rows2, :], states[hd].astype(BF16)) for hd in heads]
        v_nb = [(u_s[hd, rows, :] - ws_qs[hd][:c]).astype(BF16) for hd in heads]
        outs = [ws_qs[hd][c:] + _dot(ai_s[hd, rows, :], v_nb[hd]) for hd in heads]
        new_states = [states[hd] * gl_s[hd, pl.ds(pl.multiple_of(ci * SUBLANES, SUBLANES), 1), :]
                      + _dot_tn(kd_s[hd, rows, :], v_nb[hd]) for hd in heads]
        for hd in heads:
            o_ref[0, hd, rows, :] = outs[hd]
            st[hd] = new_states[hd]
        return carry

    lax.fori_loop(0, nc, chunk_body, 0)


def _dn_out_kernel(o_ref, z_ref, x_ref, m_ref, nw_ref, wo_ref, lg_ref, lb_ref, out_ref, *, alpha):
    nh, dh = DN_HEADS, DN_HEAD_DIM
    z = z_ref[0]
    parts = []
    for hd in range(nh):
        oh = o_ref[0, hd]
        on = oh * lax.rsqrt(jnp.mean(oh * oh, axis=-1, keepdims=True) + RMS_EPS) * nw_ref[...]
        zh = z[:, hd * dh:(hd + 1) * dh]
        parts.append((on * (zh * jax.nn.sigmoid(zh))).astype(BF16))
    y = _dot(jnp.concatenate(parts, axis=1), wo_ref[...])
    x = x_ref[0]
    gt = m_ref[0, 2:3, :]
    out_ref[0] = _layer_norm(alpha * x + (1.0 + gt) * y, lg_ref[...], lb_ref[...])


def _deltanet_layer(x, m, w_in, conv_w, a_log, dt_bias, norm_w, w_out, ln_g, ln_b, alpha, cfg):
    b, s, d = x.shape
    nh, dh = DN_HEADS, DN_HEAD_DIM
    w = nh * dh
    assert w_in.shape == (d, 4 * w + 2 * nh) and conv_w.shape == (DN_CONV, 3 * w) and 2 * nh <= LANES
    w_main = w_in[:, :4 * w].astype(BF16)
    w_small = jnp.pad(w_in[:, 4 * w:], ((0, 0), (0, LANES - 2 * nh))).astype(BF16)
    avec = jnp.zeros((1, LANES), F32).at[0, nh:2 * nh].set(a_log.astype(F32))
    dvec = jnp.zeros((1, LANES), F32).at[0, nh:2 * nh].set(dt_bias.astype(F32))
    row = lambda v: v.reshape(1, -1)
    const2 = lambda i, j: (0, 0)

    ts = cfg["dn_in_ts"]
    assert s % ts == 0 and ts >= CONV_HALO
    head_major = jax.ShapeDtypeStruct((b, nh, s, dh), F32)
    hm_spec = lambda t: pl.BlockSpec((1, nh, t, dh), lambda i, j: (i, 0, j, 0))
    q, k, v, z, bg = pl.pallas_call(
        functools.partial(_dn_in_kernel, ts=ts, sr=min(cfg["dn_in_sub"], ts)),
        grid=(b, s // ts),
        in_specs=[pl.BlockSpec((1, ts, d), lambda i, j: (i, j, 0)),
                  pl.BlockSpec((1, N_MOD, d), lambda i, j: (i, 0, 0)),
                  pl.BlockSpec((d, 4 * w), const2),
                  pl.BlockSpec((d, LANES), const2),
                  pl.BlockSpec((DN_CONV, 3 * w), const2),
                  pl.BlockSpec((1, LANES), const2),
                  pl.BlockSpec((1, LANES), const2)],
        out_specs=[hm_spec(ts), hm_spec(ts), hm_spec(ts),
                   pl.BlockSpec((1, ts, w), lambda i, j: (i, j, 0)),
                   pl.BlockSpec((1, ts, LANES), lambda i, j: (i, j, 0))],
        out_shape=[head_major, head_major, head_major,
                   jax.ShapeDtypeStruct((b, s, w), F32),
                   jax.ShapeDtypeStruct((b, s, LANES), F32)],
        scratch_shapes=[pltpu.VMEM((CONV_HALO + ts, 3 * w), F32)],
        compiler_params=_cparams(2),
        name="deltanet_in",
    )(x, m, w_main, w_small, conv_w, avec, dvec)

    ts = cfg["dn_core_ts"]
    assert s % ts == 0 and ts % DN_CHUNK == 0 and dh == DN_CHUNK
    per_head = pltpu.VMEM((nh, ts, dh), F32)
    o = pl.pallas_call(
        functools.partial(_dn_core_kernel, ts=ts),
        grid=(b, s // ts),
        in_specs=[hm_spec(ts), hm_spec(ts), hm_spec(ts),
                  pl.BlockSpec((1, ts, LANES), lambda i, j: (i, j, 0))],
        out_specs=hm_spec(ts),
        out_shape=head_major,
        scratch_shapes=[pltpu.VMEM((nh, dh, dh), F32),
                        per_head, per_head,
                        pltpu.VMEM((nh * (ts // DN_CHUNK), SUBLANES, DN_CHUNK), F32),
                        pltpu.VMEM((nh, ts // DN_CHUNK * SUBLANES, LANES), F32),
                        per_head,
                        pltpu.VMEM((nh, 2 * ts, dh), BF16),
                        pltpu.VMEM((nh, ts, DN_CHUNK), BF16),
                        pltpu.VMEM((nh, ts, dh), BF16)],
        compiler_params=_cparams(2),
        name="deltanet_core",
    )(q, k, v, bg)

    ts = cfg["dn_out_ts"]
    assert s % ts == 0
    return pl.pallas_call(
        functools.partial(_dn_out_kernel, alpha=alpha),
        grid=(b, s // ts),
        in_specs=[hm_spec(ts),
                  pl.BlockSpec((1, ts, w), lambda i, j: (i, j, 0)),
                  pl.BlockSpec((1, ts, d), lambda i, j: (i, j, 0)),
                  pl.BlockSpec((1, N_MOD, d), lambda i, j: (i, 0, 0)),
                  pl.BlockSpec((1, dh), const2),
                  pl.BlockSpec((w, d), const2),
                  pl.BlockSpec((1, d), const2),
                  pl.BlockSpec((1, d), const2)],
        out_specs=pl.BlockSpec((1, ts, d), lambda i, j: (i, j, 0)),
        out_shape=jax.ShapeDtypeStruct((b, s, d), F32),
        compiler_params=_cparams(2),
        name="deltanet_out",
    )(o, z, x, m, row(norm_w), w_out.astype(BF16), row(ln_g), row(ln_b))


def _pack_row_chunks(v):
    half = v.shape[-1] // 2
    bits = pltpu.bitcast(v.astype(BF16).astype(F32), U32)
    packed = (bits[:, :half] >> 16) | (bits[:, half:] & jnp.uint32(0xFFFF0000))
    return [packed[:, i * LANES:(i + 1) * LANES] for i in range(half // LANES)]


def _unpack_row_chunks(chunks):
    lo = [pltpu.bitcast(c << 16, F32) for c in chunks]
    hi = [pltpu.bitcast(c & jnp.uint32(0xFFFF0000), F32) for c in chunks]
    return jnp.concatenate(lo + hi, axis=1)


def _store_row_chunks(ref, chunks):
    for i, ch in enumerate(chunks):
        ref[i] = ch


def _load_row_chunks(ref, lead=()):
    return [ref[(*lead, i)] for i in range(ref.shape[len(lead)])]


def _sc_mesh():
    return plsc.VectorSubcoreMesh(core_axis_name="c", subcore_axis_name="s")


def _sc_gather_rows(table, idx):
    n = idx.shape[1]
    assert table.shape[1] == LANES and idx.shape[0] == 1 and n % (SC_WINDOW * SC_WORKERS) == 0

    @functools.partial(pl.kernel, out_type=jax.ShapeDtypeStruct((n, LANES), table.dtype), mesh=_sc_mesh(),
                       name="sc_gather_rows")
    def gather(table_hbm, idx_hbm, out_hbm):
        def body(idx_vmem, out_vmem):
            pltpu.sync_copy(table_hbm.at[idx_vmem.at[0]], out_vmem)

        pltpu.emit_pipeline(
            body,
            grid=(n // SC_WINDOW,),
            in_specs=[pl.BlockSpec((1, SC_WINDOW), lambda i: (0, i))],
            out_specs=[pl.BlockSpec((SC_WINDOW, LANES), lambda i: (i, 0))],
            core_axis_name=("c", "s"),
            dimension_semantics=(pltpu.PARALLEL,),
        )(idx_hbm, out_hbm)

    return gather(table, idx)


def _sc_scatter_rows(x, idx_list, n_out):
    n = x.shape[0]
    assert x.shape[1] == LANES and n % (SC_WINDOW * SC_WORKERS) == 0
    assert all(idx.shape == (1, n) for idx in idx_list)

    @functools.partial(pl.kernel, out_type=jax.ShapeDtypeStruct((n_out, LANES), x.dtype), mesh=_sc_mesh(),
                       name="sc_scatter_rows")
    def scatter(x_hbm, *refs):
        idx_hbms, out_hbm = refs[:-1], refs[-1]

        def body(x_vmem, *idx_vmems):
            for idx_vmem in idx_vmems:
                pltpu.sync_copy(x_vmem, out_hbm.at[idx_vmem.at[0]])

        pltpu.emit_pipeline(
            body,
            grid=(n // SC_WINDOW,),
            in_specs=[pl.BlockSpec((SC_WINDOW, LANES), lambda i: (i, 0))]
                     + [pl.BlockSpec((1, SC_WINDOW), lambda i: (0, i))] * len(idx_list),
            out_specs=[],
            core_axis_name=("c", "s"),
            dimension_semantics=(pltpu.PARALLEL,),
        )(x_hbm, *idx_hbms)

    return scatter(x, *idx_list)


def _router_kernel(x_ref, m_ref, rwt_ref, rb_ref, tri_ref,
                   hp_ref, idx_ref, p_ref, rank_ref, cnt_ref, run):
    i = pl.program_id(0)

    @pl.when(i == 0)
    def _():
        run[...] = jnp.zeros(run.shape, F32)

    x = x_ref[...]
    d = x.shape[-1]
    sh, sc = m_ref[0, 3:4, :], m_ref[0, 4:5, :]
    h = x * (1.0 + sc) + sh
    _store_row_chunks(hp_ref, _pack_row_chunks(h))

    h_hi = h.astype(BF16)
    h_lo = (h - h_hi.astype(F32)).astype(BF16)
    rw = rwt_ref[...]
    w_hi = rw.astype(BF16)
    w_lo = (rw - w_hi.astype(F32)).astype(BF16)
    ne = rw.shape[0]
    hi_terms = _dot_nt(jnp.concatenate([w_hi, w_lo], axis=0), h_hi)
    logits = hi_terms[:ne] + hi_terms[ne:] + _dot_nt(w_hi, h_lo) + rb_ref[...]
    eio = lax.broadcasted_iota(I32, logits.shape, 0).astype(F32)
    vals, idxs, sels = [], [], []
    for _ in range(TOP_K):
        mx = jnp.max(logits, axis=0, keepdims=True)
        ix = jnp.min(jnp.where(logits == mx, eio, float(ne)), axis=0, keepdims=True)
        sel = eio == ix
        logits = jnp.where(sel, -jnp.inf, logits)
        vals.append(mx)
        idxs.append(ix)
        sels.append(sel)
    exps = [jnp.exp(v - vals[0]) for v in vals]
    den = functools.reduce(lambda a, b_: a + b_, exps)
    chosen = functools.reduce(jnp.logical_or, sels)
    onehot = jnp.where(chosen, 1.0, 0.0)
    before = _dot(onehot.astype(BF16), tri_ref[...]) + run[...]
    ranks = [jnp.sum(jnp.where(sel, before, 0.0), axis=0, keepdims=True) for sel in sels]
    run[...] = run[...] + jnp.sum(onehot, axis=1, keepdims=True)
    idx_ref[...] = jnp.concatenate(idxs, axis=0).astype(I32)
    p_ref[...] = jnp.concatenate([e / den for e in exps]
                                 + [jnp.zeros((p_ref.shape[0] - TOP_K, den.shape[1]), F32)], axis=0)
    rank_ref[...] = jnp.concatenate(ranks, axis=0).astype(I32)
    cnt_ref[...] = jnp.broadcast_to(run[...], cnt_ref.shape)


def _experts_kernel(be_ref, nu_ref, nx_ref, par_ref, nv_ref, xs_ref, wgu_hbm, bgu_ref, wd_hbm, bd_ref, y_ref,
                    wgu_f, wd_f, sems, wgu_b, wd_b, *, layer):
    j = pl.program_id(0)

    def weight_copies(expert, slot):
        return (pltpu.make_async_copy(wgu_hbm.at[layer, expert], wgu_f.at[slot], sems.at[0, slot]),
                pltpu.make_async_copy(wd_hbm.at[layer, expert], wd_f.at[slot], sems.at[1, slot]))

    @pl.when(j >= nu_ref[0])
    def _():
        y_ref[...] = jnp.zeros(y_ref.shape, U32)

    @pl.when(j < nu_ref[0])
    def _():
        expert, slot = be_ref[j], par_ref[j]
        new_expert = jnp.logical_or(j == 0, expert != be_ref[jnp.maximum(j - 1, 0)])

        @pl.when(j == 0)
        def _():
            for cp in weight_copies(expert, slot):
                cp.start()

        @pl.when(new_expert)
        def _():
            for cp in weight_copies(expert, slot):
                cp.wait()

            @pl.when(nx_ref[j] >= 0)
            def _():
                for cp in weight_copies(nx_ref[j], 1 - slot):
                    cp.start()

            wgu_b[...] = wgu_f[slot].astype(BF16)
            wd_b[...] = wd_f[slot].astype(BF16)

        def mlp(chunks):
            xb = _unpack_row_chunks(chunks).astype(BF16)
            gu = _dot(xb, wgu_b[...]) + bgu_ref[0, 0]
            f = gu.shape[1] // 2
            glu = jnp.minimum(gu[:, :f], SWIGLU_LIMIT)
            lin = jnp.clip(gu[:, f:], -SWIGLU_LIMIT, SWIGLU_LIMIT)
            act = glu * jax.nn.sigmoid(SWIGLU_ALPHA * glu) * (lin + 1.0)
            y = _dot(act.astype(BF16), wd_b[...]) + bd_ref[0, 0]
            _store_row_chunks(y_ref, _pack_row_chunks(y))

        bm = xs_ref.shape[1]
        n_rows = nv_ref[j]

        @pl.when(n_rows == bm)
        def _():
            mlp(_load_row_chunks(xs_ref))

        @pl.when(n_rows < bm)
        def _():
            is_token = lax.broadcasted_iota(I32, xs_ref.shape[1:], 0) < n_rows
            mlp([jnp.where(is_token, ch, jnp.uint32(0)) for ch in _load_row_chunks(xs_ref)])


def _combine_kernel(ya_ref, x_ref, m_ref, p_ref, lg_ref, lb_ref, *rest, alpha):
    o_ref = rest[-1]
    p_rows = p_ref[...]
    eye = (lax.broadcasted_iota(I32, (p_rows.shape[0], LANES), 0)
           == lax.broadcasted_iota(I32, (p_rows.shape[0], LANES), 1)).astype(F32)
    p = lax.dot_general(p_rows, eye, (((0,), (0,)), ((), ())), preferred_element_type=F32,
                        precision=HIGHEST)
    y = None
    for kk in range(TOP_K):
        yk = p[:, kk:kk + 1] * _unpack_row_chunks(_load_row_chunks(ya_ref, (kk,)))
        y = yk if y is None else y + yk
    gt = m_ref[0, 5:6, :]
    o_ref[...] = _layer_norm(alpha * x_ref[...] + (1.0 + gt) * y, lg_ref[...], lb_ref[...])


def _moe_layer(x, m, layer, router_w, router_b, w_gu, b_gu, w_down, b_down, ln_g, ln_b, alpha, cfg):
    b, s, d = x.shape
    n_tok = b * s
    ne = router_w.shape[-1]
    f2 = w_gu.shape[-1]
    ff = f2 // 2
    assert d % (2 * LANES) == 0 and w_down.shape[-2:] == (ff, d)
    rc = d // (2 * LANES)
    xf = x.reshape(n_tok, d)
    const2 = lambda i: (0, 0)

    tt = cfg["router_tt"]
    assert s % tt == 0
    tri = jnp.triu(jnp.ones((tt, tt), BF16), k=1)
    hp, idx, probs, rank, cnt = pl.pallas_call(
        _router_kernel,
        grid=(n_tok // tt,),
        in_specs=[pl.BlockSpec((tt, d), lambda i: (i, 0)),
                  pl.BlockSpec((1, N_MOD, d), lambda i: (i * tt // s, 0, 0)),
                  pl.BlockSpec((ne, d), const2),
                  pl.BlockSpec((ne, 1), const2),
                  pl.BlockSpec((tt, tt), const2)],
        out_specs=[pl.BlockSpec((rc, tt, LANES), lambda i: (0, i, 0)),
                   pl.BlockSpec((TOP_K, tt), lambda i: (0, i)),
                   pl.BlockSpec((SUBLANES, tt), lambda i: (0, i)),
                   pl.BlockSpec((TOP_K, tt), lambda i: (0, i)),
                   pl.BlockSpec((ne, LANES), const2)],
        out_shape=[jax.ShapeDtypeStruct((rc, n_tok, LANES), U32),
                   jax.ShapeDtypeStruct((TOP_K, n_tok), I32),
                   jax.ShapeDtypeStruct((SUBLANES, n_tok), F32),
                   jax.ShapeDtypeStruct((TOP_K, n_tok), I32),
                   jax.ShapeDtypeStruct((ne, LANES), F32)],
        scratch_shapes=[pltpu.VMEM((ne, 1), F32)],
        compiler_params=_cparams(1),
        name="moe_router",
    )(xf, m, router_w[layer].T, router_b[layer].reshape(ne, 1), tri)

    bm = cfg["expert_bm"]
    cap = n_tok * TOP_K + ne * bm
    nb = cap // bm
    counts = cnt[:, 0].astype(I32)
    padded = (counts + bm - 1) // bm * bm
    pad_end = jnp.cumsum(padded)
    pad_start = pad_end - padded
    eids = jnp.arange(ne, dtype=I32)[:, None, None]
    dest = rank + jnp.sum(jnp.where(idx[None] == eids, pad_start[:, None, None], 0), axis=0)
    block_start = jnp.arange(nb, dtype=I32) * bm
    block_expert = jnp.minimum(jnp.sum(block_start[None, :] >= pad_end[:, None], axis=0), ne - 1).astype(I32)
    n_used = (pad_end[-1:] // bm).astype(I32)
    first_of_expert = jnp.concatenate([jnp.ones((1,), bool), block_expert[1:] != block_expert[:-1]])
    weight_slot = ((jnp.cumsum(first_of_expert.astype(I32)) - 1) % 2).astype(I32)
    next_block = pad_end[block_expert] // bm
    next_expert = jnp.where(next_block < n_used[0], block_expert[jnp.minimum(next_block, nb - 1)], -1).astype(I32)

    plane_offset = (jnp.arange(rc * n_tok, dtype=I32) // n_tok * cap)[None, :]
    slot_rows = [jnp.tile(dest[kk:kk + 1], (1, rc)) + plane_offset for kk in range(TOP_K)]

    xs = _sc_scatter_rows(hp.reshape(rc * n_tok, LANES), slot_rows, rc * cap).reshape(rc, cap, LANES)

    last = lambda j, be, nu, *_: jnp.minimum(j, nu[0] - 1)
    bias_map = lambda j, be, nu, *_: (layer, be[last(j, be, nu)], 0, 0)
    block_tokens = jnp.clip(pad_start[block_expert] + counts[block_expert] - block_start, 0, bm).astype(I32)
    y_rows = pl.pallas_call(
        functools.partial(_experts_kernel, layer=layer),
        grid_spec=pltpu.PrefetchScalarGridSpec(
            num_scalar_prefetch=5,
            grid=(nb,),
            in_specs=[pl.BlockSpec((rc, bm, LANES), lambda j, be, nu, *_: (0, last(j, be, nu), 0)),
                      pl.BlockSpec(memory_space=pl.ANY),
                      pl.BlockSpec((1, 1, 1, f2), bias_map),
                      pl.BlockSpec(memory_space=pl.ANY),
                      pl.BlockSpec((1, 1, 1, d), bias_map)],
            out_specs=pl.BlockSpec((rc, bm, LANES), lambda j, *_: (0, j, 0)),
            scratch_shapes=[pltpu.VMEM((2, d, f2), F32), pltpu.VMEM((2, ff, d), F32),
                            pltpu.SemaphoreType.DMA((2, 2)),
                            pltpu.VMEM((d, f2), BF16), pltpu.VMEM((ff, d), BF16)]),
        out_shape=jax.ShapeDtypeStruct((rc, cap, LANES), U32),
        compiler_params=_cparams(1),
        name="moe_experts",
    )(block_expert, n_used, next_expert, weight_slot, block_tokens, xs, w_gu, b_gu.reshape(*b_gu.shape[:2], 1, f2),
      w_down, b_down.reshape(*b_down.shape[:2], 1, d))

    tt = cfg["combine_tt"]
    unit = n_tok // 16
    gather_quantum = SC_WINDOW * SC_WORKERS // (TOP_K * rc)
    if n_tok % 16 == 0 and unit % tt == 0 and unit % max(gather_quantum, 1) == 0:
        group_sizes = [unit * parts for parts in cfg["combine_group_sixteenths"]]
    else:
        group_sizes = [n_tok]
    assert s % tt == 0 and sum(group_sizes) == n_tok
    y_table = y_rows.reshape(rc * cap, LANES)
    plane_rows = (jnp.arange(rc, dtype=I32) * cap)[None, :, None]
    out = None
    start = 0
    for n_grp in group_sizes:
        tiles = n_grp // tt
        rows_g = (dest[:, start:start + n_grp][:, None, :] + plane_rows).reshape(1, TOP_K * rc * n_grp)
        y_assign = _sc_gather_rows(y_table, rows_g).reshape(TOP_K, rc, n_grp, LANES)
        first = start // tt
        start += n_grp
        operands = [y_assign, xf, m, probs, ln_g.reshape(1, d), ln_b.reshape(1, d)]
        in_specs = [pl.BlockSpec((TOP_K, rc, tt, LANES), lambda i: (0, 0, i, 0)),
                    pl.BlockSpec((tt, d), lambda i, first=first: (first + i, 0)),
                    pl.BlockSpec((1, N_MOD, d), lambda i, first=first: ((first + i) * tt // s, 0, 0)),
                    pl.BlockSpec((SUBLANES, tt), lambda i, first=first: (0, first + i)),
                    pl.BlockSpec((1, d), const2),
                    pl.BlockSpec((1, d), const2)]
        aliases = {}
        if out is not None:
            operands.append(out)
            in_specs.append(pl.BlockSpec(memory_space=pl.ANY))
            aliases = {len(operands) - 1: 0}
        out = pl.pallas_call(
            functools.partial(_combine_kernel, alpha=alpha),
            grid=(tiles,),
            in_specs=in_specs,
            out_specs=pl.BlockSpec((tt, d), lambda i, first=first: (first + i, 0)),
            out_shape=jax.ShapeDtypeStruct((n_tok, d), F32),
            input_output_aliases=aliases,
            compiler_params=_cparams(1),
            name="moe_combine",
        )(*operands)
    return out.reshape(b, s, d)


def kernel(x, c, ada_w, ada_b, ln_g, ln_b, pool_w, pool_scale, dn_w_in, dn_conv_w, dn_a_log, dn_dt_bias,
           dn_norm_w, dn_w_out, router_w, router_b, exp_w_gu, exp_b_gu, exp_w_down, exp_b_down):
    b, s, d = x.shape
    depth = ada_w.shape[0]
    alpha = (2 * depth) ** 0.25
    cfg = _tile_config(s, b * s)
    mods = _mods(c, ada_w, ada_b)
    n_mixers = 2
    for i in range(depth):
        m = mods[i]
        j = i // n_mixers
        if i % n_mixers == 0:
            x = _pool_layer(x, m, pool_w[j], pool_scale[j], ln_g[i, 0], ln_b[i, 0], alpha, cfg)
        else:
            x = _deltanet_layer(x, m, dn_w_in[j], dn_conv_w[j], dn_a_log[j], dn_dt_bias[j], dn_norm_w[j],
                                dn_w_out[j], ln_g[i, 0], ln_b[i, 0], alpha, cfg)
        x = _moe_layer(x, m, i, router_w, router_b, exp_w_gu, exp_b_gu, exp_w_down, exp_b_down,
                       ln_g[i, 1], ln_b[i, 1], alpha, cfg)
    return x
```
